```python
import jax
import jax.numpy as jnp
from jax import lax
import numpy as np

D_MODEL = 1024
BATCH = 16
SEQ = 256
DEPTH = 4
DEC_BATCH = 2
DEC_SEQ = 4096
PAST_LEN = 256

GRID_W = 64
N_EVEN = (DEPTH + 1) // 2
N_ODD = DEPTH // 2
C_A = D_MODEL // 2
G_A = 8
CHUNK = 128
C_B = D_MODEL // 2
CONV_K = 31
N_HEADS_C = 16
HEAD_DIM = D_MODEL // N_HEADS_C
WIN_ROWS = 8
WIN_COLS = 16
N_EXPERTS = 32
TOP_K = 4
D_EXPERT = D_MODEL
SWIGLU_ALPHA = 1.702
SWIGLU_LIMIT = 7.0
MOE_BLOCK = 128
DEEPNORM_ALPHA = (2 * DEPTH) ** 0.25
DEEPNORM_BETA = (8 * DEPTH) ** -0.25
LN_EPS = 1e-5
NEG_INF = -1e30

kernel_name = 'hybrid_gmlp_conv_natten_moe_diffusion_step'


def layer_norm(x, g, b):
    xf = x.astype(jnp.float32)
    mu = jnp.mean(xf, axis=-1, keepdims=True)
    var = jnp.mean(jnp.square(xf - mu), axis=-1, keepdims=True)
    y = (xf - mu) * lax.rsqrt(var + LN_EPS) * g.astype(jnp.float32) + b.astype(jnp.float32)
    return y.astype(x.dtype)


def adaln(cvec, w_mod_l, b_mod_l):
    m = jax.nn.silu(cvec) @ w_mod_l + b_mod_l
    return m.reshape(cvec.shape[0], 6, 1, D_MODEL)


def mixer_ab(h, w_in, sgu_g, sgu_b, w_sp, b_sp, conv_w, conv_b, cln_g, cln_b, w_out):
    bn, n, _ = h.shape
    z = h @ w_in
    u, v, a, g = jnp.split(z, [C_A, 2 * C_A, 2 * C_A + C_B], axis=-1)
    u = jax.nn.gelu(u)
    v = layer_norm(jax.nn.gelu(v), sgu_g, sgu_b)
    vc = v.reshape(bn, n // CHUNK, CHUNK, G_A, C_A // G_A)
    s = jnp.einsum('gpq,bnqgc->bnpgc', w_sp, vc) + b_sp.T[None, None, :, :, None]
    y_a = u * s.reshape(bn, n, C_A)
    gl = a * jax.nn.sigmoid(g)
    dc = lax.conv_general_dilated(
        gl, conv_w[:, None, :].astype(gl.dtype), window_strides=(1,),
        padding=[(CONV_K // 2, CONV_K // 2)], dimension_numbers=('NWC', 'WIO', 'NWC'),
        feature_group_count=C_B) + conv_b
    y_b = jax.nn.silu(layer_norm(dc, cln_g, cln_b))
    return jnp.concatenate([y_a, y_b], axis=-1) @ w_out


def attn_ctx(h, w_qkv, w_out):
    bn, n, _ = h.shape
    q, k, v = jnp.split(h @ w_qkv, 3, axis=-1)
    q = q.reshape(bn, n, N_HEADS_C, HEAD_DIM)
    k = k.reshape(bn, n, N_HEADS_C, HEAD_DIM)
    v = v.reshape(bn, n, N_HEADS_C, HEAD_DIM)
    s = jnp.einsum('bqhd,bkhd->bhqk', q, k).astype(jnp.float32) * (HEAD_DIM ** -0.5)
    p = jax.nn.softmax(s, axis=-1).astype(v.dtype)
    o = jnp.einsum('bhqk,bkhd->bqhd', p, v).reshape(bn, n, D_MODEL)
    return o @ w_out, k, v


def attn_latent(h, ck, cv, w_qkv, rpb, w_out):
    bn, n, _ = h.shape
    rows_n = n // GRID_W
    wr = min(WIN_ROWS, rows_n)
    m_blocks = GRID_W // WIN_COLS
    kbw = 2 * WIN_COLS
    q, k, v = jnp.split(h @ w_qkv, 3, axis=-1)
    q = q.reshape(bn, rows_n, m_blocks, WIN_COLS, N_HEADS_C, HEAD_DIM)
    k = k.reshape(bn, rows_n, GRID_W, N_HEADS_C, HEAD_DIM)
    v = v.reshape(bn, rows_n, GRID_W, N_HEADS_C, HEAD_DIM)
    r = jnp.arange(rows_n)
    rows = jnp.clip(r - wr // 2, 0, rows_n - wr)[:, None] + jnp.arange(wr)
    mb = jnp.arange(m_blocks)
    cols = jnp.clip(mb * WIN_COLS - WIN_COLS // 2, 0, GRID_W - kbw)[:, None] + jnp.arange(kbw)
    ridx = rows[:, None, :, None]
    cidx = cols[None, :, None, :]
    kb = k[:, ridx, cidx].reshape(bn, rows_n, m_blocks, wr * kbw, N_HEADS_C, HEAD_DIM)
    vb = v[:, ridx, cidx].reshape(bn, rows_n, m_blocks, wr * kbw, N_HEADS_C, HEAD_DIM)
    qcol = mb[:, None] * WIN_COLS + jnp.arange(WIN_COLS)
    qcs = jnp.clip(qcol - WIN_COLS // 2, 0, GRID_W - WIN_COLS)
    kc = cols[:, None, :]
    valid = (kc >= qcs[..., None]) & (kc < qcs[..., None] + WIN_COLS)
    valid = jnp.broadcast_to(valid[:, :, None, :], (m_blocks, WIN_COLS, wr, kbw))
    valid = valid.reshape(m_blocks, 1, WIN_COLS, wr * kbw)
    dc_idx = jnp.clip(kc - qcol[..., None] + WIN_COLS - 1, 0, 2 * WIN_COLS - 2)
    dr_idx = rows - r[:, None] + WIN_ROWS - 1
    bias = rpb[:, dr_idx[:, None, None, :, None], dc_idx[None, :, :, None, :]]
    bias = jnp.transpose(bias, (1, 2, 0, 3, 4, 5)).reshape(
        rows_n, m_blocks, N_HEADS_C, WIN_COLS, wr * kbw).astype(jnp.float32)
    scale = HEAD_DIM ** -0.5
    s_loc = jnp.einsum('brmqhd,brmkhd->brmhqk', q, kb).astype(jnp.float32) * scale + bias
    s_loc = jnp.where(valid, s_loc, NEG_INF)
    s_ctx = jnp.einsum('brmqhd,bkhd->brmhqk', q, ck).astype(jnp.float32) * scale
    p = jax.nn.softmax(jnp.concatenate([s_loc, s_ctx], axis=-1), axis=-1).astype(v.dtype)
    n_loc = wr * kbw
    o = (jnp.einsum('brmhqk,brmkhd->brmqhd', p[..., :n_loc], vb)
         + jnp.einsum('brmhqk,bkhd->brmqhd', p[..., n_loc:], cv))
    return o.reshape(bn, n, D_MODEL) @ w_out


def swiglu_clamped(hgu):
    x_glu, x_lin = jnp.split(hgu, 2, axis=-1)
    x_glu = jnp.minimum(x_glu, SWIGLU_LIMIT)
    x_lin = jnp.clip(x_lin, -SWIGLU_LIMIT, SWIGLU_LIMIT)
    return x_glu * jax.nn.sigmoid(SWIGLU_ALPHA * x_glu) * (x_lin + 1.0)


def moe(x, w_r, b_r, w_gu, b_gu, w_d, b_d):
    bn, n, d = x.shape
    t = bn * n
    xt = x.reshape(t, d)
    logits = (xt @ w_r + b_r).astype(jnp.float32)
    top_v, top_e = lax.top_k(logits, TOP_K)
    gates = jax.nn.softmax(top_v, axis=-1)
    n_assign = t * TOP_K
    e_flat = top_e.reshape(n_assign)
    tok_flat = jnp.repeat(jnp.arange(t, dtype=jnp.int32), TOP_K)
    g_flat = gates.reshape(n_assign)
    order = jnp.argsort(e_flat)
    e_s = e_flat[order]
    tok_s = tok_flat[order]
    g_s = g_flat[order]
    counts = jnp.bincount(e_flat, length=N_EXPERTS)
    padded = (counts + MOE_BLOCK - 1) // MOE_BLOCK * MOE_BLOCK
    pad_end = jnp.cumsum(padded)
    pad_start = pad_end - padded
    grp_start = jnp.cumsum(counts) - counts
    dest = pad_start[e_s] + jnp.arange(n_assign) - grp_start[e_s]
    n_blocks = -(-n_assign // MOE_BLOCK) + N_EXPERTS
    n_rows = n_blocks * MOE_BLOCK
    rows_tok = jnp.full((n_rows,), t, jnp.int32).at[dest].set(tok_s)
    rows_gate = jnp.zeros((n_rows,), jnp.float32).at[dest].set(g_s)
    block_e = jnp.minimum(
        jnp.searchsorted(pad_end, jnp.arange(n_blocks) * MOE_BLOCK, side='right'),
        N_EXPERTS - 1).astype(jnp.int32)
    x_pad = jnp.concatenate([xt, jnp.zeros((1, d), xt.dtype)], axis=0)
    xb = x_pad[rows_tok].reshape(n_blocks, MOE_BLOCK, d)

    def expert_block(args):
        xblk, e = args
        hgu = xblk @ w_gu[e] + b_gu[e]
        return swiglu_clamped(hgu) @ w_d[e] + b_d[e]

    yb = lax.map(expert_block, (xb, block_e)).reshape(n_rows, d)
    y = jnp.zeros((t + 1, d), x.dtype).at[rows_tok].add(yb * rows_gate[:, None].astype(yb.dtype))
    return y[:t].reshape(bn, n, d)


def setup_inputs(seed: int = 0) -> dict:
    key = jax.random.key(seed)
    ks = jax.random.split(key, 29)
    d = D_MODEL

    def nrm(k, shape, scale):
        return jax.random.normal(k, shape, jnp.float32) * scale

    return {
        'x_prompt': nrm(ks[0], (BATCH, SEQ, d), 1.0),
        'x_sample': nrm(ks[1], (DEC_BATCH, DEC_SEQ, d), 1.0),
        'c': nrm(ks[2], (DEC_BATCH, d), 1.0),
        'cache_k': nrm(ks[3], (DEC_BATCH, N_ODD, PAST_LEN, N_HEADS_C, HEAD_DIM), 1.0),
        'cache_v': nrm(ks[4], (DEC_BATCH, N_ODD, PAST_LEN, N_HEADS_C, HEAD_DIM), 1.0),
        'c_ctx': nrm(ks[5], (d,), 1.0),
        'w_mod': nrm(ks[6], (DEPTH, d, 6 * d), 0.5 * d ** -0.5),
        'b_mod': nrm(ks[7], (DEPTH, 6 * d), 0.02),
        'ln_g': 1.0 + nrm(ks[8], (DEPTH, 2, d), 0.02),
        'ln_b': nrm(ks[9], (DEPTH, 2, d), 0.02),
        'w_in_ab': nrm(ks[10], (N_EVEN, d, 2 * C_A + 2 * C_B), d ** -0.5),
        'sgu_ln_g': 1.0 + nrm(ks[11], (N_EVEN, C_A), 0.02),
        'sgu_ln_b': nrm(ks[12], (N_EVEN, C_A), 0.02),
        'w_spatial': nrm(ks[13], (N_EVEN, G_A, CHUNK, CHUNK), CHUNK ** -0.5),
        'b_spatial': 1.0 + nrm(ks[14], (N_EVEN, G_A, CHUNK), 0.02),
        'conv_w': nrm(ks[15], (N_EVEN, CONV_K, C_B), CONV_K ** -0.5),
        'conv_b': nrm(ks[16], (N_EVEN, C_B), 0.02),
        'conv_ln_g': 1.0 + nrm(ks[17], (N_EVEN, C_B), 0.02),
        'conv_ln_b': nrm(ks[18], (N_EVEN, C_B), 0.02),
        'w_out_ab': nrm(ks[19], (N_EVEN, C_A + C_B, d), (C_A + C_B) ** -0.5 * DEEPNORM_BETA),
        'w_qkv': nrm(ks[20], (N_ODD, d, 3 * d), d ** -0.5),
        'rpb': nrm(ks[21], (N_ODD, N_HEADS_C, 2 * WIN_ROWS - 1, 2 * WIN_COLS - 1), 0.1),
        'w_out_c': nrm(ks[22], (N_ODD, d, d), d ** -0.5 * DEEPNORM_BETA),
        'w_router': nrm(ks[23], (DEPTH, d, N_EXPERTS), d ** -0.5),
        'b_router': nrm(ks[24], (DEPTH, N_EXPERTS), 0.01),
        'w_gate_up': nrm(ks[25], (DEPTH, N_EXPERTS, d, 2 * D_EXPERT), d ** -0.5),
        'b_gate_up': nrm(ks[26], (DEPTH, N_EXPERTS, 2 * D_EXPERT), 0.01),
        'w_down': nrm(ks[27], (DEPTH, N_EXPERTS, D_EXPERT, d), D_EXPERT ** -0.5 * DEEPNORM_BETA),
        'b_down': nrm(ks[28], (DEPTH, N_EXPERTS, d), 0.01),
    }


def reference(x_prompt, x_sample, c, cache_k, cache_v, c_ctx, w_mod, b_mod, ln_g, ln_b,
              w_in_ab, sgu_ln_g, sgu_ln_b, w_spatial, b_spatial, conv_w, conv_b,
              conv_ln_g, conv_ln_b, w_out_ab, w_qkv, rpb, w_out_c, w_router, b_router,
              w_gate_up, b_gate_up, w_down, b_down):

    def run(x, cvec, ctx_kv):
        ks_out = []
        vs_out = []
        for l in range(DEPTH):
            i = l // 2
            mod = adaln(cvec, w_mod[l], b_mod[l])
            h = x * (1.0 + mod[:, 1]) + mod[:, 0]
            if l % 2 == 0:
                y = mixer_ab(h, w_in_ab[i], sgu_ln_g[i], sgu_ln_b[i], w_spatial[i], b_spatial[i],
                             conv_w[i], conv_b[i], conv_ln_g[i], conv_ln_b[i], w_out_ab[i])
            elif ctx_kv is None:
                y, k_l, v_l = attn_ctx(h, w_qkv[i], w_out_c[i])
                ks_out.append(k_l)
                vs_out.append(v_l)
            else:
                y = attn_latent(h, ctx_kv[0][:, i], ctx_kv[1][:, i], w_qkv[i], rpb[i], w_out_c[i])
            x = layer_norm(DEEPNORM_ALPHA * x + mod[:, 2] * y, ln_g[l, 0], ln_b[l, 0])
            h = x * (1.0 + mod[:, 4]) + mod[:, 3]
            y = moe(h, w_router[l], b_router[l], w_gate_up[l], b_gate_up[l], w_down[l], b_down[l])
            x = layer_norm(DEEPNORM_ALPHA * x + mod[:, 5] * y, ln_g[l, 1], ln_b[l, 1])
        return x, ks_out, vs_out

    y_prompt, ks_p, vs_p = run(x_prompt, c_ctx[None, :], None)
    new_cache_k = jnp.stack(ks_p, axis=1)
    new_cache_v = jnp.stack(vs_p, axis=1)
    y_sample, _, _ = run(x_sample, c, (cache_k, cache_v))
    return (y_prompt, y_sample, new_cache_k, new_cache_v)
```

```python
import functools

import numpy as np
import jax
import jax.numpy as jnp
from jax import lax
from jax.experimental import pallas as pl
from jax.experimental.pallas import tpu as pltpu

F32 = jnp.float32
BF16 = jnp.bfloat16

GRID_W = 64
G_A = 8
CHUNK = 128
CONV_K = 31
WIN_ROWS = 8
WIN_COLS = 16
TOP_K = 4
SWIGLU_ALPHA = 1.702
SWIGLU_LIMIT = 7.0
LN_EPS = 1e-5
NEG_INF = -1e30

LANES = 128
SUBLANES = 8
VMEM_LIMIT = 56 * 1024 * 1024

TOKEN_TILE = 256
HALO = 16
MOE_TILE = 256
Q_ROWS = 4
K_ROWS = 12
PAD_LOGIT = -3e38


def _ln(x, g, b):
    mu = jnp.mean(x, axis=-1, keepdims=True)
    xc = x - mu
    var = jnp.mean(xc * xc, axis=-1, keepdims=True)
    return xc * lax.rsqrt(var + LN_EPS) * g + b


def _gelu(x):
    return 0.5 * x * (1.0 + jnp.tanh(0.7978845608028654 * (x + 0.044715 * (x * x * x))))


def _sigmoid(x):
    return jax.nn.sigmoid(x)


def _dot(a, b):
    return jnp.dot(a, b, preferred_element_type=F32)


def _dot_nt(a, b):
    return lax.dot_general(a, b, (((1,), (1,)), ((), ())), preferred_element_type=F32)


def _params(n_axes):
    return pltpu.CompilerParams(dimension_semantics=("arbitrary",) * n_axes,
                                vmem_limit_bytes=VMEM_LIMIT)


def _mod_body(c_ref, w_ref, b_ref, o_ref):
    c = c_ref[...]
    s = (c * _sigmoid(c)).astype(BF16)
    o_ref[...] = _dot(s, w_ref[...].astype(BF16)) + b_ref[...]


def _modulation(cvec, w_mod, b_mod):
    depth, d, n = w_mod.shape
    tn = n // 4
    return pl.pallas_call(
        _mod_body,
        grid=(depth, n // tn),
        in_specs=[pl.BlockSpec((SUBLANES, d), lambda l, j: (0, 0)),
                  pl.BlockSpec((None, d, tn), lambda l, j: (l, 0, j)),
                  pl.BlockSpec((None, 1, tn), lambda l, j: (l, 0, j))],
        out_specs=pl.BlockSpec((None, SUBLANES, tn), lambda l, j: (l, 0, j)),
        out_shape=jax.ShapeDtypeStruct((depth, SUBLANES, n), F32),
        compiler_params=_params(2),
        name="adaln_modulation",
    )(cvec, w_mod, b_mod.reshape(depth, 1, n))


def _post_mixer(alpha, x, y, mod_ref, lng_ref, lnb_ref, wr_ref, br_ref,
                x1_ref, h2_ref, te_ref, tg_ref):
    x1 = _ln(alpha * x + mod_ref[2:3, :] * y, lng_ref[...], lnb_ref[...])
    x1_ref[...] = x1
    h2 = x1 * (1.0 + mod_ref[4:5, :]) + mod_ref[3:4, :]
    h2_ref[...] = h2
    logits = _dot(h2.astype(BF16), wr_ref[...]) + br_ref[...]
    lane = lax.broadcasted_iota(jnp.int32, logits.shape, 1)
    vals, idxs = [], []
    for _ in range(TOP_K):
        m = jnp.max(logits, axis=-1, keepdims=True)
        idx = jnp.min(jnp.where(logits == m, lane, LANES), axis=-1, keepdims=True)
        vals.append(m)
        idxs.append(idx)
        logits = jnp.where(lane == idx, -jnp.inf, logits)
    exps = [jnp.exp(v - vals[0]) for v in vals]
    den = exps[0]
    for e in exps[1:]:
        den = den + e
    te = jnp.zeros(lane.shape, jnp.int32)
    tg = jnp.zeros(lane.shape, F32)
    for k in range(TOP_K):
        te = jnp.where(lane == k, idxs[k], te)
        tg = jnp.where(lane == k, exps[k] / den, tg)
    te_ref[...] = te
    tg_ref[...] = tg


def _epilogue_specs(l, d):
    in_specs = [pl.BlockSpec((None, None, 1, d), lambda i, s, p, n: (l, 0, 0, 0)),
                pl.BlockSpec((None, None, 1, d), lambda i, s, p, n: (l, 0, 0, 0)),
                pl.BlockSpec((None, d, LANES), lambda i, s, p, n: (l, 0, 0)),
                pl.BlockSpec((None, 1, LANES), lambda i, s, p, n: (l, 0, 0))]
    out_specs = [pl.BlockSpec((TOKEN_TILE, d), lambda i, s, p, n: (i, 0)),
                 pl.BlockSpec((TOKEN_TILE, d), lambda i, s, p, n: (i, 0)),
                 pl.BlockSpec((TOKEN_TILE, LANES), lambda i, s, p, n: (i, 0)),
                 pl.BlockSpec((TOKEN_TILE, LANES), lambda i, s, p, n: (i, 0))]
    return in_specs, out_specs


def _epilogue_out_shapes(t, d):
    return [jax.ShapeDtypeStruct((t, d), F32), jax.ShapeDtypeStruct((t, d), F32),
            jax.ShapeDtypeStruct((t, LANES), jnp.int32), jax.ShapeDtypeStruct((t, LANES), F32)]


def _even_body(alpha, seg_ref, prev_ref, next_ref,
               x_ref, xp_ref, xn_ref, mod_ref, win_ref, sg_ref, sb_ref, wsp_ref, bsp_ref,
               cw_ref, cb_ref, cg_ref, cbb_ref, wout_ref, lng_ref, lnb_ref, wr_ref, br_ref,
               x1_ref, h2_ref, te_ref, tg_ref, gl_scr):
    i = pl.program_id(0)
    tt = x_ref.shape[0]
    ca = sg_ref.shape[-1]
    cb2 = 2 * ca
    x = x_ref[...]
    sc = 1.0 + mod_ref[1:2, :]
    sh = mod_ref[0:1, :]
    z = _dot((x * sc + sh).astype(BF16), win_ref[...])

    u = _gelu(z[:, :ca])
    v = _ln(_gelu(z[:, ca:cb2]), sg_ref[...], sb_ref[...]).astype(BF16)
    half = lax.broadcasted_iota(jnp.int32, (CHUNK, LANES), 1) < (LANES // 2)
    chunks = []
    for ck in range(tt // CHUNK):
        cols = []
        for j in range(ca // LANES):
            vblk = v[ck * CHUNK:(ck + 1) * CHUNK, j * LANES:(j + 1) * LANES]
            cols.append(jnp.where(half, _dot(wsp_ref[2 * j], vblk), _dot(wsp_ref[2 * j + 1], vblk)))
        chunks.append(jnp.concatenate(cols, axis=1) + bsp_ref[...])
    y_a = u * jnp.concatenate(chunks, axis=0)

    def glu_rows(xh_ref):
        zh = _dot((xh_ref[...] * sc + sh).astype(BF16), win_ref[:, cb2:])
        return zh[:, :ca] * _sigmoid(zh[:, ca:])

    gl_scr[0:HALO, :] = jnp.where(prev_ref[i] > 0, glu_rows(xp_ref), 0.0)
    gl_scr[HALO:HALO + tt, :] = z[:, cb2:cb2 + ca] * _sigmoid(z[:, cb2 + ca:])
    gl_scr[HALO + tt:, :] = jnp.where(next_ref[i] > 0, glu_rows(xn_ref), 0.0)
    off = HALO - CONV_K // 2
    dc = gl_scr[off:off + tt, :] * cw_ref[0:1, :]
    for k in range(1, CONV_K):
        dc = dc + gl_scr[off + k:off + k + tt, :] * cw_ref[k:k + 1, :]
    yb = _ln(dc + cb_ref[...], cg_ref[...], cbb_ref[...])
    y_b = yb * _sigmoid(yb)

    y = _dot(jnp.concatenate([y_a, y_b], axis=1).astype(BF16), wout_ref[...])
    _post_mixer(alpha, x, y, mod_ref, lng_ref, lnb_ref, wr_ref, br_ref,
                x1_ref, h2_ref, te_ref, tg_ref)


def _even_layer(l, alpha, meta, x, mod, p):
    t, d = x.shape
    li = l // 2
    nh = TOKEN_TILE // HALO
    n_halo = t // HALO
    ca = p["sgu_g"].shape[-1]
    ep_in, ep_out = _epilogue_specs(l, d)
    const3 = lambda i, s, pv, nx: (li, 0, 0)
    in_specs = [
        pl.BlockSpec((TOKEN_TILE, d), lambda i, s, pv, nx: (i, 0)),
        pl.BlockSpec((HALO, d), lambda i, s, pv, nx: (jnp.maximum(i * nh - 1, 0), 0)),
        pl.BlockSpec((HALO, d), lambda i, s, pv, nx: (jnp.minimum((i + 1) * nh, n_halo - 1), 0)),
        pl.BlockSpec((None, None, 6, d), lambda i, s, pv, nx: (l, s[i], 0, 0)),
        pl.BlockSpec((None, d, 4 * ca), const3),
        pl.BlockSpec((None, 1, ca), const3),
        pl.BlockSpec((None, 1, ca), const3),
        pl.BlockSpec((None, G_A, CHUNK, CHUNK), lambda i, s, pv, nx: (li, 0, 0, 0)),
        pl.BlockSpec((None, CHUNK, ca), const3),
        pl.BlockSpec((None, CONV_K, ca), const3),
        pl.BlockSpec((None, 1, ca), const3),
        pl.BlockSpec((None, 1, ca), const3),
        pl.BlockSpec((None, 1, ca), const3),
        pl.BlockSpec((None, 2 * ca, d), const3),
    ] + ep_in
    return pl.pallas_call(
        functools.partial(_even_body, alpha),
        grid_spec=pltpu.PrefetchScalarGridSpec(
            num_scalar_prefetch=3, grid=(t // TOKEN_TILE,),
            in_specs=in_specs, out_specs=ep_out,
            scratch_shapes=[pltpu.VMEM((TOKEN_TILE + 2 * HALO, ca), F32)]),
        out_shape=_epilogue_out_shapes(t, d),
        compiler_params=_params(1),
        name=f"even_mixer_{l}",
    )(meta["seg"], meta["prev"], meta["next"], x, x, x, mod,
      p["w_in"], p["sgu_g"], p["sgu_b"], p["w_sp"], p["b_sp"], p["conv_w"], p["conv_b"],
      p["cln_g"], p["cln_b"], p["w_out_ab"], p["ln_g"], p["ln_b"], p["w_router"], p["b_router"])


def _qkv_body(seg_ref, x_ref, mod_ref, w_ref, q_ref, k_ref, v_ref, k32_ref, v32_ref):
    d = x_ref.shape[1]
    h = (x_ref[...] * (1.0 + mod_ref[1:2, :]) + mod_ref[0:1, :]).astype(BF16)
    qkv = _dot(h, w_ref[...])
    q_ref[...] = qkv[:, :d].astype(BF16)
    k = qkv[:, d:2 * d]
    v = qkv[:, 2 * d:]
    k32_ref[...] = k
    v32_ref[...] = v
    k_ref[...] = k.astype(BF16)
    v_ref[...] = v.astype(BF16)


def _qkv_proj(l, meta, x, mod, w_qkv):
    t, d = x.shape
    li = l // 2
    tile = pl.BlockSpec((TOKEN_TILE, d), lambda i, s: (i, 0))
    return pl.pallas_call(
        _qkv_body,
        grid_spec=pltpu.PrefetchScalarGridSpec(
            num_scalar_prefetch=1, grid=(t // TOKEN_TILE,),
            in_specs=[tile,
                      pl.BlockSpec((None, None, 6, d), lambda i, s: (l, s[i], 0, 0)),
                      pl.BlockSpec((None, d, 3 * d), lambda i, s: (li, 0, 0))],
            out_specs=[tile] * 5),
        out_shape=[jax.ShapeDtypeStruct((t, d), BF16)] * 3 + [jax.ShapeDtypeStruct((t, d), F32)] * 2,
        compiler_params=_params(1),
        name=f"qkv_proj_{l}",
    )(meta["seg"], x, mod, w_qkv)


def _head_pair_attention(q2, k_parts, v_parts, bias_parts, scale):
    lane = lax.broadcasted_iota(jnp.int32, q2.shape, 1)
    outs = []
    for hh in range(2):
        qm = jnp.where((lane >= hh * (LANES // 2)) & (lane < (hh + 1) * (LANES // 2)), q2,
                       jnp.zeros_like(q2))
        ss = []
        for j, kp in enumerate(k_parts):
            s = _dot_nt(qm, kp) * scale
            if bias_parts[hh][j] is not None:
                s = s + bias_parts[hh][j]
            ss.append(s)
        m = ss[0].max(axis=-1, keepdims=True)
        for s in ss[1:]:
            m = jnp.maximum(m, s.max(axis=-1, keepdims=True))
        den = None
        o = None
        for s, vp in zip(ss, v_parts):
            e = jnp.exp(s - m)
            es = e.sum(axis=-1, keepdims=True)
            den = es if den is None else den + es
            pv = _dot(e.astype(BF16), vp)
            o = pv if o is None else o + pv
        outs.append(o / den)
    return jnp.where(lane < LANES // 2, outs[0], outs[1])


def _attn_ctx_body(scale, q_ref, k_ref, v_ref, o_ref):
    d = q_ref.shape[1]
    for pr in range(d // LANES):
        sl = slice(pr * LANES, (pr + 1) * LANES)
        o = _head_pair_attention(q_ref[:, sl], [k_ref[:, sl]], [v_ref[:, sl]],
                                 [[None], [None]], scale)
        o_ref[:, sl] = o.astype(o_ref.dtype)


def _attn_ctx(q, k, v, n_seq, seq, scale):
    d = q.shape[1]
    blk = pl.BlockSpec((seq, d), lambda b: (b, 0))
    return pl.pallas_call(
        functools.partial(_attn_ctx_body, scale),
        grid=(n_seq,),
        in_specs=[blk, blk, blk],
        out_specs=blk,
        out_shape=jax.ShapeDtypeStruct((n_seq * seq, d), BF16),
        compiler_params=_params(1),
        name="attn_ctx",
    )(q, k, v)


def _attn_lat_body(scale, cls_ref, kb_ref, q_ref, k0_ref, k1_ref, k2_ref, v0_ref, v1_ref, v2_ref,
                   ck_ref, cv_ref, bias_ref, o_ref):
    d = q_ref.shape[1]
    for pr in range(d // LANES):
        sl = slice(pr * LANES, (pr + 1) * LANES)
        k_loc = jnp.concatenate([k0_ref[:, sl], k1_ref[:, sl], k2_ref[:, sl]], axis=0)
        v_loc = jnp.concatenate([v0_ref[:, sl], v1_ref[:, sl], v2_ref[:, sl]], axis=0)
        ck = ck_ref[:, sl].astype(BF16)
        cv = cv_ref[:, sl].astype(BF16)
        o = _head_pair_attention(q_ref[:, sl], [k_loc, ck], [v_loc, cv],
                                 [[bias_ref[2 * pr], None], [bias_ref[2 * pr + 1], None]], scale)
        o_ref[:, sl] = o.astype(o_ref.dtype)


def _latent_window_tables(rows_n):
    wr = min(WIN_ROWS, rows_n)
    n_rt = rows_n // Q_ROWS
    kstart = np.clip(np.arange(n_rt) * Q_ROWS - wr // 2, 0, rows_n - K_ROWS)
    kstart = (kstart // Q_ROWS) * Q_ROWS
    patterns, cls = [], []
    for rt in range(n_rt):
        pat = np.full((Q_ROWS, K_ROWS), -1, np.int64)
        for qi in range(Q_ROWS):
            r = rt * Q_ROWS + qi
            rs = int(np.clip(r - wr // 2, 0, rows_n - wr))
            for kj in range(K_ROWS):
                kr = int(kstart[rt]) + kj
                if rs <= kr < rs + wr:
                    pat[qi, kj] = kr - r + WIN_ROWS - 1
        assert (pat >= 0).sum(axis=1).min() == wr, "key block does not cover the window"
        key = pat.tobytes()
        if key not in [p.tobytes() for p in patterns]:
            patterns.append(pat)
        cls.append([p.tobytes() for p in patterns].index(key))
    return (kstart // Q_ROWS).astype(np.int32), np.asarray(cls, np.int32), np.stack(patterns)


def _latent_bias(rpb, patterns):
    h = rpb.shape[0]
    qc = np.arange(GRID_W)[:, None]
    kc = np.arange(GRID_W)[None, :]
    qcs = np.clip(qc - WIN_COLS // 2, 0, GRID_W - WIN_COLS)
    col_ok = (kc >= qcs) & (kc < qcs + WIN_COLS)
    dc = np.clip(kc - qc + WIN_COLS - 1, 0, 2 * WIN_COLS - 2)
    onehot = (dc[None] == np.arange(2 * WIN_COLS - 1)[:, None, None]) & col_ok[None]
    cm = jnp.einsum("hrd,dqk->hrqk", rpb, jnp.asarray(onehot, F32), precision=lax.Precision.HIGHEST)
    cm = jnp.where(jnp.asarray(col_ok), cm, NEG_INF)
    cx = jnp.concatenate([cm, jnp.full((h, 1, GRID_W, GRID_W), NEG_INF, F32)], axis=1)
    idx = np.where(patterns >= 0, patterns, 2 * WIN_ROWS - 1)
    b = jnp.take(cx, jnp.asarray(idx.reshape(-1), jnp.int32), axis=1)
    b = b.reshape(h, idx.shape[0], Q_ROWS, K_ROWS, GRID_W, GRID_W)
    b = jnp.transpose(b, (1, 0, 2, 4, 3, 5))
    return b.reshape(idx.shape[0], h, Q_ROWS * GRID_W, K_ROWS * GRID_W)


def _attn_lat(q, k, v, ck, cv, bias, tables, tok0, n_batch, n_tok, scale):
    d = q.shape[1]
    kblk, cls, _ = tables
    n_rt = kblk.shape[0]
    qt = Q_ROWS * GRID_W
    base = tok0 // qt
    per_b = n_tok // qt
    h = bias.shape[1]
    lc = ck.shape[1]

    def kv_spec(j):
        return pl.BlockSpec((qt, d), lambda b, r, c, kb: (base + b * per_b + kb[r] + j, 0))

    return pl.pallas_call(
        functools.partial(_attn_lat_body, scale),
        grid_spec=pltpu.PrefetchScalarGridSpec(
            num_scalar_prefetch=2, grid=(n_batch, n_rt),
            in_specs=[pl.BlockSpec((qt, d), lambda b, r, c, kb: (base + b * per_b + r, 0)),
                      kv_spec(0), kv_spec(1), kv_spec(2), kv_spec(0), kv_spec(1), kv_spec(2),
                      pl.BlockSpec((None, lc, d), lambda b, r, c, kb: (b, 0, 0)),
                      pl.BlockSpec((None, lc, d), lambda b, r, c, kb: (b, 0, 0)),
                      pl.BlockSpec((None, h, qt, K_ROWS * GRID_W), lambda b, r, c, kb: (c[r], 0, 0, 0))],
            out_specs=pl.BlockSpec((qt, d), lambda b, r, c, kb: (b * per_b + r, 0))),
        out_shape=jax.ShapeDtypeStruct((n_batch * n_tok, d), BF16),
        compiler_params=_params(2),
        name="attn_latent",
    )(jnp.asarray(cls), jnp.asarray(kblk), q, k, k, k, v, v, v, ck, cv, bias)


def _proj_body(alpha, seg_ref, prev_ref, next_ref, x_ref, o_ref, mod_ref, w_ref,
               lng_ref, lnb_ref, wr_ref, br_ref, x1_ref, h2_ref, te_ref, tg_ref):
    y = _dot(o_ref[...], w_ref[...])
    _post_mixer(alpha, x_ref[...], y, mod_ref, lng_ref, lnb_ref, wr_ref, br_ref,
                x1_ref, h2_ref, te_ref, tg_ref)


def _odd_out_proj(l, alpha, meta, x, o, mod, p):
    t, d = x.shape
    li = l // 2
    ep_in, ep_out = _epilogue_specs(l, d)
    tile = pl.BlockSpec((TOKEN_TILE, d), lambda i, s, pv, nx: (i, 0))
    return pl.pallas_call(
        functools.partial(_proj_body, alpha),
        grid_spec=pltpu.PrefetchScalarGridSpec(
            num_scalar_prefetch=3, grid=(t // TOKEN_TILE,),
            in_specs=[tile, tile,
                      pl.BlockSpec((None, None, 6, d), lambda i, s, pv, nx: (l, s[i], 0, 0)),
                      pl.BlockSpec((None, d, d), lambda i, s, pv, nx: (li, 0, 0))] + ep_in,
            out_specs=ep_out),
        out_shape=_epilogue_out_shapes(t, d),
        compiler_params=_params(1),
        name=f"attn_out_proj_{l}",
    )(meta["seg"], meta["prev"], meta["next"], x, o, mod, p["w_out_c"],
      p["ln_g"], p["ln_b"], p["w_router"], p["b_router"])


def _route_tables(top_e, n_experts, n_tiles):
    t = top_e.shape[0]
    onehot = (top_e[:, :, None] == jnp.arange(n_experts, dtype=jnp.int32)).any(axis=1).astype(jnp.int32)
    csum = jnp.cumsum(onehot, axis=0)
    counts = csum[-1]
    padded = (counts + MOE_TILE - 1) // MOE_TILE * MOE_TILE
    pad_end = jnp.cumsum(padded)
    pos_te = (pad_end - padded)[None, :] + csum - onehot
    pos = jnp.take_along_axis(pos_te, top_e, axis=1)
    n_rows = n_tiles * MOE_TILE
    assign = jnp.arange(t * TOP_K, dtype=jnp.int32)
    rows_assign = jnp.full((n_rows,), -1, jnp.int32).at[pos.reshape(-1)].set(
        assign, unique_indices=True)
    valid = rows_assign >= 0
    row = jnp.arange(n_rows, dtype=jnp.int32)
    dump = t * TOP_K + ((row // MOE_TILE) % 2) * MOE_TILE + row % MOE_TILE
    rows_src = jnp.where(valid, rows_assign // TOP_K, 0)
    rows_dst = jnp.where(valid, (rows_assign % TOP_K) * t + rows_assign // TOP_K, dump)
    n_used = (pad_end[-1] // MOE_TILE).astype(jnp.int32)
    tile_e = jnp.searchsorted(pad_end, jnp.arange(n_tiles, dtype=jnp.int32) * MOE_TILE, side="right")
    tile_e = jnp.minimum(tile_e, n_experts - 1).astype(jnp.int32)
    tile_e = jnp.where(jnp.arange(n_tiles) < n_used, tile_e, tile_e[jnp.maximum(n_used - 1, 0)])
    shape3 = (n_tiles, 1, MOE_TILE)
    return tile_e, n_used.reshape(1), rows_src.reshape(shape3), rows_dst.reshape(shape3)


def _moe_body(te_ref, nu_ref, src_ref, srcn_ref, dst_ref, h2_hbm, wgu_ref, bgu_ref, wd_ref, bd_ref,
              y4_hbm, xbuf, ybuf, wgu_bf, wd_bf, gsem, ssem):
    i = pl.program_id(0)
    n_steps = pl.num_programs(0)
    nu = nu_ref[0]
    slot = i % 2
    tm = xbuf.shape[1]
    de = wd_bf.shape[0]

    def gather_start(idx_ref, s):
        def body(r, c):
            pltpu.make_async_copy(h2_hbm.at[pl.ds(idx_ref[0, r], 1)],
                                  xbuf.at[s, pl.ds(r, 1)], gsem.at[s]).start()
            return c
        lax.fori_loop(0, tm, body, 0, unroll=8)

    def scatter_wait(s):
        pltpu.make_async_copy(ybuf.at[s], y4_hbm.at[pl.ds(0, tm)], ssem.at[s]).wait()

    @pl.when(i == 0)
    def _():
        gather_start(src_ref, 0)
        n_real = y4_hbm.shape[0] - 2 * tm
        ybuf[...] = jnp.zeros(ybuf.shape, ybuf.dtype)
        for s in range(2):
            cp = pltpu.make_async_copy(ybuf.at[s], y4_hbm.at[pl.ds(n_real + s * tm, tm)], ssem.at[s])
            cp.start()
            cp.wait()

    @pl.when(i + 1 < nu)
    def _():
        gather_start(srcn_ref, 1 - slot)

    @pl.when(i < nu)
    def _():
        pltpu.make_async_copy(h2_hbm.at[pl.ds(0, tm)], xbuf.at[slot], gsem.at[slot]).wait()

        @pl.when((i == 0) | (te_ref[i] != te_ref[jnp.maximum(i - 1, 0)]))
        def _():
            wgu_bf[...] = wgu_ref[...].astype(BF16)
            wd_bf[...] = wd_ref[...].astype(BF16)

        hgu = _dot(xbuf[slot].astype(BF16), wgu_bf[...]) + bgu_ref[...]
        x_glu = jnp.minimum(hgu[:, :de], SWIGLU_LIMIT)
        x_lin = jnp.clip(hgu[:, de:], -SWIGLU_LIMIT, SWIGLU_LIMIT)
        act = x_glu * _sigmoid(SWIGLU_ALPHA * x_glu) * (x_lin + 1.0)
        y = _dot(act.astype(BF16), wd_bf[...]) + bd_ref[...]

        @pl.when(i >= 2)
        def _():
            scatter_wait(slot)

        ybuf[slot] = y

        def body(r, c):
            pltpu.make_async_copy(ybuf.at[slot, pl.ds(r, 1)],
                                  y4_hbm.at[pl.ds(dst_ref[0, r], 1)], ssem.at[slot]).start()
            return c
        lax.fori_loop(0, tm, body, 0, unroll=8)

    @pl.when(i == n_steps - 1)
    def _():
        @pl.when(nu >= 2)
        def _():
            scatter_wait(nu % 2)
        scatter_wait((nu + 1) % 2)


def _moe_experts(l, h2, tables, w_gu, b_gu, w_d, b_d):
    t, d = h2.shape
    tile_e, n_used, rows_src, rows_dst = tables
    n_tiles = tile_e.shape[0]
    n_e, de = w_d.shape[1], w_d.shape[2]
    smem_blk = lambda f: pl.BlockSpec((None, 1, MOE_TILE), f, memory_space=pltpu.SMEM)
    return pl.pallas_call(
        _moe_body,
        grid_spec=pltpu.PrefetchScalarGridSpec(
            num_scalar_prefetch=2, grid=(n_tiles,),
            in_specs=[smem_blk(lambda i, te, nu: (i, 0, 0)),
                      smem_blk(lambda i, te, nu: (jnp.minimum(i + 1, n_tiles - 1), 0, 0)),
                      smem_blk(lambda i, te, nu: (i, 0, 0)),
                      pl.BlockSpec(memory_space=pl.ANY),
                      pl.BlockSpec((None, None, d, 2 * de), lambda i, te, nu: (l, te[i], 0, 0)),
                      pl.BlockSpec((None, None, 1, 2 * de), lambda i, te, nu: (l, te[i], 0, 0)),
                      pl.BlockSpec((None, None, de, d), lambda i, te, nu: (l, te[i], 0, 0)),
                      pl.BlockSpec((None, None, 1, d), lambda i, te, nu: (l, te[i], 0, 0))],
            out_specs=pl.BlockSpec(memory_space=pl.ANY),
            scratch_shapes=[pltpu.VMEM((2, MOE_TILE, d), F32),
                            pltpu.VMEM((2, MOE_TILE, d), F32),
                            pltpu.VMEM((d, 2 * de), BF16),
                            pltpu.VMEM((de, d), BF16),
                            pltpu.SemaphoreType.DMA((2,)),
                            pltpu.SemaphoreType.DMA((2,))]),
        out_shape=jax.ShapeDtypeStruct((t * TOP_K + 2 * MOE_TILE, d), F32),
        compiler_params=_params(1),
        name=f"moe_experts_{l}",
    )(tile_e, n_used, rows_src, rows_src, rows_dst, h2, w_gu,
      b_gu.reshape(b_gu.shape[0], n_e, 1, 2 * de), w_d, b_d.reshape(b_d.shape[0], n_e, 1, d))


def _combine_body(alpha, seg_ref, x1_ref, y0_ref, y1_ref, y2_ref, y3_ref, g_ref, mod_ref,
                  lng_ref, lnb_ref, o_ref):
    g = g_ref[...]
    y = g[:, 0:1] * y0_ref[...]
    for k, y_ref in ((1, y1_ref), (2, y2_ref), (3, y3_ref)):
        y = y + g[:, k:k + 1] * y_ref[...]
    o_ref[...] = _ln(alpha * x1_ref[...] + mod_ref[5:6, :] * y, lng_ref[...], lnb_ref[...])


def _moe_combine(l, alpha, meta, x1, y4, gates, mod, ln_g, ln_b):
    t, d = x1.shape
    nt = t // TOKEN_TILE
    tile = pl.BlockSpec((TOKEN_TILE, d), lambda i, s: (i, 0))
    y_spec = lambda k: pl.BlockSpec((TOKEN_TILE, d), lambda i, s: (k * nt + i, 0))
    return pl.pallas_call(
        functools.partial(_combine_body, alpha),
        grid_spec=pltpu.PrefetchScalarGridSpec(
            num_scalar_prefetch=1, grid=(nt,),
            in_specs=[tile, y_spec(0), y_spec(1), y_spec(2), y_spec(3),
                      pl.BlockSpec((TOKEN_TILE, LANES), lambda i, s: (i, 0)),
                      pl.BlockSpec((None, None, 6, d), lambda i, s: (l, s[i], 0, 0)),
                      pl.BlockSpec((None, None, 1, d), lambda i, s: (l, 1, 0, 0)),
                      pl.BlockSpec((None, None, 1, d), lambda i, s: (l, 1, 0, 0))],
            out_specs=tile),
        out_shape=jax.ShapeDtypeStruct((t, d), F32),
        compiler_params=_params(1),
        name=f"moe_combine_{l}",
    )(meta["seg"], x1, y4, y4, y4, y4, gates, mod, ln_g, ln_b)


def _token_meta(n_ctx_seq, seq, n_lat, lat_seq):
    seg, prev, nxt = [], [], []
    for n_seq, length, seg_of in ((n_ctx_seq, seq, lambda b: 0), (n_lat, lat_seq, lambda b: 1 + b)):
        per = length // TOKEN_TILE
        for b in range(n_seq):
            for j in range(per):
                seg.append(seg_of(b))
                prev.append(int(j > 0))
                nxt.append(int(j < per - 1))
    as_i32 = lambda a: jnp.asarray(np.asarray(a, np.int32))
    return {"seg": as_i32(seg), "prev": as_i32(prev), "next": as_i32(nxt)}


def kernel(x_prompt, x_sample, c, cache_k, cache_v, c_ctx, w_mod, b_mod, ln_g, ln_b, w_in_ab, sgu_ln_g, sgu_ln_b, w_spatial, b_spatial, conv_w, conv_b, conv_ln_g, conv_ln_b, w_out_ab, w_qkv, rpb, w_out_c, w_router, b_router, w_gate_up, b_gate_up, w_down, b_down):
    n_ctx_seq, seq, d = x_prompt.shape
    n_lat, lat_seq, _ = x_sample.shape
    depth = w_mod.shape[0]
    n_heads, head_dim = cache_k.shape[3], cache_k.shape[4]
    n_experts = w_router.shape[-1]
    ca = sgu_ln_g.shape[-1]
    n_even, n_odd = w_in_ab.shape[0], w_qkv.shape[0]
    t_ctx, t_lat = n_ctx_seq * seq, n_lat * lat_seq
    t = t_ctx + t_lat
    rows_n = lat_seq // GRID_W
    assert seq % TOKEN_TILE == 0 and lat_seq % TOKEN_TILE == 0 and 1 + n_lat <= SUBLANES
    assert TOKEN_TILE % CHUNK == 0 and HALO >= CONV_K // 2 and ca == w_out_ab.shape[1] // 2
    assert rows_n % Q_ROWS == 0 and rows_n >= K_ROWS and t_ctx % (Q_ROWS * GRID_W) == 0
    assert n_heads * head_dim == d and 2 * head_dim == LANES and n_experts <= LANES
    assert (t * TOP_K) % MOE_TILE == 0
    alpha = float((2 * depth) ** 0.25)
    scale = float(head_dim ** -0.5)
    meta = _token_meta(n_ctx_seq, seq, n_lat, lat_seq)

    x = jnp.concatenate([x_prompt.reshape(t_ctx, d), x_sample.reshape(t_lat, d)], axis=0)
    cvec = jnp.zeros((SUBLANES, d), F32).at[0].set(c_ctx).at[1:1 + n_lat].set(c)
    mod = _modulation(cvec, w_mod, b_mod).reshape(depth, SUBLANES, 6, d)

    pad_e = LANES - n_experts
    common = {
        "ln_g": ln_g.reshape(depth, 2, 1, d), "ln_b": ln_b.reshape(depth, 2, 1, d),
        "w_router": jnp.pad(w_router, ((0, 0), (0, 0), (0, pad_e))).astype(BF16),
        "b_router": jnp.pad(b_router, ((0, 0), (0, pad_e)), constant_values=PAD_LOGIT).reshape(depth, 1, LANES),
    }
    even = dict(common)
    even.update({
        "w_in": w_in_ab.astype(BF16), "sgu_g": sgu_ln_g.reshape(n_even, 1, ca),
        "sgu_b": sgu_ln_b.reshape(n_even, 1, ca), "w_sp": w_spatial.astype(BF16),
        "b_sp": jnp.repeat(jnp.transpose(b_spatial, (0, 2, 1)), ca // G_A, axis=2),
        "conv_w": conv_w, "conv_b": conv_b.reshape(n_even, 1, ca),
        "cln_g": conv_ln_g.reshape(n_even, 1, ca), "cln_b": conv_ln_b.reshape(n_even, 1, ca),
        "w_out_ab": w_out_ab.astype(BF16)})
    odd = dict(common)
    odd["w_out_c"] = w_out_c.astype(BF16)
    w_qkv_bf = w_qkv.astype(BF16)
    lat_tables = _latent_window_tables(rows_n)
    n_tiles = t * TOP_K // MOE_TILE + n_experts

    new_k, new_v = [], []
    for l in range(depth):
        i = l // 2
        if l % 2 == 0:
            x1, h2, top_e, gates = _even_layer(l, alpha, meta, x, mod, even)
        else:
            q, k, v, k32, v32 = _qkv_proj(l, meta, x, mod, w_qkv_bf)
            new_k.append(k32[:t_ctx].reshape(n_ctx_seq, seq, n_heads, head_dim))
            new_v.append(v32[:t_ctx].reshape(n_ctx_seq, seq, n_heads, head_dim))
            o_ctx = _attn_ctx(q, k, v, n_ctx_seq, seq, scale)
            bias = _latent_bias(rpb[i], lat_tables[2])
            o_lat = _attn_lat(q, k, v, cache_k[:, i].reshape(n_lat, -1, d),
                              cache_v[:, i].reshape(n_lat, -1, d), bias, lat_tables,
                              t_ctx, n_lat, lat_seq, scale)
            o = jnp.concatenate([o_ctx, o_lat], axis=0)
            x1, h2, top_e, gates = _odd_out_proj(l, alpha, meta, x, o, mod, odd)
        tables = _route_tables(top_e[:, :TOP_K], n_experts, n_tiles)
        y4 = _moe_experts(l, h2, tables, w_gate_up, b_gate_up, w_down, b_down)
        x = _moe_combine(l, alpha, meta, x1, y4, gates, mod, common["ln_g"], common["ln_b"])

    y_prompt = x[:t_ctx].reshape(n_ctx_seq, seq, d)
    y_sample = x[t_ctx:].reshape(n_lat, lat_seq, d)
    return (y_prompt, y_sample, jnp.stack(new_k, axis=1), jnp.stack(new_v, axis=1))
```

```python
import functools

import numpy as np
import jax
import jax.numpy as jnp
from jax import lax
from jax.experimental import pallas as pl
from jax.experimental.pallas import tpu as pltpu

F32 = jnp.float32
BF16 = jnp.bfloat16

GRID_W = 64
G_A = 8
CHUNK = 128
CONV_K = 31
WIN_ROWS = 8
WIN_COLS = 16
TOP_K = 4
SWIGLU_ALPHA = 1.702
SWIGLU_LIMIT = 7.0
LN_EPS = 1e-5
NEG_INF = -1e30

LANES = 128
SUBLANES = 8
VMEM_LIMIT = 56 * 1024 * 1024

TOKEN_TILE = 256
HALO = 16
MOE_TILE = 256
Q_ROWS = 4
K_ROWS = 12
PAD_LOGIT = -3e38


def _ln(x, g, b):
    mu = jnp.mean(x, axis=-1, keepdims=True)
    xc = x - mu
    var = jnp.mean(xc * xc, axis=-1, keepdims=True)
    return xc * lax.rsqrt(var + LN_EPS) * g + b


def _gelu(x):
    return 0.5 * x * (1.0 + jnp.tanh(0.7978845608028654 * (x + 0.044715 * (x * x * x))))


def _sigmoid(x):
    return jax.nn.sigmoid(x)


def _dot(a, b):
    return jnp.dot(a, b, preferred_element_type=F32)


def _dot_nt(a, b):
    return lax.dot_general(a, b, (((1,), (1,)), ((), ())), preferred_element_type=F32)


def _params(n_axes):
    return pltpu.CompilerParams(dimension_semantics=("arbitrary",) * n_axes,
                                vmem_limit_bytes=VMEM_LIMIT)


def _mod_body(c_ref, w_ref, b_ref, o_ref):
    c = c_ref[...]
    s = (c * _sigmoid(c)).astype(BF16)
    o_ref[...] = _dot(s, w_ref[...].astype(BF16)) + b_ref[...]


def _modulation(cvec, w_mod, b_mod):
    depth, d, n = w_mod.shape
    tn = n // 4
    return pl.pallas_call(
        _mod_body,
        grid=(depth, n // tn),
        in_specs=[pl.BlockSpec((SUBLANES, d), lambda l, j: (0, 0)),
                  pl.BlockSpec((None, d, tn), lambda l, j: (l, 0, j)),
                  pl.BlockSpec((None, 1, tn), lambda l, j: (l, 0, j))],
        out_specs=pl.BlockSpec((None, SUBLANES, tn), lambda l, j: (l, 0, j)),
        out_shape=jax.ShapeDtypeStruct((depth, SUBLANES, n), F32),
        compiler_params=_params(2),
        name="adaln_modulation",
    )(cvec, w_mod, b_mod.reshape(depth, 1, n))


def _post_mixer(alpha, x, y, mod_ref, lng_ref, lnb_ref, wr_ref, br_ref,
                x1_ref, h2_ref, te_ref, tg_ref, cnt_ref, cnt_scr):
    i = pl.program_id(0)
    x1 = _ln(alpha * x + mod_ref[2:3, :] * y, lng_ref[...], lnb_ref[...])
    x1_ref[...] = x1
    h2 = x1 * (1.0 + mod_ref[4:5, :]) + mod_ref[3:4, :]
    h2_ref[...] = h2
    logits = _dot(h2.astype(BF16), wr_ref[...]) + br_ref[...]
    tt = logits.shape[0]
    lane = lax.broadcasted_iota(jnp.int32, logits.shape, 1)
    vals, idxs = [], []
    for _ in range(TOP_K):
        m = jnp.max(logits, axis=-1, keepdims=True)
        idx = jnp.min(jnp.where(logits == m, lane, LANES), axis=-1, keepdims=True)
        vals.append(m)
        idxs.append(idx)
        logits = jnp.where(lane == idx, -jnp.inf, logits)
    exps = [jnp.exp(v - vals[0]) for v in vals]
    den = exps[0]
    for e in exps[1:]:
        den = den + e

    @pl.when(i == 0)
    def _():
        cnt_scr[...] = jnp.zeros(cnt_scr.shape, cnt_scr.dtype)

    onehot = jnp.zeros(logits.shape, F32)
    for k in range(TOP_K):
        onehot = onehot + (lane == idxs[k]).astype(F32)
    row = lax.broadcasted_iota(jnp.int32, (tt, tt), 0)
    col = lax.broadcasted_iota(jnp.int32, (tt, tt), 1)
    before = _dot((row > col).astype(BF16), onehot.astype(BF16)) + cnt_scr[0:1, :]
    cnt = cnt_scr[...] + jnp.sum(onehot, axis=0, keepdims=True)
    cnt_scr[...] = cnt
    cnt_ref[...] = cnt.astype(jnp.int32)

    te = jnp.zeros(lane.shape, jnp.int32)
    tg = jnp.zeros(lane.shape, F32)
    for k in range(TOP_K):
        rank = jnp.sum(jnp.where(lane == idxs[k], before, 0.0), axis=-1, keepdims=True)
        te = jnp.where(lane == k, idxs[k], te)
        te = jnp.where(lane == TOP_K + k, rank.astype(jnp.int32), te)
        tg = jnp.where(lane == k, exps[k] / den, tg)
    te_ref[...] = te
    tg_ref[...] = tg


def _epilogue_specs(l, d):
    in_specs = [pl.BlockSpec((None, None, 1, d), lambda i, s, p, n: (l, 0, 0, 0)),
                pl.BlockSpec((None, None, 1, d), lambda i, s, p, n: (l, 0, 0, 0)),
                pl.BlockSpec((None, d, LANES), lambda i, s, p, n: (l, 0, 0)),
                pl.BlockSpec((None, 1, LANES), lambda i, s, p, n: (l, 0, 0))]
    out_specs = [pl.BlockSpec((TOKEN_TILE, d), lambda i, s, p, n: (i, 0)),
                 pl.BlockSpec((TOKEN_TILE, d), lambda i, s, p, n: (i, 0)),
                 pl.BlockSpec((TOKEN_TILE, LANES), lambda i, s, p, n: (i, 0)),
                 pl.BlockSpec((TOKEN_TILE, LANES), lambda i, s, p, n: (i, 0)),
                 pl.BlockSpec((SUBLANES, LANES), lambda i, s, p, n: (0, 0))]
    return in_specs, out_specs


def _epilogue_out_shapes(t, d):
    return [jax.ShapeDtypeStruct((t, d), F32), jax.ShapeDtypeStruct((t, d), F32),
            jax.ShapeDtypeStruct((t, LANES), jnp.int32), jax.ShapeDtypeStruct((t, LANES), F32),
            jax.ShapeDtypeStruct((SUBLANES, LANES), jnp.int32)]


_EPILOGUE_SCRATCH = [pltpu.VMEM((SUBLANES, LANES), F32)]


def _even_body(alpha, seg_ref, prev_ref, next_ref,
               x_ref, xp_ref, xn_ref, mod_ref, win_ref, sg_ref, sb_ref, wsp_ref, bsp_ref,
               cw_ref, cb_ref, cg_ref, cbb_ref, wout_ref, lng_ref, lnb_ref, wr_ref, br_ref,
               x1_ref, h2_ref, te_ref, tg_ref, cnt_ref, gl_scr, cnt_scr):
    i = pl.program_id(0)
    tt = x_ref.shape[0]
    ca = sg_ref.shape[-1]
    cb2 = 2 * ca
    x = x_ref[...]
    sc = 1.0 + mod_ref[1:2, :]
    sh = mod_ref[0:1, :]
    z = _dot((x * sc + sh).astype(BF16), win_ref[...])

    u = _gelu(z[:, :ca])
    v = _ln(_gelu(z[:, ca:cb2]), sg_ref[...], sb_ref[...]).astype(BF16)
    half = lax.broadcasted_iota(jnp.int32, (CHUNK, LANES), 1) < (LANES // 2)
    chunks = []
    for ck in range(tt // CHUNK):
        cols = []
        for j in range(ca // LANES):
            vblk = v[ck * CHUNK:(ck + 1) * CHUNK, j * LANES:(j + 1) * LANES]
            cols.append(jnp.where(half, _dot(wsp_ref[2 * j], vblk), _dot(wsp_ref[2 * j + 1], vblk)))
        chunks.append(jnp.concatenate(cols, axis=1) + bsp_ref[...])
    y_a = u * jnp.concatenate(chunks, axis=0)

    def glu_rows(xh_ref):
        zh = _dot((xh_ref[...] * sc + sh).astype(BF16), win_ref[:, cb2:])
        return zh[:, :ca] * _sigmoid(zh[:, ca:])

    gl_scr[0:HALO, :] = jnp.where(prev_ref[i] > 0, glu_rows(xp_ref), 0.0)
    gl_scr[HALO:HALO + tt, :] = z[:, cb2:cb2 + ca] * _sigmoid(z[:, cb2 + ca:])
    gl_scr[HALO + tt:, :] = jnp.where(next_ref[i] > 0, glu_rows(xn_ref), 0.0)
    off = HALO - CONV_K // 2
    dc = gl_scr[off:off + tt, :] * cw_ref[0:1, :]
    for k in range(1, CONV_K):
        dc = dc + gl_scr[off + k:off + k + tt, :] * cw_ref[k:k + 1, :]
    yb = _ln(dc + cb_ref[...], cg_ref[...], cbb_ref[...])
    y_b = yb * _sigmoid(yb)

    y = _dot(jnp.concatenate([y_a, y_b], axis=1).astype(BF16), wout_ref[...])
    _post_mixer(alpha, x, y, mod_ref, lng_ref, lnb_ref, wr_ref, br_ref,
                x1_ref, h2_ref, te_ref, tg_ref, cnt_ref, cnt_scr)


def _even_layer(l, alpha, meta, x, mod, p):
    t, d = x.shape
    li = l // 2
    nh = TOKEN_TILE // HALO
    n_halo = t // HALO
    ca = p["sgu_g"].shape[-1]
    ep_in, ep_out = _epilogue_specs(l, d)
    const3 = lambda i, s, pv, nx: (li, 0, 0)
    in_specs = [
        pl.BlockSpec((TOKEN_TILE, d), lambda i, s, pv, nx: (i, 0)),
        pl.BlockSpec((HALO, d), lambda i, s, pv, nx: (jnp.maximum(i * nh - 1, 0), 0)),
        pl.BlockSpec((HALO, d), lambda i, s, pv, nx: (jnp.minimum((i + 1) * nh, n_halo - 1), 0)),
        pl.BlockSpec((None, None, 6, d), lambda i, s, pv, nx: (l, s[i], 0, 0)),
        pl.BlockSpec((None, d, 4 * ca), const3),
        pl.BlockSpec((None, 1, ca), const3),
        pl.BlockSpec((None, 1, ca), const3),
        pl.BlockSpec((None, G_A, CHUNK, CHUNK), lambda i, s, pv, nx: (li, 0, 0, 0)),
        pl.BlockSpec((None, CHUNK, ca), const3),
        pl.BlockSpec((None, CONV_K, ca), const3),
        pl.BlockSpec((None, 1, ca), const3),
        pl.BlockSpec((None, 1, ca), const3),
        pl.BlockSpec((None, 1, ca), const3),
        pl.BlockSpec((None, 2 * ca, d), const3),
    ] + ep_in
    return pl.pallas_call(
        functools.partial(_even_body, alpha),
        grid_spec=pltpu.PrefetchScalarGridSpec(
            num_scalar_prefetch=3, grid=(t // TOKEN_TILE,),
            in_specs=in_specs, out_specs=ep_out,
            scratch_shapes=[pltpu.VMEM((TOKEN_TILE + 2 * HALO, ca), F32)] + _EPILOGUE_SCRATCH),
        out_shape=_epilogue_out_shapes(t, d),
        compiler_params=_params(1),
        name=f"even_mixer_{l}",
    )(meta["seg"], meta["prev"], meta["next"], x, x, x, mod,
      p["w_in"], p["sgu_g"], p["sgu_b"], p["w_sp"], p["b_sp"], p["conv_w"], p["conv_b"],
      p["cln_g"], p["cln_b"], p["w_out_ab"], p["ln_g"], p["ln_b"], p["w_router"], p["b_router"])


def _qkv_body(seg_ref, x_ref, mod_ref, w_ref, q_ref, k_ref, v_ref, k32_ref, v32_ref):
    d = x_ref.shape[1]
    h = (x_ref[...] * (1.0 + mod_ref[1:2, :]) + mod_ref[0:1, :]).astype(BF16)
    qkv = _dot(h, w_ref[...])
    q_ref[...] = qkv[:, :d].astype(BF16)
    k = qkv[:, d:2 * d]
    v = qkv[:, 2 * d:]
    k32_ref[...] = k
    v32_ref[...] = v
    k_ref[...] = k.astype(BF16)
    v_ref[...] = v.astype(BF16)


def _qkv_proj(l, meta, x, mod, w_qkv):
    t, d = x.shape
    li = l // 2
    tile = pl.BlockSpec((TOKEN_TILE, d), lambda i, s: (i, 0))
    return pl.pallas_call(
        _qkv_body,
        grid_spec=pltpu.PrefetchScalarGridSpec(
            num_scalar_prefetch=1, grid=(t // TOKEN_TILE,),
            in_specs=[tile,
                      pl.BlockSpec((None, None, 6, d), lambda i, s: (l, s[i], 0, 0)),
                      pl.BlockSpec((None, d, 3 * d), lambda i, s: (li, 0, 0))],
            out_specs=[tile] * 5),
        out_shape=[jax.ShapeDtypeStruct((t, d), BF16)] * 3 + [jax.ShapeDtypeStruct((t, d), F32)] * 2,
        compiler_params=_params(1),
        name=f"qkv_proj_{l}",
    )(meta["seg"], x, mod, w_qkv)


def _head_pair_attention(q2, k_parts, v_parts, bias_parts, scale):
    lane = lax.broadcasted_iota(jnp.int32, q2.shape, 1)
    outs = []
    for hh in range(2):
        qm = jnp.where((lane >= hh * (LANES // 2)) & (lane < (hh + 1) * (LANES // 2)), q2,
                       jnp.zeros_like(q2))
        ss = []
        for j, kp in enumerate(k_parts):
            s = _dot_nt(qm, kp) * scale
            if bias_parts[hh][j] is not None:
                s = s + bias_parts[hh][j]
            ss.append(s)
        m = ss[0].max(axis=-1, keepdims=True)
        for s in ss[1:]:
            m = jnp.maximum(m, s.max(axis=-1, keepdims=True))
        den = None
        o = None
        for s, vp in zip(ss, v_parts):
            e = jnp.exp(s - m)
            es = e.sum(axis=-1, keepdims=True)
            den = es if den is None else den + es
            pv = _dot(e.astype(BF16), vp)
            o = pv if o is None else o + pv
        outs.append(o / den)
    return jnp.where(lane < LANES // 2, outs[0], outs[1])


def _attn_ctx_body(scale, q_ref, k_ref, v_ref, o_ref):
    d = q_ref.shape[1]
    for pr in range(d // LANES):
        sl = slice(pr * LANES, (pr + 1) * LANES)
        o = _head_pair_attention(q_ref[:, sl], [k_ref[:, sl]], [v_ref[:, sl]],
                                 [[None], [None]], scale)
        o_ref[:, sl] = o.astype(o_ref.dtype)


def _attn_ctx(q, k, v, n_seq, seq, scale):
    d = q.shape[1]
    blk = pl.BlockSpec((seq, d), lambda b: (b, 0))
    return pl.pallas_call(
        functools.partial(_attn_ctx_body, scale),
        grid=(n_seq,),
        in_specs=[blk, blk, blk],
        out_specs=blk,
        out_shape=jax.ShapeDtypeStruct((n_seq * seq, d), BF16),
        compiler_params=_params(1),
        name="attn_ctx",
    )(q, k, v)


def _attn_lat_body(scale, cls_ref, kb_ref, q_ref, k0_ref, k1_ref, k2_ref, v0_ref, v1_ref, v2_ref,
                   ck_ref, cv_ref, bias_ref, o_ref):
    d = q_ref.shape[1]
    for pr in range(d // LANES):
        sl = slice(pr * LANES, (pr + 1) * LANES)
        k_loc = jnp.concatenate([k0_ref[:, sl], k1_ref[:, sl], k2_ref[:, sl]], axis=0)
        v_loc = jnp.concatenate([v0_ref[:, sl], v1_ref[:, sl], v2_ref[:, sl]], axis=0)
        ck = ck_ref[:, sl].astype(BF16)
        cv = cv_ref[:, sl].astype(BF16)
        o = _head_pair_attention(q_ref[:, sl], [k_loc, ck], [v_loc, cv],
                                 [[bias_ref[2 * pr], None], [bias_ref[2 * pr + 1], None]], scale)
        o_ref[:, sl] = o.astype(o_ref.dtype)


def _latent_window_tables(rows_n):
    wr = min(WIN_ROWS, rows_n)
    n_rt = rows_n // Q_ROWS
    kstart = np.clip(np.arange(n_rt) * Q_ROWS - wr // 2, 0, rows_n - K_ROWS)
    kstart = (kstart // Q_ROWS) * Q_ROWS
    patterns, cls = [], []
    for rt in range(n_rt):
        pat = np.full((Q_ROWS, K_ROWS), -1, np.int64)
        for qi in range(Q_ROWS):
            r = rt * Q_ROWS + qi
            rs = int(np.clip(r - wr // 2, 0, rows_n - wr))
            for kj in range(K_ROWS):
                kr = int(kstart[rt]) + kj
                if rs <= kr < rs + wr:
                    pat[qi, kj] = kr - r + WIN_ROWS - 1
        assert (pat >= 0).sum(axis=1).min() == wr, "key block does not cover the window"
        key = pat.tobytes()
        if key not in [p.tobytes() for p in patterns]:
            patterns.append(pat)
        cls.append([p.tobytes() for p in patterns].index(key))
    return (kstart // Q_ROWS).astype(np.int32), np.asarray(cls, np.int32), np.stack(patterns)


def _latent_bias(rpb, patterns):
    h = rpb.shape[0]
    qc = np.arange(GRID_W)[:, None]
    kc = np.arange(GRID_W)[None, :]
    qcs = np.clip(qc - WIN_COLS // 2, 0, GRID_W - WIN_COLS)
    col_ok = (kc >= qcs) & (kc < qcs + WIN_COLS)
    dc = np.clip(kc - qc + WIN_COLS - 1, 0, 2 * WIN_COLS - 2)
    onehot = (dc[None] == np.arange(2 * WIN_COLS - 1)[:, None, None]) & col_ok[None]
    cm = jnp.einsum("hrd,dqk->hrqk", rpb, jnp.asarray(onehot, F32), precision=lax.Precision.HIGHEST)
    cm = jnp.where(jnp.asarray(col_ok), cm, NEG_INF)
    cx = jnp.concatenate([cm, jnp.full((h, 1, GRID_W, GRID_W), NEG_INF, F32)], axis=1)
    idx = np.where(patterns >= 0, patterns, 2 * WIN_ROWS - 1)
    b = jnp.take(cx, jnp.asarray(idx.reshape(-1), jnp.int32), axis=1)
    b = b.reshape(h, idx.shape[0], Q_ROWS, K_ROWS, GRID_W, GRID_W)
    b = jnp.transpose(b, (1, 0, 2, 4, 3, 5))
    return b.reshape(idx.shape[0], h, Q_ROWS * GRID_W, K_ROWS * GRID_W)


def _attn_lat(q, k, v, ck, cv, bias, tables, tok0, n_batch, n_tok, scale):
    d = q.shape[1]
    kblk, cls, _ = tables
    n_rt = kblk.shape[0]
    qt = Q_ROWS * GRID_W
    base = tok0 // qt
    per_b = n_tok // qt
    h = bias.shape[1]
    lc = ck.shape[1]

    def kv_spec(j):
        return pl.BlockSpec((qt, d), lambda b, r, c, kb: (base + b * per_b + kb[r] + j, 0))

    return pl.pallas_call(
        functools.partial(_attn_lat_body, scale),
        grid_spec=pltpu.PrefetchScalarGridSpec(
            num_scalar_prefetch=2, grid=(n_batch, n_rt),
            in_specs=[pl.BlockSpec((qt, d), lambda b, r, c, kb: (base + b * per_b + r, 0)),
                      kv_spec(0), kv_spec(1), kv_spec(2), kv_spec(0), kv_spec(1), kv_spec(2),
                      pl.BlockSpec((None, lc, d), lambda b, r, c, kb: (b, 0, 0)),
                      pl.BlockSpec((None, lc, d), lambda b, r, c, kb: (b, 0, 0)),
                      pl.BlockSpec((None, h, qt, K_ROWS * GRID_W), lambda b, r, c, kb: (c[r], 0, 0, 0))],
            out_specs=pl.BlockSpec((qt, d), lambda b, r, c, kb: (b * per_b + r, 0))),
        out_shape=jax.ShapeDtypeStruct((n_batch * n_tok, d), BF16),
        compiler_params=_params(2),
        name="attn_latent",
    )(jnp.asarray(cls), jnp.asarray(kblk), q, k, k, k, v, v, v, ck, cv, bias)


def _proj_body(alpha, seg_ref, prev_ref, next_ref, x_ref, o_ref, mod_ref, w_ref,
               lng_ref, lnb_ref, wr_ref, br_ref, x1_ref, h2_ref, te_ref, tg_ref, cnt_ref, cnt_scr):
    y = _dot(o_ref[...], w_ref[...])
    _post_mixer(alpha, x_ref[...], y, mod_ref, lng_ref, lnb_ref, wr_ref, br_ref,
                x1_ref, h2_ref, te_ref, tg_ref, cnt_ref, cnt_scr)


def _odd_out_proj(l, alpha, meta, x, o, mod, p):
    t, d = x.shape
    li = l // 2
    ep_in, ep_out = _epilogue_specs(l, d)
    tile = pl.BlockSpec((TOKEN_TILE, d), lambda i, s, pv, nx: (i, 0))
    return pl.pallas_call(
        functools.partial(_proj_body, alpha),
        grid_spec=pltpu.PrefetchScalarGridSpec(
            num_scalar_prefetch=3, grid=(t // TOKEN_TILE,),
            in_specs=[tile, tile,
                      pl.BlockSpec((None, None, 6, d), lambda i, s, pv, nx: (l, s[i], 0, 0)),
                      pl.BlockSpec((None, d, d), lambda i, s, pv, nx: (li, 0, 0))] + ep_in,
            out_specs=ep_out, scratch_shapes=_EPILOGUE_SCRATCH),
        out_shape=_epilogue_out_shapes(t, d),
        compiler_params=_params(1),
        name=f"attn_out_proj_{l}",
    )(meta["seg"], meta["prev"], meta["next"], x, o, mod, p["w_out_c"],
      p["ln_g"], p["ln_b"], p["w_router"], p["b_router"])


def _route_tables(te, counts, n_experts, n_tiles):
    t = te.shape[0]
    counts = counts[0, :n_experts]
    padded = (counts + MOE_TILE - 1) // MOE_TILE * MOE_TILE
    pad_end = jnp.cumsum(padded).astype(jnp.int32)
    pad_start = pad_end - padded
    experts, ranks = te[:, :TOP_K], te[:, TOP_K:2 * TOP_K]
    sel = experts[:, :, None] == jnp.arange(n_experts, dtype=jnp.int32)
    pos = ranks + jnp.sum(jnp.where(sel, pad_start, 0), axis=-1)
    n_used = (pad_end[-1] // MOE_TILE).astype(jnp.int32)
    tile_start = jnp.arange(n_tiles, dtype=jnp.int32) * MOE_TILE
    tile_e = jnp.sum(tile_start[:, None] >= pad_end[None, :], axis=1)
    tile_e = jnp.minimum(tile_e, n_experts - 1).astype(jnp.int32)
    last_e = jnp.sum(jnp.where(jnp.arange(n_tiles) == n_used - 1, tile_e, 0))
    tile_e = jnp.where(jnp.arange(n_tiles) < n_used, tile_e, last_e)
    pos = pos.astype(jnp.int32).reshape(t // TOKEN_TILE, 1, TOKEN_TILE * TOP_K)
    return tile_e, n_used.reshape(1), pad_end, pos


def _dispatch_body(pend_ref, pos_ref, h2_ref, xs_hbm, stage, sem, zsem):
    i = pl.program_id(0)
    n_steps = pl.num_programs(0)
    slot = i % 2
    tt = h2_ref.shape[0]
    n_e = pend_ref.shape[0]

    def scatter_wait(s):
        for _ in range(TOP_K):
            pltpu.make_async_copy(stage.at[s], xs_hbm.at[pl.ds(0, tt)], sem.at[s]).wait()

    @pl.when(i == 0)
    def _():
        stage[0] = jnp.zeros(stage.shape[1:], stage.dtype)
        n_tiles = xs_hbm.shape[0] // tt
        n_used = pend_ref[n_e - 1] // tt
        for phase in range(2):
            for e in range(n_e):
                lo = pend_ref[e - 1] if e else 0
                for cond, row0 in ((pend_ref[e] > lo, pend_ref[e] - tt),
                                   (n_used + e < n_tiles, (n_used + e) * tt)):
                    @pl.when(cond)
                    def _():
                        cp = pltpu.make_async_copy(
                            stage.at[0], xs_hbm.at[pl.ds(pl.multiple_of(row0, tt), tt)], zsem)
                        if phase == 0:
                            cp.start()
                        else:
                            cp.wait()

    @pl.when(i >= 2)
    def _():
        scatter_wait(slot)

    stage[slot] = h2_ref[...]

    def body(r, c):
        for k in range(TOP_K):
            pltpu.make_async_copy(stage.at[slot, pl.ds(r, 1)],
                                  xs_hbm.at[pl.ds(pos_ref[0, r * TOP_K + k], 1)], sem.at[slot]).start()
        return c
    lax.fori_loop(0, tt, body, 0, unroll=4)

    @pl.when(i == n_steps - 1)
    def _():
        @pl.when(n_steps >= 2)
        def _():
            scatter_wait(1 - slot)
        scatter_wait(slot)


def _moe_dispatch(l, h2, pad_end, pos, n_tiles):
    t, d = h2.shape
    assert TOKEN_TILE == MOE_TILE
    return pl.pallas_call(
        _dispatch_body,
        grid_spec=pltpu.PrefetchScalarGridSpec(
            num_scalar_prefetch=1, grid=(t // TOKEN_TILE,),
            in_specs=[pl.BlockSpec((None, 1, TOKEN_TILE * TOP_K), lambda i, pe: (i, 0, 0),
                                   memory_space=pltpu.SMEM),
                      pl.BlockSpec((TOKEN_TILE, d), lambda i, pe: (i, 0))],
            out_specs=pl.BlockSpec(memory_space=pl.ANY),
            scratch_shapes=[pltpu.VMEM((2, TOKEN_TILE, d), F32),
                            pltpu.SemaphoreType.DMA((2,)),
                            pltpu.SemaphoreType.DMA]),
        out_shape=jax.ShapeDtypeStruct((n_tiles * MOE_TILE, d), F32),
        compiler_params=_params(1),
        name=f"moe_dispatch_{l}",
    )(pad_end, pos, h2)


def _moe_body(te_ref, nu_ref, x_ref, wgu_ref, bgu_ref, wd_ref, bd_ref, y_ref, wgu_bf, wd_bf):
    i = pl.program_id(0)
    de = wd_bf.shape[0]

    @pl.when(i < nu_ref[0])
    def _():
        @pl.when((i == 0) | (te_ref[i] != te_ref[jnp.maximum(i - 1, 0)]))
        def _():
            wgu_bf[...] = wgu_ref[...].astype(BF16)
            wd_bf[...] = wd_ref[...].astype(BF16)

        hgu = _dot(x_ref[...].astype(BF16), wgu_bf[...]) + bgu_ref[...]
        x_glu = jnp.minimum(hgu[:, :de], SWIGLU_LIMIT)
        x_lin = jnp.clip(hgu[:, de:], -SWIGLU_LIMIT, SWIGLU_LIMIT)
        act = x_glu * _sigmoid(SWIGLU_ALPHA * x_glu) * (x_lin + 1.0)
        y_ref[...] = _dot(act.astype(BF16), wd_bf[...]) + bd_ref[...]

    @pl.when(i >= nu_ref[0])
    def _():
        y_ref[...] = jnp.zeros(y_ref.shape, y_ref.dtype)


def _moe_experts(l, xs, tile_e, n_used, w_gu, b_gu, w_d, b_d):
    n_rows, d = xs.shape
    n_tiles = tile_e.shape[0]
    n_e, de = w_d.shape[1], w_d.shape[2]
    row_blk = pl.BlockSpec((MOE_TILE, d), lambda i, te, nu: (jnp.minimum(i, nu[0] - 1), 0))
    return pl.pallas_call(
        _moe_body,
        grid_spec=pltpu.PrefetchScalarGridSpec(
            num_scalar_prefetch=2, grid=(n_tiles,),
            in_specs=[row_blk,
                      pl.BlockSpec((None, None, d, 2 * de), lambda i, te, nu: (l, te[i], 0, 0)),
                      pl.BlockSpec((None, None, 1, 2 * de), lambda i, te, nu: (l, te[i], 0, 0)),
                      pl.BlockSpec((None, None, de, d), lambda i, te, nu: (l, te[i], 0, 0)),
                      pl.BlockSpec((None, None, 1, d), lambda i, te, nu: (l, te[i], 0, 0))],
            out_specs=pl.BlockSpec((MOE_TILE, d), lambda i, te, nu: (i, 0)),
            scratch_shapes=[pltpu.VMEM((d, 2 * de), BF16), pltpu.VMEM((de, d), BF16)]),
        out_shape=jax.ShapeDtypeStruct((n_rows, d), F32),
        compiler_params=_params(1),
        name=f"moe_experts_{l}",
    )(tile_e, n_used, xs, w_gu, b_gu.reshape(b_gu.shape[0], n_e, 1, 2 * de), w_d,
      b_d.reshape(b_d.shape[0], n_e, 1, d))


def _combine_body(alpha, seg_ref, pos_ref, posn_ref, x1_ref, g_ref, mod_ref, lng_ref, lnb_ref, yb_hbm,
                  o_ref, buf, sem):
    i = pl.program_id(0)
    n_steps = pl.num_programs(0)
    slot = i % 2
    tt = x1_ref.shape[0]

    def gather_start(idx_ref, s):
        def body(r, c):
            for k in range(TOP_K):
                pltpu.make_async_copy(yb_hbm.at[pl.ds(idx_ref[0, r * TOP_K + k], 1)],
                                      buf.at[s, k, pl.ds(r, 1)], sem.at[s]).start()
            return c
        lax.fori_loop(0, tt, body, 0, unroll=4)

    @pl.when(i == 0)
    def _():
        gather_start(pos_ref, 0)

    @pl.when(i + 1 < n_steps)
    def _():
        gather_start(posn_ref, 1 - slot)

    for k in range(TOP_K):
        pltpu.make_async_copy(yb_hbm.at[pl.ds(0, tt)], buf.at[slot, k], sem.at[slot]).wait()
    g = g_ref[...]
    y = g[:, 0:1] * buf[slot, 0]
    for k in range(1, TOP_K):
        y = y + g[:, k:k + 1] * buf[slot, k]
    o_ref[...] = _ln(alpha * x1_ref[...] + mod_ref[5:6, :] * y, lng_ref[...], lnb_ref[...])


def _moe_combine(l, alpha, meta, x1, yb, pos, gates, mod, ln_g, ln_b):
    t, d = x1.shape
    nt = t // TOKEN_TILE
    tile = pl.BlockSpec((TOKEN_TILE, d), lambda i, s: (i, 0))
    pos_blk = lambda f: pl.BlockSpec((None, 1, TOKEN_TILE * TOP_K), f, memory_space=pltpu.SMEM)
    return pl.pallas_call(
        functools.partial(_combine_body, alpha),
        grid_spec=pltpu.PrefetchScalarGridSpec(
            num_scalar_prefetch=1, grid=(nt,),
            in_specs=[pos_blk(lambda i, s: (i, 0, 0)),
                      pos_blk(lambda i, s: (jnp.minimum(i + 1, nt - 1), 0, 0)),
                      tile,
                      pl.BlockSpec((TOKEN_TILE, LANES), lambda i, s: (i, 0)),
                      pl.BlockSpec((None, None, 6, d), lambda i, s: (l, s[i], 0, 0)),
                      pl.BlockSpec((None, None, 1, d), lambda i, s: (l, 1, 0, 0)),
                      pl.BlockSpec((None, None, 1, d), lambda i, s: (l, 1, 0, 0)),
                      pl.BlockSpec(memory_space=pl.ANY)],
            out_specs=tile,
            scratch_shapes=[pltpu.VMEM((2, TOP_K, TOKEN_TILE, d), F32),
                            pltpu.SemaphoreType.DMA((2,))]),
        out_shape=jax.ShapeDtypeStruct((t, d), F32),
        compiler_params=_params(1),
        name=f"moe_combine_{l}",
    )(meta["seg"], pos, pos, x1, gates, mod, ln_g, ln_b, yb)


def _token_meta(n_ctx_seq, seq, n_lat, lat_seq):
    seg, prev, nxt = [], [], []
    for n_seq, length, seg_of in ((n_ctx_seq, seq, lambda b: 0), (n_lat, lat_seq, lambda b: 1 + b)):
        per = length // TOKEN_TILE
        for b in range(n_seq):
            for j in range(per):
                seg.append(seg_of(b))
                prev.append(int(j > 0))
                nxt.append(int(j < per - 1))
    as_i32 = lambda a: jnp.asarray(np.asarray(a, np.int32))
    return {"seg": as_i32(seg), "prev": as_i32(prev), "next": as_i32(nxt)}


def kernel(x_prompt, x_sample, c, cache_k, cache_v, c_ctx, w_mod, b_mod, ln_g, ln_b, w_in_ab, sgu_ln_g, sgu_ln_b, w_spatial, b_spatial, conv_w, conv_b, conv_ln_g, conv_ln_b, w_out_ab, w_qkv, rpb, w_out_c, w_router, b_router, w_gate_up, b_gate_up, w_down, b_down):
    n_ctx_seq, seq, d = x_prompt.shape
    n_lat, lat_seq, _ = x_sample.shape
    depth = w_mod.shape[0]
    n_heads, head_dim = cache_k.shape[3], cache_k.shape[4]
    n_experts = w_router.shape[-1]
    ca = sgu_ln_g.shape[-1]
    n_even, n_odd = w_in_ab.shape[0], w_qkv.shape[0]
    t_ctx, t_lat = n_ctx_seq * seq, n_lat * lat_seq
    t = t_ctx + t_lat
    rows_n = lat_seq // GRID_W
    assert seq % TOKEN_TILE == 0 and lat_seq % TOKEN_TILE == 0 and 1 + n_lat <= SUBLANES
    assert TOKEN_TILE % CHUNK == 0 and HALO >= CONV_K // 2 and ca == w_out_ab.shape[1] // 2
    assert rows_n % Q_ROWS == 0 and rows_n >= K_ROWS and t_ctx % (Q_ROWS * GRID_W) == 0
    assert n_heads * head_dim == d and 2 * head_dim == LANES and n_experts <= LANES
    assert (t * TOP_K) % MOE_TILE == 0
    alpha = float((2 * depth) ** 0.25)
    scale = float(head_dim ** -0.5)
    meta = _token_meta(n_ctx_seq, seq, n_lat, lat_seq)

    x = jnp.concatenate([x_prompt.reshape(t_ctx, d), x_sample.reshape(t_lat, d)], axis=0)
    cvec = jnp.zeros((SUBLANES, d), F32).at[0].set(c_ctx).at[1:1 + n_lat].set(c)
    mod = _modulation(cvec, w_mod, b_mod).reshape(depth, SUBLANES, 6, d)

    pad_e = LANES - n_experts
    common = {
        "ln_g": ln_g.reshape(depth, 2, 1, d), "ln_b": ln_b.reshape(depth, 2, 1, d),
        "w_router": jnp.pad(w_router, ((0, 0), (0, 0), (0, pad_e))).astype(BF16),
        "b_router": jnp.pad(b_router, ((0, 0), (0, pad_e)), constant_values=PAD_LOGIT).reshape(depth, 1, LANES),
    }
    even = dict(common)
    even.update({
        "w_in": w_in_ab.astype(BF16), "sgu_g": sgu_ln_g.reshape(n_even, 1, ca),
        "sgu_b": sgu_ln_b.reshape(n_even, 1, ca), "w_sp": w_spatial.astype(BF16),
        "b_sp": jnp.repeat(jnp.transpose(b_spatial, (0, 2, 1)), ca // G_A, axis=2),
        "conv_w": conv_w, "conv_b": conv_b.reshape(n_even, 1, ca),
        "cln_g": conv_ln_g.reshape(n_even, 1, ca), "cln_b": conv_ln_b.reshape(n_even, 1, ca),
        "w_out_ab": w_out_ab.astype(BF16)})
    odd = dict(common)
    odd["w_out_c"] = w_out_c.astype(BF16)
    w_qkv_bf = w_qkv.astype(BF16)
    lat_tables = _latent_window_tables(rows_n)
    n_tiles = t * TOP_K // MOE_TILE + n_experts

    new_k, new_v = [], []
    for l in range(depth):
        i = l // 2
        if l % 2 == 0:
            x1, h2, top_e, gates, counts = _even_layer(l, alpha, meta, x, mod, even)
        else:
            q, k, v, k32, v32 = _qkv_proj(l, meta, x, mod, w_qkv_bf)
            new_k.append(k32[:t_ctx].reshape(n_ctx_seq, seq, n_heads, head_dim))
            new_v.append(v32[:t_ctx].reshape(n_ctx_seq, seq, n_heads, head_dim))
            o_ctx = _attn_ctx(q, k, v, n_ctx_seq, seq, scale)
            bias = _latent_bias(rpb[i], lat_tables[2])
            o_lat = _attn_lat(q, k, v, cache_k[:, i].reshape(n_lat, -1, d),
                              cache_v[:, i].reshape(n_lat, -1, d), bias, lat_tables,
                              t_ctx, n_lat, lat_seq, scale)
            o = jnp.concatenate([o_ctx, o_lat], axis=0)
            x1, h2, top_e, gates, counts = _odd_out_proj(l, alpha, meta, x, o, mod, odd)
        tile_e, n_used, pad_end, pos = _route_tables(top_e, counts, n_experts, n_tiles)
        xs = _moe_dispatch(l, h2, pad_end, pos, n_tiles)
        yb = _moe_experts(l, xs, tile_e, n_used, w_gate_up, b_gate_up, w_down, b_down)
        x = _moe_combine(l, alpha, meta, x1, yb, pos, gates, mod, common["ln_g"], common["ln_b"])

    y_prompt = x[:t_ctx].reshape(n_ctx_seq, seq, d)
    y_sample = x[t_ctx:].reshape(n_lat, lat_seq, d)
    return (y_prompt, y_sample, jnp.stack(new_k, axis=1), jnp.stack(new_v, axis=1))
```

```python
import functools

import numpy as np
import jax
import jax.numpy as jnp
from jax import lax
from jax.experimental import pallas as pl
from jax.experimental.pallas import tpu as pltpu

F32 = jnp.float32
BF16 = jnp.bfloat16

GRID_W = 64
G_A = 8
CHUNK = 128
CONV_K = 31
WIN_ROWS = 8
WIN_COLS = 16
TOP_K = 4
SWIGLU_ALPHA = 1.702
SWIGLU_LIMIT = 7.0
LN_EPS = 1e-5
NEG_INF = -1e30

LANES = 128
SUBLANES = 8
VMEM_LIMIT = 56 * 1024 * 1024

TOKEN_TILE = 256
HALO = 16
MOE_TILE = 256
Q_ROWS = 4
K_ROWS = 12
PAD_LOGIT = -3e38


def _ln(x, g, b):
    mu = jnp.mean(x, axis=-1, keepdims=True)
    xc = x - mu
    var = jnp.mean(xc * xc, axis=-1, keepdims=True)
    return xc * lax.rsqrt(var + LN_EPS) * g + b


def _gelu(x):
    return 0.5 * x * (1.0 + jnp.tanh(0.7978845608028654 * (x + 0.044715 * (x * x * x))))


def _sigmoid(x):
    return jax.nn.sigmoid(x)


def _dot(a, b):
    return jnp.dot(a, b, preferred_element_type=F32)


def _dot_nt(a, b):
    return lax.dot_general(a, b, (((1,), (1,)), ((), ())), preferred_element_type=F32)


def _params(n_axes):
    return pltpu.CompilerParams(dimension_semantics=("arbitrary",) * n_axes,
                                vmem_limit_bytes=VMEM_LIMIT)


def _mod_body(c_ref, w_ref, b_ref, o_ref):
    c = c_ref[...]
    s = (c * _sigmoid(c)).astype(BF16)
    o_ref[...] = _dot(s, w_ref[...].astype(BF16)) + b_ref[...]


def _modulation(cvec, w_mod, b_mod):
    depth, d, n = w_mod.shape
    tn = n // 4
    return pl.pallas_call(
        _mod_body,
        grid=(depth, n // tn),
        in_specs=[pl.BlockSpec((SUBLANES, d), lambda l, j: (0, 0)),
                  pl.BlockSpec((None, d, tn), lambda l, j: (l, 0, j)),
                  pl.BlockSpec((None, 1, tn), lambda l, j: (l, 0, j))],
        out_specs=pl.BlockSpec((None, SUBLANES, tn), lambda l, j: (l, 0, j)),
        out_shape=jax.ShapeDtypeStruct((depth, SUBLANES, n), F32),
        compiler_params=_params(2),
        name="adaln_modulation",
    )(cvec, w_mod, b_mod.reshape(depth, 1, n))


def _post_mixer(alpha, x, y, mod_ref, lng_ref, lnb_ref, wr_ref, br_ref,
                x1_ref, h2_ref, te_ref, tg_ref, cnt_ref, cnt_scr):
    i = pl.program_id(0)
    x1 = _ln(alpha * x + mod_ref[2:3, :] * y, lng_ref[...], lnb_ref[...])
    x1_ref[...] = x1
    h2 = x1 * (1.0 + mod_ref[4:5, :]) + mod_ref[3:4, :]
    h2_ref[...] = h2
    logits = _dot(h2.astype(BF16), wr_ref[...]) + br_ref[...]
    tt = logits.shape[0]
    lane = lax.broadcasted_iota(jnp.int32, logits.shape, 1)
    vals, idxs = [], []
    for _ in range(TOP_K):
        m = jnp.max(logits, axis=-1, keepdims=True)
        idx = jnp.min(jnp.where(logits == m, lane, LANES), axis=-1, keepdims=True)
        vals.append(m)
        idxs.append(idx)
        logits = jnp.where(lane == idx, -jnp.inf, logits)
    exps = [jnp.exp(v - vals[0]) for v in vals]
    den = exps[0]
    for e in exps[1:]:
        den = den + e

    @pl.when(i == 0)
    def _():
        cnt_scr[...] = jnp.zeros(cnt_scr.shape, cnt_scr.dtype)

    onehot = jnp.zeros(logits.shape, F32)
    for k in range(TOP_K):
        onehot = onehot + (lane == idxs[k]).astype(F32)
    row = lax.broadcasted_iota(jnp.int32, (tt, tt), 0)
    col = lax.broadcasted_iota(jnp.int32, (tt, tt), 1)
    before = _dot((row > col).astype(BF16), onehot.astype(BF16)) + cnt_scr[0:1, :]
    cnt = cnt_scr[...] + jnp.sum(onehot, axis=0, keepdims=True)
    cnt_scr[...] = cnt
    cnt_ref[...] = cnt.astype(jnp.int32)

    te = jnp.zeros(lane.shape, jnp.int32)
    tg = jnp.zeros(lane.shape, F32)
    for k in range(TOP_K):
        rank = jnp.sum(jnp.where(lane == idxs[k], before, 0.0), axis=-1, keepdims=True)
        te = jnp.where(lane == k, idxs[k], te)
        te = jnp.where(lane == TOP_K + k, rank.astype(jnp.int32), te)
        tg = jnp.where(lane == k, exps[k] / den, tg)
    te_ref[...] = te
    tg_ref[...] = tg


def _epilogue_specs(l, d):
    in_specs = [pl.BlockSpec((None, None, 1, d), lambda i, s, p, n: (l, 0, 0, 0)),
                pl.BlockSpec((None, None, 1, d), lambda i, s, p, n: (l, 0, 0, 0)),
                pl.BlockSpec((None, d, LANES), lambda i, s, p, n: (l, 0, 0)),
                pl.BlockSpec((None, 1, LANES), lambda i, s, p, n: (l, 0, 0))]
    out_specs = [pl.BlockSpec((TOKEN_TILE, d), lambda i, s, p, n: (i, 0)),
                 pl.BlockSpec((TOKEN_TILE, d), lambda i, s, p, n: (i, 0)),
                 pl.BlockSpec((TOKEN_TILE, LANES), lambda i, s, p, n: (i, 0)),
                 pl.BlockSpec((TOKEN_TILE, LANES), lambda i, s, p, n: (i, 0)),
                 pl.BlockSpec((SUBLANES, LANES), lambda i, s, p, n: (0, 0))]
    return in_specs, out_specs


def _epilogue_out_shapes(t, d):
    return [jax.ShapeDtypeStruct((t, d), F32), jax.ShapeDtypeStruct((t, d), F32),
            jax.ShapeDtypeStruct((t, LANES), jnp.int32), jax.ShapeDtypeStruct((t, LANES), F32),
            jax.ShapeDtypeStruct((SUBLANES, LANES), jnp.int32)]


_EPILOGUE_SCRATCH = [pltpu.VMEM((SUBLANES, LANES), F32)]


def _even_body(alpha, seg_ref, prev_ref, next_ref,
               x_ref, xp_ref, xn_ref, mod_ref, win_ref, sg_ref, sb_ref, wsp_ref, bsp_ref,
               cw_ref, cb_ref, cg_ref, cbb_ref, wout_ref, lng_ref, lnb_ref, wr_ref, br_ref,
               x1_ref, h2_ref, te_ref, tg_ref, cnt_ref, gl_scr, cnt_scr):
    i = pl.program_id(0)
    tt = x_ref.shape[0]
    ca = sg_ref.shape[-1]
    cb2 = 2 * ca
    x = x_ref[...]
    sc = 1.0 + mod_ref[1:2, :]
    sh = mod_ref[0:1, :]
    z = _dot((x * sc + sh).astype(BF16), win_ref[...])

    u = _gelu(z[:, :ca])
    v = _ln(_gelu(z[:, ca:cb2]), sg_ref[...], sb_ref[...]).astype(BF16)
    half = lax.broadcasted_iota(jnp.int32, (CHUNK, LANES), 1) < (LANES // 2)
    chunks = []
    for ck in range(tt // CHUNK):
        cols = []
        for j in range(ca // LANES):
            vblk = v[ck * CHUNK:(ck + 1) * CHUNK, j * LANES:(j + 1) * LANES]
            cols.append(jnp.where(half, _dot(wsp_ref[2 * j], vblk), _dot(wsp_ref[2 * j + 1], vblk)))
        chunks.append(jnp.concatenate(cols, axis=1) + bsp_ref[...])
    y_a = u * jnp.concatenate(chunks, axis=0)

    def glu_rows(xh_ref):
        zh = _dot((xh_ref[...] * sc + sh).astype(BF16), win_ref[:, cb2:])
        return zh[:, :ca] * _sigmoid(zh[:, ca:])

    gl_scr[0:HALO, :] = jnp.where(prev_ref[i] > 0, glu_rows(xp_ref), 0.0)
    gl_scr[HALO:HALO + tt, :] = z[:, cb2:cb2 + ca] * _sigmoid(z[:, cb2 + ca:])
    gl_scr[HALO + tt:, :] = jnp.where(next_ref[i] > 0, glu_rows(xn_ref), 0.0)
    off = HALO - CONV_K // 2
    dc = gl_scr[off:off + tt, :] * cw_ref[0:1, :]
    for k in range(1, CONV_K):
        dc = dc + gl_scr[off + k:off + k + tt, :] * cw_ref[k:k + 1, :]
    yb = _ln(dc + cb_ref[...], cg_ref[...], cbb_ref[...])
    y_b = yb * _sigmoid(yb)

    y = _dot(jnp.concatenate([y_a, y_b], axis=1).astype(BF16), wout_ref[...])
    _post_mixer(alpha, x, y, mod_ref, lng_ref, lnb_ref, wr_ref, br_ref,
                x1_ref, h2_ref, te_ref, tg_ref, cnt_ref, cnt_scr)


def _even_layer(l, alpha, meta, x, mod, p):
    t, d = x.shape
    li = l // 2
    nh = TOKEN_TILE // HALO
    n_halo = t // HALO
    ca = p["sgu_g"].shape[-1]
    ep_in, ep_out = _epilogue_specs(l, d)
    const3 = lambda i, s, pv, nx: (li, 0, 0)
    in_specs = [
        pl.BlockSpec((TOKEN_TILE, d), lambda i, s, pv, nx: (i, 0)),
        pl.BlockSpec((HALO, d), lambda i, s, pv, nx: (jnp.maximum(i * nh - 1, 0), 0)),
        pl.BlockSpec((HALO, d), lambda i, s, pv, nx: (jnp.minimum((i + 1) * nh, n_halo - 1), 0)),
        pl.BlockSpec((None, None, 6, d), lambda i, s, pv, nx: (l, s[i], 0, 0)),
        pl.BlockSpec((None, d, 4 * ca), const3),
        pl.BlockSpec((None, 1, ca), const3),
        pl.BlockSpec((None, 1, ca), const3),
        pl.BlockSpec((None, G_A, CHUNK, CHUNK), lambda i, s, pv, nx: (li, 0, 0, 0)),
        pl.BlockSpec((None, CHUNK, ca), const3),
        pl.BlockSpec((None, CONV_K, ca), const3),
        pl.BlockSpec((None, 1, ca), const3),
        pl.BlockSpec((None, 1, ca), const3),
        pl.BlockSpec((None, 1, ca), const3),
        pl.BlockSpec((None, 2 * ca, d), const3),
    ] + ep_in
    return pl.pallas_call(
        functools.partial(_even_body, alpha),
        grid_spec=pltpu.PrefetchScalarGridSpec(
            num_scalar_prefetch=3, grid=(t // TOKEN_TILE,),
            in_specs=in_specs, out_specs=ep_out,
            scratch_shapes=[pltpu.VMEM((TOKEN_TILE + 2 * HALO, ca), F32)] + _EPILOGUE_SCRATCH),
        out_shape=_epilogue_out_shapes(t, d),
        compiler_params=_params(1),
        name=f"even_mixer_{l}",
    )(meta["seg"], meta["prev"], meta["next"], x, x, x, mod,
      p["w_in"], p["sgu_g"], p["sgu_b"], p["w_sp"], p["b_sp"], p["conv_w"], p["conv_b"],
      p["cln_g"], p["cln_b"], p["w_out_ab"], p["ln_g"], p["ln_b"], p["w_router"], p["b_router"])


def _qkv_body(seg_ref, x_ref, mod_ref, w_ref, q_ref, k_ref, v_ref, k32_ref, v32_ref):
    d = x_ref.shape[1]
    h = (x_ref[...] * (1.0 + mod_ref[1:2, :]) + mod_ref[0:1, :]).astype(BF16)
    qkv = _dot(h, w_ref[...])
    q_ref[...] = qkv[:, :d].astype(BF16)
    k = qkv[:, d:2 * d]
    v = qkv[:, 2 * d:]
    k32_ref[...] = k
    v32_ref[...] = v
    k_ref[...] = k.astype(BF16)
    v_ref[...] = v.astype(BF16)


def _qkv_proj(l, meta, x, mod, w_qkv):
    t, d = x.shape
    li = l // 2
    tile = pl.BlockSpec((TOKEN_TILE, d), lambda i, s: (i, 0))
    return pl.pallas_call(
        _qkv_body,
        grid_spec=pltpu.PrefetchScalarGridSpec(
            num_scalar_prefetch=1, grid=(t // TOKEN_TILE,),
            in_specs=[tile,
                      pl.BlockSpec((None, None, 6, d), lambda i, s: (l, s[i], 0, 0)),
                      pl.BlockSpec((None, d, 3 * d), lambda i, s: (li, 0, 0))],
            out_specs=[tile] * 5),
        out_shape=[jax.ShapeDtypeStruct((t, d), BF16)] * 3 + [jax.ShapeDtypeStruct((t, d), F32)] * 2,
        compiler_params=_params(1),
        name=f"qkv_proj_{l}",
    )(meta["seg"], x, mod, w_qkv)


def _head_pair_attention(q2, k_parts, v_parts, bias_parts, scale):
    lane = lax.broadcasted_iota(jnp.int32, q2.shape, 1)
    outs = []
    for hh in range(2):
        qm = jnp.where((lane >= hh * (LANES // 2)) & (lane < (hh + 1) * (LANES // 2)), q2,
                       jnp.zeros_like(q2))
        ss = []
        for j, kp in enumerate(k_parts):
            s = _dot_nt(qm, kp) * scale
            if bias_parts[hh][j] is not None:
                s = s + bias_parts[hh][j]
            ss.append(s)
        m = ss[0].max(axis=-1, keepdims=True)
        for s in ss[1:]:
            m = jnp.maximum(m, s.max(axis=-1, keepdims=True))
        den = None
        o = None
        for s, vp in zip(ss, v_parts):
            e = jnp.exp(s - m)
            es = e.sum(axis=-1, keepdims=True)
            den = es if den is None else den + es
            pv = _dot(e.astype(BF16), vp)
            o = pv if o is None else o + pv
        outs.append(o / den)
    return jnp.where(lane < LANES // 2, outs[0], outs[1])


def _attn_ctx_body(scale, q_ref, k_ref, v_ref, o_ref):
    d = q_ref.shape[1]
    for pr in range(d // LANES):
        sl = slice(pr * LANES, (pr + 1) * LANES)
        o = _head_pair_attention(q_ref[:, sl], [k_ref[:, sl]], [v_ref[:, sl]],
                                 [[None], [None]], scale)
        o_ref[:, sl] = o.astype(o_ref.dtype)


def _attn_ctx(q, k, v, n_seq, seq, scale):
    d = q.shape[1]
    blk = pl.BlockSpec((seq, d), lambda b: (b, 0))
    return pl.pallas_call(
        functools.partial(_attn_ctx_body, scale),
        grid=(n_seq,),
        in_specs=[blk, blk, blk],
        out_specs=blk,
        out_shape=jax.ShapeDtypeStruct((n_seq * seq, d), BF16),
        compiler_params=_params(1),
        name="attn_ctx",
    )(q, k, v)


def _attn_lat_body(scale, cls_ref, kb_ref, q_ref, k0_ref, k1_ref, k2_ref, v0_ref, v1_ref, v2_ref,
                   ck_ref, cv_ref, bias_ref, o_ref):
    d = q_ref.shape[1]
    for pr in range(d // LANES):
        sl = slice(pr * LANES, (pr + 1) * LANES)
        k_loc = jnp.concatenate([k0_ref[:, sl], k1_ref[:, sl], k2_ref[:, sl]], axis=0)
        v_loc = jnp.concatenate([v0_ref[:, sl], v1_ref[:, sl], v2_ref[:, sl]], axis=0)
        ck = ck_ref[:, sl].astype(BF16)
        cv = cv_ref[:, sl].astype(BF16)
        o = _head_pair_attention(q_ref[:, sl], [k_loc, ck], [v_loc, cv],
                                 [[bias_ref[2 * pr], None], [bias_ref[2 * pr + 1], None]], scale)
        o_ref[:, sl] = o.astype(o_ref.dtype)


def _latent_window_tables(rows_n):
    wr = min(WIN_ROWS, rows_n)
    n_rt = rows_n // Q_ROWS
    kstart = np.clip(np.arange(n_rt) * Q_ROWS - wr // 2, 0, rows_n - K_ROWS)
    kstart = (kstart // Q_ROWS) * Q_ROWS
    patterns, cls = [], []
    for rt in range(n_rt):
        pat = np.full((Q_ROWS, K_ROWS), -1, np.int64)
        for qi in range(Q_ROWS):
            r = rt * Q_ROWS + qi
            rs = int(np.clip(r - wr // 2, 0, rows_n - wr))
            for kj in range(K_ROWS):
                kr = int(kstart[rt]) + kj
                if rs <= kr < rs + wr:
                    pat[qi, kj] = kr - r + WIN_ROWS - 1
        assert (pat >= 0).sum(axis=1).min() == wr, "key block does not cover the window"
        key = pat.tobytes()
        if key not in [p.tobytes() for p in patterns]:
            patterns.append(pat)
        cls.append([p.tobytes() for p in patterns].index(key))
    return (kstart // Q_ROWS).astype(np.int32), np.asarray(cls, np.int32), np.stack(patterns)


def _latent_bias(rpb, patterns):
    h = rpb.shape[0]
    qc = np.arange(GRID_W)[:, None]
    kc = np.arange(GRID_W)[None, :]
    qcs = np.clip(qc - WIN_COLS // 2, 0, GRID_W - WIN_COLS)
    col_ok = (kc >= qcs) & (kc < qcs + WIN_COLS)
    dc = np.clip(kc - qc + WIN_COLS - 1, 0, 2 * WIN_COLS - 2)
    onehot = (dc[None] == np.arange(2 * WIN_COLS - 1)[:, None, None]) & col_ok[None]
    cm = jnp.einsum("hrd,dqk->hrqk", rpb, jnp.asarray(onehot, F32), precision=lax.Precision.HIGHEST)
    cm = jnp.where(jnp.asarray(col_ok), cm, NEG_INF)
    cx = jnp.concatenate([cm, jnp.full((h, 1, GRID_W, GRID_W), NEG_INF, F32)], axis=1)
    idx = np.where(patterns >= 0, patterns, 2 * WIN_ROWS - 1)
    classes = []
    for pat in idx:
        q_rows = [jnp.concatenate([cx[:, int(dr)] for dr in pat_q], axis=-1) for pat_q in pat]
        classes.append(jnp.concatenate(q_rows, axis=-2))
    return jnp.stack(classes, axis=0)


def _attn_lat(q, k, v, ck, cv, bias, tables, tok0, n_batch, n_tok, scale):
    d = q.shape[1]
    kblk, cls, _ = tables
    n_rt = kblk.shape[0]
    qt = Q_ROWS * GRID_W
    base = tok0 // qt
    per_b = n_tok // qt
    h = bias.shape[1]
    lc = ck.shape[1]

    def kv_spec(j):
        return pl.BlockSpec((qt, d), lambda b, r, c, kb: (base + b * per_b + kb[r] + j, 0))

    return pl.pallas_call(
        functools.partial(_attn_lat_body, scale),
        grid_spec=pltpu.PrefetchScalarGridSpec(
            num_scalar_prefetch=2, grid=(n_batch, n_rt),
            in_specs=[pl.BlockSpec((qt, d), lambda b, r, c, kb: (base + b * per_b + r, 0)),
                      kv_spec(0), kv_spec(1), kv_spec(2), kv_spec(0), kv_spec(1), kv_spec(2),
                      pl.BlockSpec((None, lc, d), lambda b, r, c, kb: (b, 0, 0)),
                      pl.BlockSpec((None, lc, d), lambda b, r, c, kb: (b, 0, 0)),
                      pl.BlockSpec((None, h, qt, K_ROWS * GRID_W), lambda b, r, c, kb: (c[r], 0, 0, 0))],
            out_specs=pl.BlockSpec((qt, d), lambda b, r, c, kb: (b * per_b + r, 0))),
        out_shape=jax.ShapeDtypeStruct((n_batch * n_tok, d), BF16),
        compiler_params=_params(2),
        name="attn_latent",
    )(jnp.asarray(cls), jnp.asarray(kblk), q, k, k, k, v, v, v, ck, cv, bias)


def _proj_body(alpha, seg_ref, prev_ref, next_ref, x_ref, o_ref, mod_ref, w_ref,
               lng_ref, lnb_ref, wr_ref, br_ref, x1_ref, h2_ref, te_ref, tg_ref, cnt_ref, cnt_scr):
    y = _dot(o_ref[...], w_ref[...])
    _post_mixer(alpha, x_ref[...], y, mod_ref, lng_ref, lnb_ref, wr_ref, br_ref,
                x1_ref, h2_ref, te_ref, tg_ref, cnt_ref, cnt_scr)


def _odd_out_proj(l, alpha, meta, x, o, mod, p):
    t, d = x.shape
    li = l // 2
    ep_in, ep_out = _epilogue_specs(l, d)
    tile = pl.BlockSpec((TOKEN_TILE, d), lambda i, s, pv, nx: (i, 0))
    return pl.pallas_call(
        functools.partial(_proj_body, alpha),
        grid_spec=pltpu.PrefetchScalarGridSpec(
            num_scalar_prefetch=3, grid=(t // TOKEN_TILE,),
            in_specs=[tile, tile,
                      pl.BlockSpec((None, None, 6, d), lambda i, s, pv, nx: (l, s[i], 0, 0)),
                      pl.BlockSpec((None, d, d), lambda i, s, pv, nx: (li, 0, 0))] + ep_in,
            out_specs=ep_out, scratch_shapes=_EPILOGUE_SCRATCH),
        out_shape=_epilogue_out_shapes(t, d),
        compiler_params=_params(1),
        name=f"attn_out_proj_{l}",
    )(meta["seg"], meta["prev"], meta["next"], x, o, mod, p["w_out_c"],
      p["ln_g"], p["ln_b"], p["w_router"], p["b_router"])


def _route_tables(te, counts, n_experts, n_tiles):
    t = te.shape[0]
    counts = counts[0, :n_experts]
    padded = (counts + MOE_TILE - 1) // MOE_TILE * MOE_TILE
    pad_end = jnp.cumsum(padded).astype(jnp.int32)
    pad_start = pad_end - padded
    experts, ranks = te[:, :TOP_K], te[:, TOP_K:2 * TOP_K]
    sel = experts[:, :, None] == jnp.arange(n_experts, dtype=jnp.int32)
    pos = ranks + jnp.sum(jnp.where(sel, pad_start, 0), axis=-1)
    n_used = (pad_end[-1] // MOE_TILE).astype(jnp.int32)
    tile_start = jnp.arange(n_tiles, dtype=jnp.int32) * MOE_TILE
    tile_e = jnp.sum(tile_start[:, None] >= pad_end[None, :], axis=1)
    tile_e = jnp.minimum(tile_e, n_experts - 1).astype(jnp.int32)
    last_e = jnp.sum(jnp.where(jnp.arange(n_tiles) == n_used - 1, tile_e, 0))
    tile_e = jnp.where(jnp.arange(n_tiles) < n_used, tile_e, last_e)
    nonempty = padded > 0
    eid = jnp.arange(n_experts, dtype=jnp.int32)
    later = (eid[None, :] > eid[:, None]) & nonempty[None, :]
    next_e = jnp.where(later.any(axis=1), jnp.argmax(later, axis=1).astype(jnp.int32), eid)
    parity = ((jnp.cumsum(nonempty.astype(jnp.int32)) - 1) % 2).astype(jnp.int32)
    pos = pos.astype(jnp.int32).reshape(t // TOKEN_TILE, 1, TOKEN_TILE * TOP_K)
    return (tile_e, n_used.reshape(1), next_e[tile_e], parity[tile_e]), pad_end, pos


def _dispatch_body(pend_ref, pos_ref, h2_ref, xs_hbm, stage, sem, zsem):
    i = pl.program_id(0)
    n_steps = pl.num_programs(0)
    slot = i % 2
    tt = h2_ref.shape[0]
    n_e = pend_ref.shape[0]

    def scatter_wait(s):
        for _ in range(TOP_K):
            pltpu.make_async_copy(stage.at[s], xs_hbm.at[pl.ds(0, tt)], sem.at[s]).wait()

    @pl.when(i == 0)
    def _():
        stage[0] = jnp.zeros(stage.shape[1:], stage.dtype)
        n_tiles = xs_hbm.shape[0] // tt
        n_used = pend_ref[n_e - 1] // tt
        for phase in range(2):
            for e in range(n_e):
                lo = pend_ref[e - 1] if e else 0
                for cond, row0 in ((pend_ref[e] > lo, pend_ref[e] - tt),
                                   (n_used + e < n_tiles, (n_used + e) * tt)):
                    @pl.when(cond)
                    def _():
                        cp = pltpu.make_async_copy(
                            stage.at[0], xs_hbm.at[pl.ds(pl.multiple_of(row0, tt), tt)], zsem)
                        if phase == 0:
                            cp.start()
                        else:
                            cp.wait()

    @pl.when(i >= 2)
    def _():
        scatter_wait(slot)

    stage[slot] = h2_ref[...]

    def body(r, c):
        for k in range(TOP_K):
            pltpu.make_async_copy(stage.at[slot, pl.ds(r, 1)],
                                  xs_hbm.at[pl.ds(pos_ref[0, r * TOP_K + k], 1)],
                                  sem.at[slot]).start(priority=k % 2)
        return c
    lax.fori_loop(0, tt, body, 0, unroll=4)

    @pl.when(i == n_steps - 1)
    def _():
        @pl.when(n_steps >= 2)
        def _():
            scatter_wait(1 - slot)
        scatter_wait(slot)


def _moe_dispatch(l, h2, pad_end, pos, n_tiles):
    t, d = h2.shape
    assert TOKEN_TILE == MOE_TILE
    return pl.pallas_call(
        _dispatch_body,
        grid_spec=pltpu.PrefetchScalarGridSpec(
            num_scalar_prefetch=1, grid=(t // TOKEN_TILE,),
            in_specs=[pl.BlockSpec((None, 1, TOKEN_TILE * TOP_K), lambda i, pe: (i, 0, 0),
                                   memory_space=pltpu.SMEM),
                      pl.BlockSpec((TOKEN_TILE, d), lambda i, pe: (i, 0))],
            out_specs=pl.BlockSpec(memory_space=pl.ANY),
            scratch_shapes=[pltpu.VMEM((2, TOKEN_TILE, d), F32),
                            pltpu.SemaphoreType.DMA((2,)),
                            pltpu.SemaphoreType.DMA]),
        out_shape=jax.ShapeDtypeStruct((n_tiles * MOE_TILE, d), F32),
        compiler_params=_params(1),
        name=f"moe_dispatch_{l}",
    )(pad_end, pos, h2)


def _moe_body(l, te_ref, nu_ref, nxe_ref, par_ref, x_ref, bgu_ref, bd_ref, wgu_hbm, wd_hbm, y_ref,
              wgu_st, wd_st, wgu_bf, wd_bf, wsem):
    i = pl.program_id(0)
    de = wd_bf.shape[0]
    e = te_ref[i]
    par = par_ref[i]

    def weight_copies(ex, p):
        return (pltpu.make_async_copy(wgu_hbm.at[l, ex], wgu_st.at[p], wsem.at[p, 0]),
                pltpu.make_async_copy(wd_hbm.at[l, ex], wd_st.at[p], wsem.at[p, 1]))

    @pl.when(i == 0)
    def _():
        for cp in weight_copies(e, 0):
            cp.start()

    @pl.when(i < nu_ref[0])
    def _():
        @pl.when((i == 0) | (e != te_ref[jnp.maximum(i - 1, 0)]))
        def _():
            for cp in weight_copies(e, par):
                cp.wait()
            wgu_bf[...] = wgu_st[par].astype(BF16)
            wd_bf[...] = wd_st[par].astype(BF16)

            @pl.when(nxe_ref[i] != e)
            def _():
                for cp in weight_copies(nxe_ref[i], 1 - par):
                    cp.start()

        hgu = _dot(x_ref[...].astype(BF16), wgu_bf[...]) + bgu_ref[...]
        x_glu = jnp.minimum(hgu[:, :de], SWIGLU_LIMIT)
        x_lin = jnp.clip(hgu[:, de:], -SWIGLU_LIMIT, SWIGLU_LIMIT)
        act = x_glu * _sigmoid(SWIGLU_ALPHA * x_glu) * (x_lin + 1.0)
        y_ref[...] = _dot(act.astype(BF16), wd_bf[...]) + bd_ref[...]

    @pl.when(i >= nu_ref[0])
    def _():
        y_ref[...] = jnp.zeros(y_ref.shape, y_ref.dtype)


def _moe_experts(l, xs, tile_tables, w_gu, b_gu, w_d, b_d):
    n_rows, d = xs.shape
    tile_e, n_used, next_e, parity = tile_tables
    n_tiles = tile_e.shape[0]
    n_e, de = w_d.shape[1], w_d.shape[2]
    return pl.pallas_call(
        functools.partial(_moe_body, l),
        grid_spec=pltpu.PrefetchScalarGridSpec(
            num_scalar_prefetch=4, grid=(n_tiles,),
            in_specs=[pl.BlockSpec((MOE_TILE, d), lambda i, te, nu, *_: (jnp.minimum(i, nu[0] - 1), 0)),
                      pl.BlockSpec((None, None, 1, 2 * de), lambda i, te, *_: (l, te[i], 0, 0)),
                      pl.BlockSpec((None, None, 1, d), lambda i, te, *_: (l, te[i], 0, 0)),
                      pl.BlockSpec(memory_space=pl.ANY),
                      pl.BlockSpec(memory_space=pl.ANY)],
            out_specs=pl.BlockSpec((MOE_TILE, d), lambda i, *_: (i, 0)),
            scratch_shapes=[pltpu.VMEM((2, d, 2 * de), F32), pltpu.VMEM((2, de, d), F32),
                            pltpu.VMEM((d, 2 * de), BF16), pltpu.VMEM((de, d), BF16),
                            pltpu.SemaphoreType.DMA((2, 2))]),
        out_shape=jax.ShapeDtypeStruct((n_rows, d), F32),
        compiler_params=_params(1),
        name=f"moe_experts_{l}",
    )(tile_e, n_used, next_e, parity, xs, b_gu.reshape(b_gu.shape[0], n_e, 1, 2 * de),
      b_d.reshape(b_d.shape[0], n_e, 1, d), w_gu, w_d)


def _combine_body(alpha, seg_ref, pos_ref, posn_ref, x1_ref, g_ref, mod_ref, lng_ref, lnb_ref, yb_hbm,
                  o_ref, buf, sem):
    i = pl.program_id(0)
    n_steps = pl.num_programs(0)
    slot = i % 2
    tt = x1_ref.shape[0]

    def gather_start(idx_ref, s):
        def body(r, c):
            for k in range(TOP_K):
                pltpu.make_async_copy(yb_hbm.at[pl.ds(idx_ref[0, r * TOP_K + k], 1)],
                                      buf.at[s, k, pl.ds(r, 1)], sem.at[s]).start(priority=k % 2)
            return c
        lax.fori_loop(0, tt, body, 0, unroll=4)

    @pl.when(i == 0)
    def _():
        gather_start(pos_ref, 0)

    @pl.when(i + 1 < n_steps)
    def _():
        gather_start(posn_ref, 1 - slot)

    for k in range(TOP_K):
        pltpu.make_async_copy(yb_hbm.at[pl.ds(0, tt)], buf.at[slot, k], sem.at[slot]).wait()
    g = g_ref[...]
    y = g[:, 0:1] * buf[slot, 0]
    for k in range(1, TOP_K):
        y = y + g[:, k:k + 1] * buf[slot, k]
    o_ref[...] = _ln(alpha * x1_ref[...] + mod_ref[5:6, :] * y, lng_ref[...], lnb_ref[...])


def _moe_combine(l, alpha, meta, x1, yb, pos, gates, mod, ln_g, ln_b):
    t, d = x1.shape
    nt = t // TOKEN_TILE
    tile = pl.BlockSpec((TOKEN_TILE, d), lambda i, s: (i, 0))
    pos_blk = lambda f: pl.BlockSpec((None, 1, TOKEN_TILE * TOP_K), f, memory_space=pltpu.SMEM)
    return pl.pallas_call(
        functools.partial(_combine_body, alpha),
        grid_spec=pltpu.PrefetchScalarGridSpec(
            num_scalar_prefetch=1, grid=(nt,),
            in_specs=[pos_blk(lambda i, s: (i, 0, 0)),
                      pos_blk(lambda i, s: (jnp.minimum(i + 1, nt - 1), 0, 0)),
                      tile,
                      pl.BlockSpec((TOKEN_TILE, LANES), lambda i, s: (i, 0)),
                      pl.BlockSpec((None, None, 6, d), lambda i, s: (l, s[i], 0, 0)),
                      pl.BlockSpec((None, None, 1, d), lambda i, s: (l, 1, 0, 0)),
                      pl.BlockSpec((None, None, 1, d), lambda i, s: (l, 1, 0, 0)),
                      pl.BlockSpec(memory_space=pl.ANY)],
            out_specs=tile,
            scratch_shapes=[pltpu.VMEM((2, TOP_K, TOKEN_TILE, d), F32),
                            pltpu.SemaphoreType.DMA((2,))]),
        out_shape=jax.ShapeDtypeStruct((t, d), F32),
        compiler_params=_params(1),
        name=f"moe_combine_{l}",
    )(meta["seg"], pos, pos, x1, gates, mod, ln_g, ln_b, yb)


def _route_tables_inv(te, counts, n_experts, n_tiles):
    t = te.shape[0]
    counts = counts[0, :n_experts]
    padded = (counts + MOE_TILE - 1) // MOE_TILE * MOE_TILE
    pad_end = jnp.cumsum(padded).astype(jnp.int32)
    pad_start = pad_end - padded
    experts, ranks = te[:, :TOP_K], te[:, TOP_K:2 * TOP_K]
    sel = experts[:, :, None] == jnp.arange(n_experts, dtype=jnp.int32)
    pos = ranks + jnp.sum(jnp.where(sel, pad_start, 0), axis=-1)
    n_rows = n_tiles * MOE_TILE
    assign = jnp.arange(t * TOP_K, dtype=jnp.int32)
    rows_assign = jnp.full((n_rows,), -1, jnp.int32).at[pos.reshape(-1)].set(
        assign, unique_indices=True)
    valid = rows_assign >= 0
    row = jnp.arange(n_rows, dtype=jnp.int32)
    dump = t * TOP_K + ((row // MOE_TILE) % 2) * MOE_TILE + row % MOE_TILE
    rows_src = jnp.where(valid, rows_assign // TOP_K, 0).reshape(n_tiles, 1, MOE_TILE)
    rows_dst = jnp.where(valid, (rows_assign % TOP_K) * t + rows_assign // TOP_K, dump)
    rows_dst = rows_dst.reshape(n_tiles, 1, MOE_TILE)
    first_dump = (t * TOP_K + MOE_TILE + jnp.arange(MOE_TILE, dtype=jnp.int32)).reshape(1, 1, MOE_TILE)
    dst_prev = jnp.concatenate([first_dump, rows_dst[:-1]], axis=0)
    n_used = (pad_end[-1] // MOE_TILE).astype(jnp.int32)
    tile_ids = jnp.arange(n_tiles, dtype=jnp.int32)
    tile_e = jnp.sum(tile_ids[:, None] * MOE_TILE >= pad_end[None, :], axis=1)
    tile_e = jnp.minimum(tile_e, n_experts - 1).astype(jnp.int32)
    last_e = jnp.sum(jnp.where(tile_ids == n_used - 1, tile_e, 0))
    tile_e = jnp.where(tile_ids < n_used, tile_e, last_e)
    nonempty = padded > 0
    eid = jnp.arange(n_experts, dtype=jnp.int32)
    later = (eid[None, :] > eid[:, None]) & nonempty[None, :]
    next_e = jnp.where(later.any(axis=1), jnp.argmax(later, axis=1).astype(jnp.int32), eid)
    group_idx = jnp.cumsum(nonempty.astype(jnp.int32)) - 1
    return (tile_e, n_used.reshape(1), next_e[tile_e], (group_idx[tile_e] % 2).astype(jnp.int32),
            rows_src, rows_dst, dst_prev)


N_CHUNK = 4


def _moe_fused_body(l, te_ref, nu_ref, nxe_ref, par_ref, srcc_ref, srcn_ref, dstp_ref, dstc_ref,
                    h2_hbm, wgu_hbm, wd_hbm, bgu_ref, bd_ref, y4_hbm,
                    xbuf, ybuf, wgu_st, wd_st, wgu_bf, wd_bf, gsem, ssem, wsem):
    i = pl.program_id(0)
    nu = nu_ref[0]
    slot = i % 2
    tm = xbuf.shape[1]
    de = wd_bf.shape[0]
    e = te_ref[i]
    par = par_ref[i]

    def weight_copies(ex, p):
        return (pltpu.make_async_copy(wgu_hbm.at[l, ex], wgu_st.at[p], wsem.at[p, 0]),
                pltpu.make_async_copy(wd_hbm.at[l, ex], wd_st.at[p], wsem.at[p, 1]))

    def gather_copy(idx_ref, r, s):
        return pltpu.make_async_copy(h2_hbm.at[pl.ds(idx_ref[0, r], 1)], xbuf.at[s, pl.ds(r, 1)],
                                     gsem.at[s])

    def scatter_copy(idx_ref, r, s):
        return pltpu.make_async_copy(ybuf.at[s, pl.ds(r, 1)], y4_hbm.at[pl.ds(idx_ref[0, r], 1)],
                                     ssem.at[s])

    def gather_wait(s):
        pltpu.make_async_copy(h2_hbm.at[pl.ds(0, tm)], xbuf.at[s], gsem.at[s]).wait()

    def scatter_wait(s):
        pltpu.make_async_copy(ybuf.at[s], y4_hbm.at[pl.ds(0, tm)], ssem.at[s]).wait()

    @pl.when(i == 0)
    def _():
        for cp in weight_copies(e, 0):
            cp.start()

        def body(r, c):
            gather_copy(srcc_ref, r, 0).start()
            return c
        lax.fori_loop(0, tm, body, 0, unroll=8)
        n_real = y4_hbm.shape[0] - 2 * tm
        ybuf[...] = jnp.zeros(ybuf.shape, ybuf.dtype)
        for s in range(2):
            cp = pltpu.make_async_copy(ybuf.at[s], y4_hbm.at[pl.ds(n_real + s * tm, tm)], ssem.at[s])
            cp.start()
            cp.wait()

    @pl.when(i < nu)
    def _():
        @pl.when((i == 0) | (e != te_ref[jnp.maximum(i - 1, 0)]))
        def _():
            for cp in weight_copies(e, par):
                cp.wait()
            wgu_bf[...] = wgu_st[par].astype(BF16)
            wd_bf[...] = wd_st[par].astype(BF16)

            @pl.when(nxe_ref[i] != e)
            def _():
                for cp in weight_copies(nxe_ref[i], 1 - par):
                    cp.start()

        gather_wait(slot)
        x = xbuf[slot].astype(BF16)
        rows_per = tm // N_CHUNK
        cw = de // N_CHUNK
        y = None
        for c in range(N_CHUNK):
            for r in range(c * rows_per, (c + 1) * rows_per):
                gather_copy(srcn_ref, r, 1 - slot).start()
                scatter_copy(dstp_ref, r, 1 - slot).start()
            glu = _dot(x, wgu_bf[:, c * cw:(c + 1) * cw]) + bgu_ref[:, c * cw:(c + 1) * cw]
            lin = _dot(x, wgu_bf[:, de + c * cw:de + (c + 1) * cw]) + bgu_ref[:, de + c * cw:de + (c + 1) * cw]
            glu = jnp.minimum(glu, SWIGLU_LIMIT)
            lin = jnp.clip(lin, -SWIGLU_LIMIT, SWIGLU_LIMIT)
            act = glu * _sigmoid(SWIGLU_ALPHA * glu) * (lin + 1.0)
            part = _dot(act.astype(BF16), wd_bf[c * cw:(c + 1) * cw, :])
            y = part if y is None else y + part
        y = y + bd_ref[...]

        @pl.when(i >= 1)
        def _():
            scatter_wait(slot)

        ybuf[slot] = y

        @pl.when(i == nu - 1)
        def _():
            def body(r, c):
                scatter_copy(dstc_ref, r, slot).start()
                return c
            lax.fori_loop(0, tm, body, 0, unroll=8)
            scatter_wait(1 - slot)
            scatter_wait(slot)
            gather_wait(1 - slot)


def _moe_fused(l, h2, tables, w_gu, b_gu, w_d, b_d):
    t, d = h2.shape
    tile_e, n_used, next_e, parity, rows_src, rows_dst, dst_prev = tables
    n_tiles = tile_e.shape[0]
    n_e, de = w_d.shape[1], w_d.shape[2]
    smem_blk = lambda f: pl.BlockSpec((None, 1, MOE_TILE), f, memory_space=pltpu.SMEM)
    cur = lambda i, *_: (i, 0, 0)
    nxt = lambda i, *_: (jnp.minimum(i + 1, n_tiles - 1), 0, 0)
    return pl.pallas_call(
        functools.partial(_moe_fused_body, l),
        grid_spec=pltpu.PrefetchScalarGridSpec(
            num_scalar_prefetch=4, grid=(n_tiles,),
            in_specs=[smem_blk(cur), smem_blk(nxt), smem_blk(cur), smem_blk(cur),
                      pl.BlockSpec(memory_space=pl.ANY),
                      pl.BlockSpec(memory_space=pl.ANY),
                      pl.BlockSpec(memory_space=pl.ANY),
                      pl.BlockSpec((None, None, 1, 2 * de), lambda i, te, *_: (l, te[i], 0, 0)),
                      pl.BlockSpec((None, None, 1, d), lambda i, te, *_: (l, te[i], 0, 0))],
            out_specs=pl.BlockSpec(memory_space=pl.ANY),
            scratch_shapes=[pltpu.VMEM((2, MOE_TILE, d), F32),
                            pltpu.VMEM((2, MOE_TILE, d), F32),
                            pltpu.VMEM((2, d, 2 * de), F32),
                            pltpu.VMEM((2, de, d), F32),
                            pltpu.VMEM((d, 2 * de), BF16),
                            pltpu.VMEM((de, d), BF16),
                            pltpu.SemaphoreType.DMA((2,)),
                            pltpu.SemaphoreType.DMA((2,)),
                            pltpu.SemaphoreType.DMA((2, 2))]),
        out_shape=jax.ShapeDtypeStruct((t * TOP_K + 2 * MOE_TILE, d), F32),
        compiler_params=_params(1),
        name=f"moe_experts_{l}",
    )(tile_e, n_used, next_e, parity, rows_src, rows_src, dst_prev, rows_dst, h2, w_gu, w_d,
      b_gu.reshape(b_gu.shape[0], n_e, 1, 2 * de), b_d.reshape(b_d.shape[0], n_e, 1, d))


def _combine_dense_body(alpha, seg_ref, x1_ref, y0_ref, y1_ref, y2_ref, y3_ref, g_ref, mod_ref,
                        lng_ref, lnb_ref, o_ref):
    g = g_ref[...]
    y = g[:, 0:1] * y0_ref[...]
    for k, y_ref in ((1, y1_ref), (2, y2_ref), (3, y3_ref)):
        y = y + g[:, k:k + 1] * y_ref[...]
    o_ref[...] = _ln(alpha * x1_ref[...] + mod_ref[5:6, :] * y, lng_ref[...], lnb_ref[...])


def _moe_combine_dense(l, alpha, meta, x1, y4, gates, mod, ln_g, ln_b):
    t, d = x1.shape
    nt = t // TOKEN_TILE
    tile = pl.BlockSpec((TOKEN_TILE, d), lambda i, s: (i, 0))
    y_spec = lambda k: pl.BlockSpec((TOKEN_TILE, d), lambda i, s: (k * nt + i, 0))
    return pl.pallas_call(
        functools.partial(_combine_dense_body, alpha),
        grid_spec=pltpu.PrefetchScalarGridSpec(
            num_scalar_prefetch=1, grid=(nt,),
            in_specs=[tile, y_spec(0), y_spec(1), y_spec(2), y_spec(3),
                      pl.BlockSpec((TOKEN_TILE, LANES), lambda i, s: (i, 0)),
                      pl.BlockSpec((None, None, 6, d), lambda i, s: (l, s[i], 0, 0)),
                      pl.BlockSpec((None, None, 1, d), lambda i, s: (l, 1, 0, 0)),
                      pl.BlockSpec((None, None, 1, d), lambda i, s: (l, 1, 0, 0))],
            out_specs=tile),
        out_shape=jax.ShapeDtypeStruct((t, d), F32),
        compiler_params=_params(1),
        name=f"moe_combine_{l}",
    )(meta["seg"], x1, y4, y4, y4, y4, gates, mod, ln_g, ln_b)


def _token_meta(n_ctx_seq, seq, n_lat, lat_seq):
    seg, prev, nxt = [], [], []
    for n_seq, length, seg_of in ((n_ctx_seq, seq, lambda b: 0), (n_lat, lat_seq, lambda b: 1 + b)):
        per = length // TOKEN_TILE
        for b in range(n_seq):
            for j in range(per):
                seg.append(seg_of(b))
                prev.append(int(j > 0))
                nxt.append(int(j < per - 1))
    as_i32 = lambda a: jnp.asarray(np.asarray(a, np.int32))
    return {"seg": as_i32(seg), "prev": as_i32(prev), "next": as_i32(nxt)}


def kernel(x_prompt, x_sample, c, cache_k, cache_v, c_ctx, w_mod, b_mod, ln_g, ln_b, w_in_ab, sgu_ln_g, sgu_ln_b, w_spatial, b_spatial, conv_w, conv_b, conv_ln_g, conv_ln_b, w_out_ab, w_qkv, rpb, w_out_c, w_router, b_router, w_gate_up, b_gate_up, w_down, b_down):
    n_ctx_seq, seq, d = x_prompt.shape
    n_lat, lat_seq, _ = x_sample.shape
    depth = w_mod.shape[0]
    n_heads, head_dim = cache_k.shape[3], cache_k.shape[4]
    n_experts = w_router.shape[-1]
    ca = sgu_ln_g.shape[-1]
    n_even, n_odd = w_in_ab.shape[0], w_qkv.shape[0]
    t_ctx, t_lat = n_ctx_seq * seq, n_lat * lat_seq
    t = t_ctx + t_lat
    rows_n = lat_seq // GRID_W
    assert seq % TOKEN_TILE == 0 and lat_seq % TOKEN_TILE == 0 and 1 + n_lat <= SUBLANES
    assert TOKEN_TILE % CHUNK == 0 and HALO >= CONV_K // 2 and ca == w_out_ab.shape[1] // 2
    assert rows_n % Q_ROWS == 0 and rows_n >= K_ROWS and t_ctx % (Q_ROWS * GRID_W) == 0
    assert n_heads * head_dim == d and 2 * head_dim == LANES and n_experts <= LANES
    assert (t * TOP_K) % MOE_TILE == 0
    alpha = float((2 * depth) ** 0.25)
    scale = float(head_dim ** -0.5)
    meta = _token_meta(n_ctx_seq, seq, n_lat, lat_seq)

    x = jnp.concatenate([x_prompt.reshape(t_ctx, d), x_sample.reshape(t_lat, d)], axis=0)
    cvec = jnp.zeros((SUBLANES, d), F32).at[0].set(c_ctx).at[1:1 + n_lat].set(c)
    mod = _modulation(cvec, w_mod, b_mod).reshape(depth, SUBLANES, 6, d)

    pad_e = LANES - n_experts
    common = {
        "ln_g": ln_g.reshape(depth, 2, 1, d), "ln_b": ln_b.reshape(depth, 2, 1, d),
        "w_router": jnp.pad(w_router, ((0, 0), (0, 0), (0, pad_e))).astype(BF16),
        "b_router": jnp.pad(b_router, ((0, 0), (0, pad_e)), constant_values=PAD_LOGIT).reshape(depth, 1, LANES),
    }
    even = dict(common)
    even.update({
        "w_in": w_in_ab.astype(BF16), "sgu_g": sgu_ln_g.reshape(n_even, 1, ca),
        "sgu_b": sgu_ln_b.reshape(n_even, 1, ca), "w_sp": w_spatial.astype(BF16),
        "b_sp": jnp.repeat(jnp.transpose(b_spatial, (0, 2, 1)), ca // G_A, axis=2),
        "conv_w": conv_w, "conv_b": conv_b.reshape(n_even, 1, ca),
        "cln_g": conv_ln_g.reshape(n_even, 1, ca), "cln_b": conv_ln_b.reshape(n_even, 1, ca),
        "w_out_ab": w_out_ab.astype(BF16)})
    odd = dict(common)
    odd["w_out_c"] = w_out_c.astype(BF16)
    w_qkv_bf = w_qkv.astype(BF16)
    lat_tables = _latent_window_tables(rows_n)
    n_tiles = t * TOP_K // MOE_TILE + n_experts

    new_k, new_v = [], []
    for l in range(depth):
        i = l // 2
        if l % 2 == 0:
            x1, h2, top_e, gates, counts = _even_layer(l, alpha, meta, x, mod, even)
        else:
            q, k, v, k32, v32 = _qkv_proj(l, meta, x, mod, w_qkv_bf)
            new_k.append(k32[:t_ctx].reshape(n_ctx_seq, seq, n_heads, head_dim))
            new_v.append(v32[:t_ctx].reshape(n_ctx_seq, seq, n_heads, head_dim))
            o_ctx = _attn_ctx(q, k, v, n_ctx_seq, seq, scale)
            bias = _latent_bias(rpb[i], lat_tables[2])
            o_lat = _attn_lat(q, k, v, cache_k[:, i].reshape(n_lat, -1, d),
                              cache_v[:, i].reshape(n_lat, -1, d), bias, lat_tables,
                              t_ctx, n_lat, lat_seq, scale)
            o = jnp.concatenate([o_ctx, o_lat], axis=0)
            x1, h2, top_e, gates, counts = _odd_out_proj(l, alpha, meta, x, o, mod, odd)
        tile_tables, pad_end, pos = _route_tables(top_e, counts, n_experts, n_tiles)
        xs = _moe_dispatch(l, h2, pad_end, pos, n_tiles)
        yb = _moe_experts(l, xs, tile_tables, w_gate_up, b_gate_up, w_down, b_down)
        x = _moe_combine(l, alpha, meta, x1, yb, pos, gates, mod, common["ln_g"], common["ln_b"])

    y_prompt = x[:t_ctx].reshape(n_ctx_seq, seq, d)
    y_sample = x[t_ctx:].reshape(n_lat, lat_seq, d)
    return (y_prompt, y_sample, jnp.stack(new_k, axis=1), jnp.stack(new_v, axis=1))
```

```python
import functools

import numpy as np
import jax
import jax.numpy as jnp
from jax import lax
from jax.experimental import pallas as pl
from jax.experimental.pallas import tpu as pltpu

F32 = jnp.float32
BF16 = jnp.bfloat16

GRID_W = 64
G_A = 8
CHUNK = 128
CONV_K = 31
WIN_ROWS = 8
WIN_COLS = 16
TOP_K = 4
SWIGLU_ALPHA = 1.702
SWIGLU_LIMIT = 7.0
LN_EPS = 1e-5
NEG_INF = -1e30

LANES = 128
SUBLANES = 8
VMEM_LIMIT = 56 * 1024 * 1024

TOKEN_TILE = 256
HALO = 16
MOE_TILE = 256
Q_ROWS = 4
K_ROWS = 12
PAD_LOGIT = -3e38


def _ln(x, g, b):
    mu = jnp.mean(x, axis=-1, keepdims=True)
    xc = x - mu
    var = jnp.mean(xc * xc, axis=-1, keepdims=True)
    return xc * lax.rsqrt(var + LN_EPS) * g + b


def _gelu(x):
    return 0.5 * x * (1.0 + jnp.tanh(0.7978845608028654 * (x + 0.044715 * (x * x * x))))


def _sigmoid(x):
    return jax.nn.sigmoid(x)


def _dot(a, b):
    return jnp.dot(a, b, preferred_element_type=F32)


def _dot_nt(a, b):
    return lax.dot_general(a, b, (((1,), (1,)), ((), ())), preferred_element_type=F32)


def _params(n_axes):
    return pltpu.CompilerParams(dimension_semantics=("arbitrary",) * n_axes,
                                vmem_limit_bytes=VMEM_LIMIT)


def _mod_body(c_ref, w_ref, b_ref, o_ref):
    c = c_ref[...]
    s = (c * _sigmoid(c)).astype(BF16)
    o_ref[...] = _dot(s, w_ref[...].astype(BF16)) + b_ref[...]


def _modulation(cvec, w_mod, b_mod):
    depth, d, n = w_mod.shape
    tn = n // 4
    return pl.pallas_call(
        _mod_body,
        grid=(depth, n // tn),
        in_specs=[pl.BlockSpec((SUBLANES, d), lambda l, j: (0, 0)),
                  pl.BlockSpec((None, d, tn), lambda l, j: (l, 0, j)),
                  pl.BlockSpec((None, 1, tn), lambda l, j: (l, 0, j))],
        out_specs=pl.BlockSpec((None, SUBLANES, tn), lambda l, j: (l, 0, j)),
        out_shape=jax.ShapeDtypeStruct((depth, SUBLANES, n), F32),
        compiler_params=_params(2),
        name="adaln_modulation",
    )(cvec, w_mod, b_mod.reshape(depth, 1, n))


def _post_mixer(alpha, x, y, mod_ref, lng_ref, lnb_ref, wr_ref, br_ref,
                x1_ref, h2_ref, te_ref, tg_ref, cnt_ref, cnt_scr):
    i = pl.program_id(0)
    x1 = _ln(alpha * x + mod_ref[2:3, :] * y, lng_ref[...], lnb_ref[...])
    x1_ref[...] = x1
    h2 = x1 * (1.0 + mod_ref[4:5, :]) + mod_ref[3:4, :]
    h2_ref[...] = h2
    logits = _dot(h2.astype(BF16), wr_ref[...]) + br_ref[...]
    n_e = cnt_scr.shape[0]
    lt = logits.T[:n_e]
    tt = lt.shape[1]
    eidx = lax.broadcasted_iota(jnp.int32, lt.shape, 0)
    vals, idxs = [], []
    for _ in range(TOP_K):
        m = jnp.max(lt, axis=0, keepdims=True)
        idx = jnp.min(jnp.where(lt == m, eidx, n_e), axis=0, keepdims=True)
        vals.append(m)
        idxs.append(idx)
        lt = jnp.where(eidx == idx, -jnp.inf, lt)
    exps = [jnp.exp(v - vals[0]) for v in vals]
    den = exps[0]
    for e in exps[1:]:
        den = den + e

    @pl.when(i == 0)
    def _():
        cnt_scr[...] = jnp.zeros(cnt_scr.shape, cnt_scr.dtype)

    onehot = jnp.zeros(lt.shape, F32)
    for k in range(TOP_K):
        onehot = onehot + (eidx == idxs[k]).astype(F32)
    row = lax.broadcasted_iota(jnp.int32, (tt, tt), 0)
    col = lax.broadcasted_iota(jnp.int32, (tt, tt), 1)
    before = _dot(onehot.astype(BF16), (row < col).astype(BF16)) + cnt_scr[:, 0:1]
    cnt = cnt_scr[...] + jnp.sum(onehot, axis=1, keepdims=True)
    cnt_scr[...] = cnt
    cnt_ref[...] = cnt.astype(jnp.int32)

    row_te = lax.broadcasted_iota(jnp.int32, te_ref.shape, 0)
    row_tg = lax.broadcasted_iota(jnp.int32, (LANES, tt), 0)
    te = jnp.zeros(te_ref.shape, jnp.int32)
    tg = jnp.zeros((LANES, tt), F32)
    for k in range(TOP_K):
        rank = jnp.sum(jnp.where(eidx == idxs[k], before, 0.0), axis=0, keepdims=True)
        te = jnp.where(row_te == k, idxs[k], te)
        te = jnp.where(row_te == TOP_K + k, rank.astype(jnp.int32), te)
        tg = jnp.where(row_tg == k, exps[k] / den, tg)
    te_ref[...] = te
    tg_ref[...] = tg.T


def _expert_rows(n_experts):
    return -(-n_experts // SUBLANES) * SUBLANES


def _epilogue_specs(l, d, n_experts):
    in_specs = [pl.BlockSpec((None, None, 1, d), lambda i, s, p, n: (l, 0, 0, 0)),
                pl.BlockSpec((None, None, 1, d), lambda i, s, p, n: (l, 0, 0, 0)),
                pl.BlockSpec((None, d, LANES), lambda i, s, p, n: (l, 0, 0)),
                pl.BlockSpec((None, 1, LANES), lambda i, s, p, n: (l, 0, 0))]
    out_specs = [pl.BlockSpec((TOKEN_TILE, d), lambda i, s, p, n: (i, 0)),
                 pl.BlockSpec((TOKEN_TILE, d), lambda i, s, p, n: (i, 0)),
                 pl.BlockSpec((2 * TOP_K, TOKEN_TILE), lambda i, s, p, n: (0, i)),
                 pl.BlockSpec((TOKEN_TILE, LANES), lambda i, s, p, n: (i, 0)),
                 pl.BlockSpec((_expert_rows(n_experts), LANES), lambda i, s, p, n: (0, 0))]
    return in_specs, out_specs


def _epilogue_out_shapes(t, d, n_experts):
    return [jax.ShapeDtypeStruct((t, d), F32), jax.ShapeDtypeStruct((t, d), F32),
            jax.ShapeDtypeStruct((2 * TOP_K, t), jnp.int32), jax.ShapeDtypeStruct((t, LANES), F32),
            jax.ShapeDtypeStruct((_expert_rows(n_experts), LANES), jnp.int32)]


def _epilogue_scratch(n_experts):
    return [pltpu.VMEM((_expert_rows(n_experts), LANES), F32)]


def _even_body(alpha, seg_ref, prev_ref, next_ref,
               x_ref, xp_ref, xn_ref, mod_ref, win_ref, sg_ref, sb_ref, wsp_ref, bsp_ref,
               cw_ref, cb_ref, cg_ref, cbb_ref, wout_ref, lng_ref, lnb_ref, wr_ref, br_ref,
               x1_ref, h2_ref, te_ref, tg_ref, cnt_ref, gl_scr, cnt_scr):
    i = pl.program_id(0)
    tt = x_ref.shape[0]
    ca = sg_ref.shape[-1]
    cb2 = 2 * ca
    x = x_ref[...]
    sc = 1.0 + mod_ref[1:2, :]
    sh = mod_ref[0:1, :]
    z = _dot((x * sc + sh).astype(BF16), win_ref[...])

    u = _gelu(z[:, :ca])
    v = _ln(_gelu(z[:, ca:cb2]), sg_ref[...], sb_ref[...]).astype(BF16)
    half = lax.broadcasted_iota(jnp.int32, (CHUNK, LANES), 1) < (LANES // 2)
    chunks = []
    for ck in range(tt // CHUNK):
        cols = []
        for j in range(ca // LANES):
            vblk = v[ck * CHUNK:(ck + 1) * CHUNK, j * LANES:(j + 1) * LANES]
            cols.append(jnp.where(half, _dot(wsp_ref[2 * j], vblk), _dot(wsp_ref[2 * j + 1], vblk)))
        chunks.append(jnp.concatenate(cols, axis=1) + bsp_ref[...])
    y_a = u * jnp.concatenate(chunks, axis=0)

    def glu_rows(xh_ref):
        zh = _dot((xh_ref[...] * sc + sh).astype(BF16), win_ref[:, cb2:])
        return zh[:, :ca] * _sigmoid(zh[:, ca:])

    gl_scr[0:HALO, :] = jnp.where(prev_ref[i] > 0, glu_rows(xp_ref), 0.0)
    gl_scr[HALO:HALO + tt, :] = z[:, cb2:cb2 + ca] * _sigmoid(z[:, cb2 + ca:])
    gl_scr[HALO + tt:, :] = jnp.where(next_ref[i] > 0, glu_rows(xn_ref), 0.0)
    off = HALO - CONV_K // 2
    g_ext = gl_scr[...]
    n_ext = g_ext.shape[0]
    dc = None
    for res in range(SUBLANES):
        taps = [k for k in range(CONV_K) if (off + k) % SUBLANES == res]
        if not taps:
            continue
        shifted = g_ext if res == 0 else pltpu.roll(g_ext, n_ext - res, axis=0)
        for k in taps:
            q = (off + k) // SUBLANES * SUBLANES
            term = shifted[q:q + tt, :] * cw_ref[k:k + 1, :]
            dc = term if dc is None else dc + term
    yb = _ln(dc + cb_ref[...], cg_ref[...], cbb_ref[...])
    y_b = yb * _sigmoid(yb)

    y = _dot(jnp.concatenate([y_a, y_b], axis=1).astype(BF16), wout_ref[...])
    _post_mixer(alpha, x, y, mod_ref, lng_ref, lnb_ref, wr_ref, br_ref,
                x1_ref, h2_ref, te_ref, tg_ref, cnt_ref, cnt_scr)


def _even_layer(l, alpha, meta, x, mod, p):
    t, d = x.shape
    li = l // 2
    nh = TOKEN_TILE // HALO
    n_halo = t // HALO
    ca = p["sgu_g"].shape[-1]
    ep_in, ep_out = _epilogue_specs(l, d, p["n_experts"])
    const3 = lambda i, s, pv, nx: (li, 0, 0)
    in_specs = [
        pl.BlockSpec((TOKEN_TILE, d), lambda i, s, pv, nx: (i, 0)),
        pl.BlockSpec((HALO, d), lambda i, s, pv, nx: (jnp.maximum(i * nh - 1, 0), 0)),
        pl.BlockSpec((HALO, d), lambda i, s, pv, nx: (jnp.minimum((i + 1) * nh, n_halo - 1), 0)),
        pl.BlockSpec((None, None, 6, d), lambda i, s, pv, nx: (l, s[i], 0, 0)),
        pl.BlockSpec((None, d, 4 * ca), const3),
        pl.BlockSpec((None, 1, ca), const3),
        pl.BlockSpec((None, 1, ca), const3),
        pl.BlockSpec((None, G_A, CHUNK, CHUNK), lambda i, s, pv, nx: (li, 0, 0, 0)),
        pl.BlockSpec((None, CHUNK, ca), const3),
        pl.BlockSpec((None, CONV_K, ca), const3),
        pl.BlockSpec((None, 1, ca), const3),
        pl.BlockSpec((None, 1, ca), const3),
        pl.BlockSpec((None, 1, ca), const3),
        pl.BlockSpec((None, 2 * ca, d), const3),
    ] + ep_in
    return pl.pallas_call(
        functools.partial(_even_body, alpha),
        grid_spec=pltpu.PrefetchScalarGridSpec(
            num_scalar_prefetch=3, grid=(t // TOKEN_TILE,),
            in_specs=in_specs, out_specs=ep_out,
            scratch_shapes=[pltpu.VMEM((TOKEN_TILE + 2 * HALO, ca), F32)]
            + _epilogue_scratch(p["n_experts"])),
        out_shape=_epilogue_out_shapes(t, d, p["n_experts"]),
        compiler_params=_params(1),
        name=f"even_mixer_{l}",
    )(meta["seg"], meta["prev"], meta["next"], x, x, x, mod,
      p["w_in"], p["sgu_g"], p["sgu_b"], p["w_sp"], p["b_sp"], p["conv_w"], p["conv_b"],
      p["cln_g"], p["cln_b"], p["w_out_ab"], p["ln_g"], p["ln_b"], p["w_router"], p["b_router"])


def _qkv_body(n_ctx_tiles, seg_ref, x_ref, mod_ref, w_ref, q_ref, k_ref, v_ref, k32_ref, v32_ref):
    d = x_ref.shape[1]
    h = (x_ref[...] * (1.0 + mod_ref[1:2, :]) + mod_ref[0:1, :]).astype(BF16)
    qkv = _dot(h, w_ref[...])
    q_ref[...] = qkv[:, :d].astype(BF16)
    k = qkv[:, d:2 * d]
    v = qkv[:, 2 * d:]
    k_ref[...] = k.astype(BF16)
    v_ref[...] = v.astype(BF16)

    @pl.when(pl.program_id(0) < n_ctx_tiles)
    def _():
        k32_ref[...] = k
        v32_ref[...] = v


def _qkv_proj(l, meta, x, mod, w_qkv, t_ctx):
    t, d = x.shape
    li = l // 2
    n_ctx_tiles = t_ctx // TOKEN_TILE
    tile = pl.BlockSpec((TOKEN_TILE, d), lambda i, s: (i, 0))
    ctx_tile = pl.BlockSpec((TOKEN_TILE, d), lambda i, s: (jnp.minimum(i, n_ctx_tiles - 1), 0))
    return pl.pallas_call(
        functools.partial(_qkv_body, n_ctx_tiles),
        grid_spec=pltpu.PrefetchScalarGridSpec(
            num_scalar_prefetch=1, grid=(t // TOKEN_TILE,),
            in_specs=[tile,
                      pl.BlockSpec((None, None, 6, d), lambda i, s: (l, s[i], 0, 0)),
                      pl.BlockSpec((None, d, 3 * d), lambda i, s: (li, 0, 0))],
            out_specs=[tile] * 3 + [ctx_tile] * 2),
        out_shape=[jax.ShapeDtypeStruct((t, d), BF16)] * 3 + [jax.ShapeDtypeStruct((t_ctx, d), F32)] * 2,
        compiler_params=_params(1),
        name=f"qkv_proj_{l}",
    )(meta["seg"], x, mod, w_qkv)


def _head_pair_attention(q2, k_parts, v_parts, bias_parts, scale):
    lane = lax.broadcasted_iota(jnp.int32, q2.shape, 1)
    outs = []
    for hh in range(2):
        qm = jnp.where((lane >= hh * (LANES // 2)) & (lane < (hh + 1) * (LANES // 2)), q2 * scale,
                       jnp.zeros_like(q2))
        ss = []
        for j, kp in enumerate(k_parts):
            s = _dot_nt(qm, kp)
            if bias_parts[hh][j] is not None:
                s = s + bias_parts[hh][j]
            ss.append(s)
        m = ss[0].max(axis=-1, keepdims=True)
        for s in ss[1:]:
            m = jnp.maximum(m, s.max(axis=-1, keepdims=True))
        den = None
        o = None
        for s, vp in zip(ss, v_parts):
            e = jnp.exp(s - m)
            es = e.sum(axis=-1, keepdims=True)
            den = es if den is None else den + es
            pv = _dot(e.astype(BF16), vp)
            o = pv if o is None else o + pv
        outs.append(o / den)
    return jnp.where(lane < LANES // 2, outs[0], outs[1])


def _attn_ctx_body(scale, q_ref, k_ref, v_ref, o_ref):
    d = q_ref.shape[1]
    for pr in range(d // LANES):
        sl = slice(pr * LANES, (pr + 1) * LANES)
        o = _head_pair_attention(q_ref[:, sl], [k_ref[:, sl]], [v_ref[:, sl]],
                                 [[None], [None]], scale)
        o_ref[:, sl] = o.astype(o_ref.dtype)


def _attn_ctx(q, k, v, n_seq, seq, scale):
    d = q.shape[1]
    blk = pl.BlockSpec((seq, d), lambda b: (b, 0))
    return pl.pallas_call(
        functools.partial(_attn_ctx_body, scale),
        grid=(n_seq,),
        in_specs=[blk, blk, blk],
        out_specs=blk,
        out_shape=jax.ShapeDtypeStruct((n_seq * seq, d), BF16),
        compiler_params=_params(1),
        name="attn_ctx",
    )(q, k, v)


def _attn_lat_body(scale, cls_ref, kb_ref, q_ref, k0_ref, k1_ref, k2_ref, v0_ref, v1_ref, v2_ref,
                   ck_ref, cv_ref, bias_ref, o_ref):
    d = q_ref.shape[1]
    for pr in range(d // LANES):
        sl = slice(pr * LANES, (pr + 1) * LANES)
        k_loc = jnp.concatenate([k0_ref[:, sl], k1_ref[:, sl], k2_ref[:, sl]], axis=0)
        v_loc = jnp.concatenate([v0_ref[:, sl], v1_ref[:, sl], v2_ref[:, sl]], axis=0)
        ck = ck_ref[:, sl].astype(BF16)
        cv = cv_ref[:, sl].astype(BF16)
        o = _head_pair_attention(q_ref[:, sl], [k_loc, ck], [v_loc, cv],
                                 [[bias_ref[2 * pr], None], [bias_ref[2 * pr + 1], None]], scale)
        o_ref[:, sl] = o.astype(o_ref.dtype)


def _latent_window_tables(rows_n):
    wr = min(WIN_ROWS, rows_n)
    n_rt = rows_n // Q_ROWS
    kstart = np.clip(np.arange(n_rt) * Q_ROWS - wr // 2, 0, rows_n - K_ROWS)
    kstart = (kstart // Q_ROWS) * Q_ROWS
    patterns, cls = [], []
    for rt in range(n_rt):
        pat = np.full((Q_ROWS, K_ROWS), -1, np.int64)
        for qi in range(Q_ROWS):
            r = rt * Q_ROWS + qi
            rs = int(np.clip(r - wr // 2, 0, rows_n - wr))
            for kj in range(K_ROWS):
                kr = int(kstart[rt]) + kj
                if rs <= kr < rs + wr:
                    pat[qi, kj] = kr - r + WIN_ROWS - 1
        assert (pat >= 0).sum(axis=1).min() == wr, "key block does not cover the window"
        key = pat.tobytes()
        if key not in [p.tobytes() for p in patterns]:
            patterns.append(pat)
        cls.append([p.tobytes() for p in patterns].index(key))
    return (kstart // Q_ROWS).astype(np.int32), np.asarray(cls, np.int32), np.stack(patterns)


def _latent_bias(rpb, patterns):
    h = rpb.shape[0]
    qc = np.arange(GRID_W)[:, None]
    kc = np.arange(GRID_W)[None, :]
    qcs = np.clip(qc - WIN_COLS // 2, 0, GRID_W - WIN_COLS)
    col_ok = (kc >= qcs) & (kc < qcs + WIN_COLS)
    dc = np.clip(kc - qc + WIN_COLS - 1, 0, 2 * WIN_COLS - 2)
    onehot = (dc[None] == np.arange(2 * WIN_COLS - 1)[:, None, None]) & col_ok[None]
    cm = jnp.einsum("hrd,dqk->hrqk", rpb, jnp.asarray(onehot, F32), precision=lax.Precision.HIGHEST)
    cm = jnp.where(jnp.asarray(col_ok), cm, NEG_INF)
    cx = jnp.concatenate([cm, jnp.full((h, 1, GRID_W, GRID_W), NEG_INF, F32)], axis=1)
    idx = np.where(patterns >= 0, patterns, 2 * WIN_ROWS - 1)
    classes = []
    for pat in idx:
        q_rows = [jnp.concatenate([cx[:, int(dr)] for dr in pat_q], axis=-1) for pat_q in pat]
        classes.append(jnp.concatenate(q_rows, axis=-2))
    return jnp.stack(classes, axis=0)


def _attn_lat(q, k, v, ck, cv, bias, tables, tok0, n_batch, n_tok, scale):
    d = q.shape[1]
    kblk, cls, _ = tables
    n_rt = kblk.shape[0]
    qt = Q_ROWS * GRID_W
    base = tok0 // qt
    per_b = n_tok // qt
    h = bias.shape[1]
    lc = ck.shape[1]

    def kv_spec(j):
        return pl.BlockSpec((qt, d), lambda b, r, c, kb: (base + b * per_b + kb[r] + j, 0))

    return pl.pallas_call(
        functools.partial(_attn_lat_body, scale),
        grid_spec=pltpu.PrefetchScalarGridSpec(
            num_scalar_prefetch=2, grid=(n_batch, n_rt),
            in_specs=[pl.BlockSpec((qt, d), lambda b, r, c, kb: (base + b * per_b + r, 0)),
                      kv_spec(0), kv_spec(1), kv_spec(2), kv_spec(0), kv_spec(1), kv_spec(2),
                      pl.BlockSpec((None, lc, d), lambda b, r, c, kb: (b, 0, 0)),
                      pl.BlockSpec((None, lc, d), lambda b, r, c, kb: (b, 0, 0)),
                      pl.BlockSpec((None, h, qt, K_ROWS * GRID_W), lambda b, r, c, kb: (c[r], 0, 0, 0))],
            out_specs=pl.BlockSpec((qt, d), lambda b, r, c, kb: (b * per_b + r, 0))),
        out_shape=jax.ShapeDtypeStruct((n_batch * n_tok, d), BF16),
        compiler_params=_params(2),
        name="attn_latent",
    )(jnp.asarray(cls), jnp.asarray(kblk), q, k, k, k, v, v, v, ck, cv, bias)


def _proj_body(alpha, n_ctx_tiles, seg_ref, prev_ref, next_ref, x_ref, oc_ref, ol_ref, mod_ref, w_ref,
               lng_ref, lnb_ref, wr_ref, br_ref, x1_ref, h2_ref, te_ref, tg_ref, cnt_ref, cnt_scr):
    o = jnp.where(pl.program_id(0) < n_ctx_tiles, oc_ref[...], ol_ref[...])
    y = _dot(o, w_ref[...])
    _post_mixer(alpha, x_ref[...], y, mod_ref, lng_ref, lnb_ref, wr_ref, br_ref,
                x1_ref, h2_ref, te_ref, tg_ref, cnt_ref, cnt_scr)


def _odd_out_proj(l, alpha, meta, x, o_ctx, o_lat, mod, p):
    t, d = x.shape
    li = l // 2
    nc = o_ctx.shape[0] // TOKEN_TILE
    ep_in, ep_out = _epilogue_specs(l, d, p["n_experts"])
    tile = pl.BlockSpec((TOKEN_TILE, d), lambda i, s, pv, nx: (i, 0))
    return pl.pallas_call(
        functools.partial(_proj_body, alpha, nc),
        grid_spec=pltpu.PrefetchScalarGridSpec(
            num_scalar_prefetch=3, grid=(t // TOKEN_TILE,),
            in_specs=[tile,
                      pl.BlockSpec((TOKEN_TILE, d), lambda i, s, pv, nx: (jnp.minimum(i, nc - 1), 0)),
                      pl.BlockSpec((TOKEN_TILE, d), lambda i, s, pv, nx: (jnp.maximum(i - nc, 0), 0)),
                      pl.BlockSpec((None, None, 6, d), lambda i, s, pv, nx: (l, s[i], 0, 0)),
                      pl.BlockSpec((None, d, d), lambda i, s, pv, nx: (li, 0, 0))] + ep_in,
            out_specs=ep_out, scratch_shapes=_epilogue_scratch(p["n_experts"])),
        out_shape=_epilogue_out_shapes(t, d, p["n_experts"]),
        compiler_params=_params(1),
        name=f"attn_out_proj_{l}",
    )(meta["seg"], meta["prev"], meta["next"], x, o_ctx, o_lat, mod, p["w_out_c"],
      p["ln_g"], p["ln_b"], p["w_router"], p["b_router"])


def _route_tables(te, counts, n_experts, n_tiles):
    t = te.shape[1]
    counts = counts[:n_experts, 0]
    padded = (counts + MOE_TILE - 1) // MOE_TILE * MOE_TILE
    pad_end = jnp.cumsum(padded).astype(jnp.int32)
    pad_start = pad_end - padded
    experts, ranks = te[:TOP_K], te[TOP_K:]
    sel = experts[:, :, None] == jnp.arange(n_experts, dtype=jnp.int32)
    pos = ranks + jnp.sum(jnp.where(sel, pad_start, 0), axis=-1)
    pos = jnp.transpose(pos.reshape(TOP_K, t // TOKEN_TILE, TOKEN_TILE), (1, 2, 0))
    n_used = (pad_end[-1] // MOE_TILE).astype(jnp.int32)
    tile_start = jnp.arange(n_tiles, dtype=jnp.int32) * MOE_TILE
    tile_e = jnp.sum(tile_start[:, None] >= pad_end[None, :], axis=1)
    tile_e = jnp.minimum(tile_e, n_experts - 1).astype(jnp.int32)
    last_e = jnp.sum(jnp.where(jnp.arange(n_tiles) == n_used - 1, tile_e, 0))
    tile_e = jnp.where(jnp.arange(n_tiles) < n_used, tile_e, last_e)
    nonempty = padded > 0
    eid = jnp.arange(n_experts, dtype=jnp.int32)
    later = (eid[None, :] > eid[:, None]) & nonempty[None, :]
    next_e = jnp.where(later.any(axis=1), jnp.argmax(later, axis=1).astype(jnp.int32), eid)
    parity = ((jnp.cumsum(nonempty.astype(jnp.int32)) - 1) % 2).astype(jnp.int32)
    pos = pos.astype(jnp.int32).reshape(t // TOKEN_TILE, 1, TOKEN_TILE * TOP_K)
    return (tile_e, n_used.reshape(1), next_e[tile_e], parity[tile_e]), pad_end, pos


def _dispatch_body(pend_ref, pos_ref, h2_ref, xs_hbm, stage, sem, zsem):
    i = pl.program_id(0)
    n_steps = pl.num_programs(0)
    slot = i % 2
    tt = h2_ref.shape[0]
    n_e = pend_ref.shape[0]

    def scatter_wait(s):
        for _ in range(TOP_K):
            pltpu.make_async_copy(stage.at[s], xs_hbm.at[pl.ds(0, tt)], sem.at[s]).wait()

    @pl.when(i == 0)
    def _():
        stage[0] = jnp.zeros(stage.shape[1:], stage.dtype)
        n_tiles = xs_hbm.shape[0] // tt
        n_used = pend_ref[n_e - 1] // tt
        for phase in range(2):
            for e in range(n_e):
                lo = pend_ref[e - 1] if e else 0
                for cond, row0 in ((pend_ref[e] > lo, pend_ref[e] - tt),
                                   (n_used + e < n_tiles, (n_used + e) * tt)):
                    @pl.when(cond)
                    def _():
                        cp = pltpu.make_async_copy(
                            stage.at[0], xs_hbm.at[pl.ds(pl.multiple_of(row0, tt), tt)], zsem)
                        if phase == 0:
                            cp.start()
                        else:
                            cp.wait()

    @pl.when(i >= 2)
    def _():
        scatter_wait(slot)

    stage[slot] = h2_ref[...].reshape(stage.shape[1:])

    def body(r, c):
        for k in range(TOP_K):
            pltpu.make_async_copy(stage.at[slot, r], xs_hbm.at[pos_ref[0, r * TOP_K + k]],
                                  sem.at[slot]).start(priority=k % 2)
        return c
    lax.fori_loop(0, tt, body, 0, unroll=8)

    @pl.when(i == n_steps - 1)
    def _():
        @pl.when(n_steps >= 2)
        def _():
            scatter_wait(1 - slot)
        scatter_wait(slot)


def _moe_dispatch(l, h2, pad_end, pos, n_tiles):
    t, d = h2.shape
    assert TOKEN_TILE == MOE_TILE
    return pl.pallas_call(
        _dispatch_body,
        grid_spec=pltpu.PrefetchScalarGridSpec(
            num_scalar_prefetch=1, grid=(t // TOKEN_TILE,),
            in_specs=[pl.BlockSpec((None, 1, TOKEN_TILE * TOP_K), lambda i, pe: (i, 0, 0),
                                   memory_space=pltpu.SMEM),
                      pl.BlockSpec((TOKEN_TILE, d), lambda i, pe: (i, 0))],
            out_specs=pl.BlockSpec(memory_space=pl.ANY),
            scratch_shapes=[pltpu.VMEM((2, TOKEN_TILE, SUBLANES, LANES), F32),
                            pltpu.SemaphoreType.DMA((2,)),
                            pltpu.SemaphoreType.DMA]),
        out_shape=jax.ShapeDtypeStruct((n_tiles * MOE_TILE, SUBLANES, LANES), F32),
        compiler_params=_params(1),
        name=f"moe_dispatch_{l}",
    )(pad_end, pos, h2)


def _moe_body(l, te_ref, nu_ref, nxe_ref, par_ref, x_ref, bgu_ref, bd_ref, wgu_hbm, wd_hbm, y_ref,
              wgu_st, wd_st, wgu_bf, wd_bf, wsem):
    i = pl.program_id(0)
    de = wd_bf.shape[0]
    e = te_ref[i]
    par = par_ref[i]

    def weight_copies(ex, p):
        return (pltpu.make_async_copy(wgu_hbm.at[l, ex], wgu_st.at[p], wsem.at[p, 0]),
                pltpu.make_async_copy(wd_hbm.at[l, ex], wd_st.at[p], wsem.at[p, 1]))

    @pl.when(i == 0)
    def _():
        for cp in weight_copies(e, 0):
            cp.start()

    @pl.when(i < nu_ref[0])
    def _():
        @pl.when((i == 0) | (e != te_ref[jnp.maximum(i - 1, 0)]))
        def _():
            for cp in weight_copies(e, par):
                cp.wait()
            wgu_bf[...] = wgu_st[par].astype(BF16)
            wd_bf[...] = wd_st[par].astype(BF16)

            @pl.when(nxe_ref[i] != e)
            def _():
                for cp in weight_copies(nxe_ref[i], 1 - par):
                    cp.start()

        x = x_ref[...].reshape(x_ref.shape[0], wgu_bf.shape[0])
        hgu = _dot(x.astype(BF16), wgu_bf[...]) + bgu_ref[...]
        x_glu = jnp.minimum(hgu[:, :de], SWIGLU_LIMIT)
        x_lin = jnp.clip(hgu[:, de:], -SWIGLU_LIMIT, SWIGLU_LIMIT)
        act = x_glu * _sigmoid(SWIGLU_ALPHA * x_glu) * (x_lin + 1.0)
        y = _dot(act.astype(BF16), wd_bf[...]) + bd_ref[...]
        y_ref[...] = y.reshape(y_ref.shape)

    @pl.when(i >= nu_ref[0])
    def _():
        y_ref[...] = jnp.zeros(y_ref.shape, y_ref.dtype)


def _moe_experts(l, xs, tile_tables, w_gu, b_gu, w_d, b_d):
    n_rows = xs.shape[0]
    tile_e, n_used, next_e, parity = tile_tables
    n_tiles = tile_e.shape[0]
    n_e, de, d = w_d.shape[1], w_d.shape[2], w_d.shape[3]
    row_blk = (MOE_TILE,) + xs.shape[1:]
    return pl.pallas_call(
        functools.partial(_moe_body, l),
        grid_spec=pltpu.PrefetchScalarGridSpec(
            num_scalar_prefetch=4, grid=(n_tiles,),
            in_specs=[pl.BlockSpec(row_blk, lambda i, te, nu, *_: (jnp.minimum(i, nu[0] - 1), 0, 0)),
                      pl.BlockSpec((None, None, 1, 2 * de), lambda i, te, *_: (l, te[i], 0, 0)),
                      pl.BlockSpec((None, None, 1, d), lambda i, te, *_: (l, te[i], 0, 0)),
                      pl.BlockSpec(memory_space=pl.ANY),
                      pl.BlockSpec(memory_space=pl.ANY)],
            out_specs=pl.BlockSpec(row_blk, lambda i, *_: (i, 0, 0)),
            scratch_shapes=[pltpu.VMEM((2, d, 2 * de), F32), pltpu.VMEM((2, de, d), F32),
                            pltpu.VMEM((d, 2 * de), BF16), pltpu.VMEM((de, d), BF16),
                            pltpu.SemaphoreType.DMA((2, 2))]),
        out_shape=jax.ShapeDtypeStruct(xs.shape, F32),
        compiler_params=_params(1),
        name=f"moe_experts_{l}",
    )(tile_e, n_used, next_e, parity, xs, b_gu.reshape(b_gu.shape[0], n_e, 1, 2 * de),
      b_d.reshape(b_d.shape[0], n_e, 1, d), w_gu, w_d)


def _combine_body(alpha, seg_ref, pos_ref, posn_ref, x1_ref, g_ref, mod_ref, lng_ref, lnb_ref, yb_hbm,
                  o_ref, buf, sem):
    i = pl.program_id(0)
    n_steps = pl.num_programs(0)
    slot = i % 2
    tt = x1_ref.shape[0]

    def gather_start(idx_ref, s):
        def body(r, c):
            for k in range(TOP_K):
                pltpu.make_async_copy(yb_hbm.at[idx_ref[0, r * TOP_K + k]], buf.at[s, k, r],
                                      sem.at[s]).start(priority=k % 2)
            return c
        lax.fori_loop(0, tt, body, 0, unroll=8)

    @pl.when(i == 0)
    def _():
        gather_start(pos_ref, 0)

    @pl.when(i + 1 < n_steps)
    def _():
        gather_start(posn_ref, 1 - slot)

    for k in range(TOP_K):
        pltpu.make_async_copy(yb_hbm.at[pl.ds(0, tt)], buf.at[slot, k], sem.at[slot]).wait()
    g = g_ref[...]
    y = None
    for k in range(TOP_K):
        part = g[:, k:k + 1] * buf[slot, k].reshape(x1_ref.shape)
        y = part if y is None else y + part
    o_ref[...] = _ln(alpha * x1_ref[...] + mod_ref[5:6, :] * y, lng_ref[...], lnb_ref[...])


def _moe_combine(l, alpha, meta, x1, yb, pos, gates, mod, ln_g, ln_b):
    t, d = x1.shape
    nt = t // TOKEN_TILE
    tile = pl.BlockSpec((TOKEN_TILE, d), lambda i, s: (i, 0))
    pos_blk = lambda f: pl.BlockSpec((None, 1, TOKEN_TILE * TOP_K), f, memory_space=pltpu.SMEM)
    return pl.pallas_call(
        functools.partial(_combine_body, alpha),
        grid_spec=pltpu.PrefetchScalarGridSpec(
            num_scalar_prefetch=1, grid=(nt,),
            in_specs=[pos_blk(lambda i, s: (i, 0, 0)),
                      pos_blk(lambda i, s: (jnp.minimum(i + 1, nt - 1), 0, 0)),
                      tile,
                      pl.BlockSpec((TOKEN_TILE, LANES), lambda i, s: (i, 0)),
                      pl.BlockSpec((None, None, 6, d), lambda i, s: (l, s[i], 0, 0)),
                      pl.BlockSpec((None, None, 1, d), lambda i, s: (l, 1, 0, 0)),
                      pl.BlockSpec((None, None, 1, d), lambda i, s: (l, 1, 0, 0)),
                      pl.BlockSpec(memory_space=pl.ANY)],
            out_specs=tile,
            scratch_shapes=[pltpu.VMEM((2, TOP_K, TOKEN_TILE) + yb.shape[1:], F32),
                            pltpu.SemaphoreType.DMA((2,))]),
        out_shape=jax.ShapeDtypeStruct((t, d), F32),
        compiler_params=_params(1),
        name=f"moe_combine_{l}",
    )(meta["seg"], pos, pos, x1, gates, mod, ln_g, ln_b, yb)


def _route_tables_inv(te, counts, n_experts, n_tiles):
    t = te.shape[0]
    counts = counts[0, :n_experts]
    padded = (counts + MOE_TILE - 1) // MOE_TILE * MOE_TILE
    pad_end = jnp.cumsum(padded).astype(jnp.int32)
    pad_start = pad_end - padded
    experts, ranks = te[:, :TOP_K], te[:, TOP_K:2 * TOP_K]
    sel = experts[:, :, None] == jnp.arange(n_experts, dtype=jnp.int32)
    pos = ranks + jnp.sum(jnp.where(sel, pad_start, 0), axis=-1)
    n_rows = n_tiles * MOE_TILE
    assign = jnp.arange(t * TOP_K, dtype=jnp.int32)
    rows_assign = jnp.full((n_rows,), -1, jnp.int32).at[pos.reshape(-1)].set(
        assign, unique_indices=True)
    valid = rows_assign >= 0
    row = jnp.arange(n_rows, dtype=jnp.int32)
    dump = t * TOP_K + ((row // MOE_TILE) % 2) * MOE_TILE + row % MOE_TILE
    rows_src = jnp.where(valid, rows_assign // TOP_K, 0).reshape(n_tiles, 1, MOE_TILE)
    rows_dst = jnp.where(valid, (rows_assign % TOP_K) * t + rows_assign // TOP_K, dump)
    rows_dst = rows_dst.reshape(n_tiles, 1, MOE_TILE)
    first_dump = (t * TOP_K + MOE_TILE + jnp.arange(MOE_TILE, dtype=jnp.int32)).reshape(1, 1, MOE_TILE)
    dst_prev = jnp.concatenate([first_dump, rows_dst[:-1]], axis=0)
    n_used = (pad_end[-1] // MOE_TILE).astype(jnp.int32)
    tile_ids = jnp.arange(n_tiles, dtype=jnp.int32)
    tile_e = jnp.sum(tile_ids[:, None] * MOE_TILE >= pad_end[None, :], axis=1)
    tile_e = jnp.minimum(tile_e, n_experts - 1).astype(jnp.int32)
    last_e = jnp.sum(jnp.where(tile_ids == n_used - 1, tile_e, 0))
    tile_e = jnp.where(tile_ids < n_used, tile_e, last_e)
    nonempty = padded > 0
    eid = jnp.arange(n_experts, dtype=jnp.int32)
    later = (eid[None, :] > eid[:, None]) & nonempty[None, :]
    next_e = jnp.where(later.any(axis=1), jnp.argmax(later, axis=1).astype(jnp.int32), eid)
    group_idx = jnp.cumsum(nonempty.astype(jnp.int32)) - 1
    return (tile_e, n_used.reshape(1), next_e[tile_e], (group_idx[tile_e] % 2).astype(jnp.int32),
            rows_src, rows_dst, dst_prev)


N_CHUNK = 4


def _moe_fused_body(l, te_ref, nu_ref, nxe_ref, par_ref, srcc_ref, srcn_ref, dstp_ref, dstc_ref,
                    h2_hbm, wgu_hbm, wd_hbm, bgu_ref, bd_ref, y4_hbm,
                    xbuf, ybuf, wgu_st, wd_st, wgu_bf, wd_bf, gsem, ssem, wsem):
    i = pl.program_id(0)
    nu = nu_ref[0]
    slot = i % 2
    tm = xbuf.shape[1]
    de = wd_bf.shape[0]
    e = te_ref[i]
    par = par_ref[i]

    def weight_copies(ex, p):
        return (pltpu.make_async_copy(wgu_hbm.at[l, ex], wgu_st.at[p], wsem.at[p, 0]),
                pltpu.make_async_copy(wd_hbm.at[l, ex], wd_st.at[p], wsem.at[p, 1]))

    def gather_copy(idx_ref, r, s):
        return pltpu.make_async_copy(h2_hbm.at[pl.ds(idx_ref[0, r], 1)], xbuf.at[s, pl.ds(r, 1)],
                                     gsem.at[s])

    def scatter_copy(idx_ref, r, s):
        return pltpu.make_async_copy(ybuf.at[s, pl.ds(r, 1)], y4_hbm.at[pl.ds(idx_ref[0, r], 1)],
                                     ssem.at[s])

    def gather_wait(s):
        pltpu.make_async_copy(h2_hbm.at[pl.ds(0, tm)], xbuf.at[s], gsem.at[s]).wait()

    def scatter_wait(s):
        pltpu.make_async_copy(ybuf.at[s], y4_hbm.at[pl.ds(0, tm)], ssem.at[s]).wait()

    @pl.when(i == 0)
    def _():
        for cp in weight_copies(e, 0):
            cp.start()

        def body(r, c):
            gather_copy(srcc_ref, r, 0).start()
            return c
        lax.fori_loop(0, tm, body, 0, unroll=8)
        n_real = y4_hbm.shape[0] - 2 * tm
        ybuf[...] = jnp.zeros(ybuf.shape, ybuf.dtype)
        for s in range(2):
            cp = pltpu.make_async_copy(ybuf.at[s], y4_hbm.at[pl.ds(n_real + s * tm, tm)], ssem.at[s])
            cp.start()
            cp.wait()

    @pl.when(i < nu)
    def _():
        @pl.when((i == 0) | (e != te_ref[jnp.maximum(i - 1, 0)]))
        def _():
            for cp in weight_copies(e, par):
                cp.wait()
            wgu_bf[...] = wgu_st[par].astype(BF16)
            wd_bf[...] = wd_st[par].astype(BF16)

            @pl.when(nxe_ref[i] != e)
            def _():
                for cp in weight_copies(nxe_ref[i], 1 - par):
                    cp.start()

        gather_wait(slot)
        x = xbuf[slot].astype(BF16)
        rows_per = tm // N_CHUNK
        cw = de // N_CHUNK
        y = None
        for c in range(N_CHUNK):
            for r in range(c * rows_per, (c + 1) * rows_per):
                gather_copy(srcn_ref, r, 1 - slot).start()
                scatter_copy(dstp_ref, r, 1 - slot).start()
            glu = _dot(x, wgu_bf[:, c * cw:(c + 1) * cw]) + bgu_ref[:, c * cw:(c + 1) * cw]
            lin = _dot(x, wgu_bf[:, de + c * cw:de + (c + 1) * cw]) + bgu_ref[:, de + c * cw:de + (c + 1) * cw]
            glu = jnp.minimum(glu, SWIGLU_LIMIT)
            lin = jnp.clip(lin, -SWIGLU_LIMIT, SWIGLU_LIMIT)
            act = glu * _sigmoid(SWIGLU_ALPHA * glu) * (lin + 1.0)
            part = _dot(act.astype(BF16), wd_bf[c * cw:(c + 1) * cw, :])
            y = part if y is None else y + part
        y = y + bd_ref[...]

        @pl.when(i >= 1)
        def _():
            scatter_wait(slot)

        ybuf[slot] = y

        @pl.when(i == nu - 1)
        def _():
            def body(r, c):
                scatter_copy(dstc_ref, r, slot).start()
                return c
            lax.fori_loop(0, tm, body, 0, unroll=8)
            scatter_wait(1 - slot)
            scatter_wait(slot)
            gather_wait(1 - slot)


def _moe_fused(l, h2, tables, w_gu, b_gu, w_d, b_d):
    t, d = h2.shape
    tile_e, n_used, next_e, parity, rows_src, rows_dst, dst_prev = tables
    n_tiles = tile_e.shape[0]
    n_e, de = w_d.shape[1], w_d.shape[2]
    smem_blk = lambda f: pl.BlockSpec((None, 1, MOE_TILE), f, memory_space=pltpu.SMEM)
    cur = lambda i, *_: (i, 0, 0)
    nxt = lambda i, *_: (jnp.minimum(i + 1, n_tiles - 1), 0, 0)
    return pl.pallas_call(
        functools.partial(_moe_fused_body, l),
        grid_spec=pltpu.PrefetchScalarGridSpec(
            num_scalar_prefetch=4, grid=(n_tiles,),
            in_specs=[smem_blk(cur), smem_blk(nxt), smem_blk(cur), smem_blk(cur),
                      pl.BlockSpec(memory_space=pl.ANY),
                      pl.BlockSpec(memory_space=pl.ANY),
                      pl.BlockSpec(memory_space=pl.ANY),
                      pl.BlockSpec((None, None, 1, 2 * de), lambda i, te, *_: (l, te[i], 0, 0)),
                      pl.BlockSpec((None, None, 1, d), lambda i, te, *_: (l, te[i], 0, 0))],
            out_specs=pl.BlockSpec(memory_space=pl.ANY),
            scratch_shapes=[pltpu.VMEM((2, MOE_TILE, d), F32),
                            pltpu.VMEM((2, MOE_TILE, d), F32),
                            pltpu.VMEM((2, d, 2 * de), F32),
                            pltpu.VMEM((2, de, d), F32),
                            pltpu.VMEM((d, 2 * de), BF16),
                            pltpu.VMEM((de, d), BF16),
                            pltpu.SemaphoreType.DMA((2,)),
                            pltpu.SemaphoreType.DMA((2,)),
                            pltpu.SemaphoreType.DMA((2, 2))]),
        out_shape=jax.ShapeDtypeStruct((t * TOP_K + 2 * MOE_TILE, d), F32),
        compiler_params=_params(1),
        name=f"moe_experts_{l}",
    )(tile_e, n_used, next_e, parity, rows_src, rows_src, dst_prev, rows_dst, h2, w_gu, w_d,
      b_gu.reshape(b_gu.shape[0], n_e, 1, 2 * de), b_d.reshape(b_d.shape[0], n_e, 1, d))


def _combine_dense_body(alpha, seg_ref, x1_ref, y0_ref, y1_ref, y2_ref, y3_ref, g_ref, mod_ref,
                        lng_ref, lnb_ref, o_ref):
    g = g_ref[...]
    y = g[:, 0:1] * y0_ref[...]
    for k, y_ref in ((1, y1_ref), (2, y2_ref), (3, y3_ref)):
        y = y + g[:, k:k + 1] * y_ref[...]
    o_ref[...] = _ln(alpha * x1_ref[...] + mod_ref[5:6, :] * y, lng_ref[...], lnb_ref[...])


def _moe_combine_dense(l, alpha, meta, x1, y4, gates, mod, ln_g, ln_b):
    t, d = x1.shape
    nt = t // TOKEN_TILE
    tile = pl.BlockSpec((TOKEN_TILE, d), lambda i, s: (i, 0))
    y_spec = lambda k: pl.BlockSpec((TOKEN_TILE, d), lambda i, s: (k * nt + i, 0))
    return pl.pallas_call(
        functools.partial(_combine_dense_body, alpha),
        grid_spec=pltpu.PrefetchScalarGridSpec(
            num_scalar_prefetch=1, grid=(nt,),
            in_specs=[tile, y_spec(0), y_spec(1), y_spec(2), y_spec(3),
                      pl.BlockSpec((TOKEN_TILE, LANES), lambda i, s: (i, 0)),
                      pl.BlockSpec((None, None, 6, d), lambda i, s: (l, s[i], 0, 0)),
                      pl.BlockSpec((None, None, 1, d), lambda i, s: (l, 1, 0, 0)),
                      pl.BlockSpec((None, None, 1, d), lambda i, s: (l, 1, 0, 0))],
            out_specs=tile),
        out_shape=jax.ShapeDtypeStruct((t, d), F32),
        compiler_params=_params(1),
        name=f"moe_combine_{l}",
    )(meta["seg"], x1, y4, y4, y4, y4, gates, mod, ln_g, ln_b)


def _token_meta(n_ctx_seq, seq, n_lat, lat_seq):
    seg, prev, nxt = [], [], []
    for n_seq, length, seg_of in ((n_ctx_seq, seq, lambda b: 0), (n_lat, lat_seq, lambda b: 1 + b)):
        per = length // TOKEN_TILE
        for b in range(n_seq):
            for j in range(per):
                seg.append(seg_of(b))
                prev.append(int(j > 0))
                nxt.append(int(j < per - 1))
    as_i32 = lambda a: jnp.asarray(np.asarray(a, np.int32))
    return {"seg": as_i32(seg), "prev": as_i32(prev), "next": as_i32(nxt)}


def kernel(x_prompt, x_sample, c, cache_k, cache_v, c_ctx, w_mod, b_mod, ln_g, ln_b, w_in_ab, sgu_ln_g, sgu_ln_b, w_spatial, b_spatial, conv_w, conv_b, conv_ln_g, conv_ln_b, w_out_ab, w_qkv, rpb, w_out_c, w_router, b_router, w_gate_up, b_gate_up, w_down, b_down):
    n_ctx_seq, seq, d = x_prompt.shape
    n_lat, lat_seq, _ = x_sample.shape
    depth = w_mod.shape[0]
    n_heads, head_dim = cache_k.shape[3], cache_k.shape[4]
    n_experts = w_router.shape[-1]
    ca = sgu_ln_g.shape[-1]
    n_even, n_odd = w_in_ab.shape[0], w_qkv.shape[0]
    t_ctx, t_lat = n_ctx_seq * seq, n_lat * lat_seq
    t = t_ctx + t_lat
    rows_n = lat_seq // GRID_W
    assert seq % TOKEN_TILE == 0 and lat_seq % TOKEN_TILE == 0 and 1 + n_lat <= SUBLANES
    assert TOKEN_TILE % CHUNK == 0 and HALO >= CONV_K // 2 and ca == w_out_ab.shape[1] // 2
    assert rows_n % Q_ROWS == 0 and rows_n >= K_ROWS and t_ctx % (Q_ROWS * GRID_W) == 0
    assert n_heads * head_dim == d and 2 * head_dim == LANES and n_experts <= LANES
    assert (t * TOP_K) % MOE_TILE == 0
    assert d == SUBLANES * LANES, "MoE rows are moved as one (SUBLANES, LANES) f32 tile each"
    alpha = float((2 * depth) ** 0.25)
    scale = float(head_dim ** -0.5)
    assert np.frexp(scale)[0] == 0.5, "the attention scale is folded into the bf16 queries"
    meta = _token_meta(n_ctx_seq, seq, n_lat, lat_seq)

    x = jnp.concatenate([x_prompt.reshape(t_ctx, d), x_sample.reshape(t_lat, d)], axis=0)
    cvec = jnp.zeros((SUBLANES, d), F32).at[0].set(c_ctx).at[1:1 + n_lat].set(c)
    mod = _modulation(cvec, w_mod, b_mod).reshape(depth, SUBLANES, 6, d)

    pad_e = LANES - n_experts
    common = {
        "n_experts": n_experts,
        "ln_g": ln_g.reshape(depth, 2, 1, d), "ln_b": ln_b.reshape(depth, 2, 1, d),
        "w_router": jnp.pad(w_router, ((0, 0), (0, 0), (0, pad_e))).astype(BF16),
        "b_router": jnp.pad(b_router, ((0, 0), (0, pad_e)), constant_values=PAD_LOGIT).reshape(depth, 1, LANES),
    }
    even = dict(common)
    even.update({
        "w_in": w_in_ab.astype(BF16), "sgu_g": sgu_ln_g.reshape(n_even, 1, ca),
        "sgu_b": sgu_ln_b.reshape(n_even, 1, ca), "w_sp": w_spatial.astype(BF16),
        "b_sp": jnp.repeat(jnp.transpose(b_spatial, (0, 2, 1)), ca // G_A, axis=2),
        "conv_w": conv_w, "conv_b": conv_b.reshape(n_even, 1, ca),
        "cln_g": conv_ln_g.reshape(n_even, 1, ca), "cln_b": conv_ln_b.reshape(n_even, 1, ca),
        "w_out_ab": w_out_ab.astype(BF16)})
    odd = dict(common)
    odd["w_out_c"] = w_out_c.astype(BF16)
    w_qkv_bf = w_qkv.astype(BF16)
    lat_tables = _latent_window_tables(rows_n)
    n_tiles = t * TOP_K // MOE_TILE + n_experts

    new_k, new_v = [], []
    for l in range(depth):
        i = l // 2
        if l % 2 == 0:
            x1, h2, top_e, gates, counts = _even_layer(l, alpha, meta, x, mod, even)
        else:
            q, k, v, k32, v32 = _qkv_proj(l, meta, x, mod, w_qkv_bf, t_ctx)
            new_k.append(k32.reshape(n_ctx_seq, seq, n_heads, head_dim))
            new_v.append(v32.reshape(n_ctx_seq, seq, n_heads, head_dim))
            o_ctx = _attn_ctx(q, k, v, n_ctx_seq, seq, scale)
            bias = _latent_bias(rpb[i], lat_tables[2])
            o_lat = _attn_lat(q, k, v, cache_k[:, i].reshape(n_lat, -1, d),
                              cache_v[:, i].reshape(n_lat, -1, d), bias, lat_tables,
                              t_ctx, n_lat, lat_seq, scale)
            x1, h2, top_e, gates, counts = _odd_out_proj(l, alpha, meta, x, o_ctx, o_lat, mod, odd)
        tile_tables, pad_end, pos = _route_tables(top_e, counts, n_experts, n_tiles)
        xs = _moe_dispatch(l, h2, pad_end, pos, n_tiles)
        yb = _moe_experts(l, xs, tile_tables, w_gate_up, b_gate_up, w_down, b_down)
        x = _moe_combine(l, alpha, meta, x1, yb, pos, gates, mod, common["ln_g"], common["ln_b"])

    y_prompt = x[:t_ctx].reshape(n_ctx_seq, seq, d)
    y_sample = x[t_ctx:].reshape(n_lat, lat_seq, d)
    return (y_prompt, y_sample, jnp.stack(new_k, axis=1), jnp.stack(new_v, axis=1))
```

```python
import functools

import numpy as np
import jax
import jax.numpy as jnp
from jax import lax
from jax.experimental import pallas as pl
from jax.experimental.pallas import tpu as pltpu

F32 = jnp.float32
BF16 = jnp.bfloat16

GRID_W = 64
G_A = 8
CHUNK = 128
CONV_K = 31
WIN_ROWS = 8
WIN_COLS = 16
TOP_K = 4
SWIGLU_ALPHA = 1.702
SWIGLU_LIMIT = 7.0
LN_EPS = 1e-5
NEG_INF = -1e30

LANES = 128
SUBLANES = 8
VMEM_LIMIT = 56 * 1024 * 1024

TOKEN_TILE = 256
HALO = 16
MOE_TILE = 256
Q_ROWS = 4
K_ROWS = 12
PAD_LOGIT = -3e38


def _ln(x, g, b):
    mu = jnp.mean(x, axis=-1, keepdims=True)
    xc = x - mu
    var = jnp.mean(xc * xc, axis=-1, keepdims=True)
    return xc * lax.rsqrt(var + LN_EPS) * g + b


def _gelu(x):
    return 0.5 * x * (1.0 + jnp.tanh(0.7978845608028654 * (x + 0.044715 * (x * x * x))))


def _sigmoid(x):
    return jax.nn.sigmoid(x)


def _dot(a, b):
    return jnp.dot(a, b, preferred_element_type=F32)


def _dot_nt(a, b):
    return lax.dot_general(a, b, (((1,), (1,)), ((), ())), preferred_element_type=F32)


def _params(n_axes):
    return pltpu.CompilerParams(dimension_semantics=("arbitrary",) * n_axes,
                                vmem_limit_bytes=VMEM_LIMIT)


def _mod_body(c_ref, w_ref, b_ref, o_ref):
    c = c_ref[...]
    s = (c * _sigmoid(c)).astype(BF16)
    o_ref[...] = _dot(s, w_ref[...].astype(BF16)) + b_ref[...]


def _modulation(cvec, w_mod, b_mod):
    depth, d, n = w_mod.shape
    tn = n // 4
    return pl.pallas_call(
        _mod_body,
        grid=(depth, n // tn),
        in_specs=[pl.BlockSpec((SUBLANES, d), lambda l, j: (0, 0)),
                  pl.BlockSpec((None, d, tn), lambda l, j: (l, 0, j)),
                  pl.BlockSpec((None, 1, tn), lambda l, j: (l, 0, j))],
        out_specs=pl.BlockSpec((None, SUBLANES, tn), lambda l, j: (l, 0, j)),
        out_shape=jax.ShapeDtypeStruct((depth, SUBLANES, n), F32),
        compiler_params=_params(2),
        name="adaln_modulation",
    )(cvec, w_mod, b_mod.reshape(depth, 1, n))


def _post_mixer(alpha, x, y, mod_ref, lng_ref, lnb_ref, wr_ref, br_ref,
                x1_ref, h2_ref, te_ref, tg_ref, cnt_ref, cnt_scr):
    i = pl.program_id(0)
    x1 = _ln(alpha * x + mod_ref[2:3, :] * y, lng_ref[...], lnb_ref[...])
    x1_ref[...] = x1
    h2 = x1 * (1.0 + mod_ref[4:5, :]) + mod_ref[3:4, :]
    h2_ref[...] = h2
    logits = _dot(h2.astype(BF16), wr_ref[...]) + br_ref[...]
    n_e = cnt_scr.shape[0]
    lt = logits.T[:n_e]
    tt = lt.shape[1]
    eidx = lax.broadcasted_iota(jnp.int32, lt.shape, 0)
    vals, idxs = [], []
    for _ in range(TOP_K):
        m = jnp.max(lt, axis=0, keepdims=True)
        idx = jnp.min(jnp.where(lt == m, eidx, n_e), axis=0, keepdims=True)
        vals.append(m)
        idxs.append(idx)
        lt = jnp.where(eidx == idx, -jnp.inf, lt)
    exps = [jnp.exp(v - vals[0]) for v in vals]
    den = exps[0]
    for e in exps[1:]:
        den = den + e

    @pl.when(i == 0)
    def _():
        cnt_scr[...] = jnp.zeros(cnt_scr.shape, cnt_scr.dtype)

    onehot = jnp.zeros(lt.shape, F32)
    for k in range(TOP_K):
        onehot = onehot + (eidx == idxs[k]).astype(F32)
    row = lax.broadcasted_iota(jnp.int32, (tt, tt), 0)
    col = lax.broadcasted_iota(jnp.int32, (tt, tt), 1)
    before = _dot(onehot.astype(BF16), (row < col).astype(BF16)) + cnt_scr[:, 0:1]
    cnt = cnt_scr[...] + jnp.sum(onehot, axis=1, keepdims=True)
    cnt_scr[...] = cnt
    cnt_ref[...] = cnt.astype(jnp.int32)

    row_te = lax.broadcasted_iota(jnp.int32, te_ref.shape, 0)
    row_tg = lax.broadcasted_iota(jnp.int32, (LANES, tt), 0)
    te = jnp.zeros(te_ref.shape, jnp.int32)
    tg = jnp.zeros((LANES, tt), F32)
    for k in range(TOP_K):
        rank = jnp.sum(jnp.where(eidx == idxs[k], before, 0.0), axis=0, keepdims=True)
        te = jnp.where(row_te == k, idxs[k], te)
        te = jnp.where(row_te == TOP_K + k, rank.astype(jnp.int32), te)
        tg = jnp.where(row_tg == k, exps[k] / den, tg)
    te_ref[...] = te
    tg_ref[...] = tg.T


def _expert_rows(n_experts):
    return -(-n_experts // SUBLANES) * SUBLANES


def _epilogue_specs(l, d, n_experts):
    in_specs = [pl.BlockSpec((None, None, 1, d), lambda i, s, p, n: (l, 0, 0, 0)),
                pl.BlockSpec((None, None, 1, d), lambda i, s, p, n: (l, 0, 0, 0)),
                pl.BlockSpec((None, d, LANES), lambda i, s, p, n: (l, 0, 0)),
                pl.BlockSpec((None, 1, LANES), lambda i, s, p, n: (l, 0, 0))]
    out_specs = [pl.BlockSpec((TOKEN_TILE, d), lambda i, s, p, n: (i, 0)),
                 pl.BlockSpec((TOKEN_TILE, d), lambda i, s, p, n: (i, 0)),
                 pl.BlockSpec((2 * TOP_K, TOKEN_TILE), lambda i, s, p, n: (0, i)),
                 pl.BlockSpec((TOKEN_TILE, LANES), lambda i, s, p, n: (i, 0)),
                 pl.BlockSpec((_expert_rows(n_experts), LANES), lambda i, s, p, n: (0, 0))]
    return in_specs, out_specs


def _epilogue_out_shapes(t, d, n_experts):
    return [jax.ShapeDtypeStruct((t, d), F32), jax.ShapeDtypeStruct((t, d), F32),
            jax.ShapeDtypeStruct((2 * TOP_K, t), jnp.int32), jax.ShapeDtypeStruct((t, LANES), F32),
            jax.ShapeDtypeStruct((_expert_rows(n_experts), LANES), jnp.int32)]


def _epilogue_scratch(n_experts):
    return [pltpu.VMEM((_expert_rows(n_experts), LANES), F32)]


def _even_body(alpha, seg_ref, prev_ref, next_ref,
               x_ref, xp_ref, xn_ref, mod_ref, win_ref, sg_ref, sb_ref, wsp_ref, bsp_ref,
               cw_ref, cb_ref, cg_ref, cbb_ref, wout_ref, lng_ref, lnb_ref, wr_ref, br_ref,
               x1_ref, h2_ref, te_ref, tg_ref, cnt_ref, gl_scr, cnt_scr):
    i = pl.program_id(0)
    tt = x_ref.shape[0]
    ca = sg_ref.shape[-1]
    cb2 = 2 * ca
    x = x_ref[...]
    sc = 1.0 + mod_ref[1:2, :]
    sh = mod_ref[0:1, :]
    z = _dot((x * sc + sh).astype(BF16), win_ref[...])

    u = _gelu(z[:, :ca])
    v = _ln(_gelu(z[:, ca:cb2]), sg_ref[...], sb_ref[...]).astype(BF16)
    half = lax.broadcasted_iota(jnp.int32, (CHUNK, LANES), 1) < (LANES // 2)
    chunks = []
    for ck in range(tt // CHUNK):
        cols = []
        for j in range(ca // LANES):
            vblk = v[ck * CHUNK:(ck + 1) * CHUNK, j * LANES:(j + 1) * LANES]
            cols.append(jnp.where(half, _dot(wsp_ref[2 * j], vblk), _dot(wsp_ref[2 * j + 1], vblk)))
        chunks.append(jnp.concatenate(cols, axis=1) + bsp_ref[...])
    y_a = u * jnp.concatenate(chunks, axis=0)

    def glu_rows(xh_ref):
        zh = _dot((xh_ref[...] * sc + sh).astype(BF16), win_ref[:, cb2:])
        return zh[:, :ca] * _sigmoid(zh[:, ca:])

    gl_scr[0:HALO, :] = jnp.where(prev_ref[i] > 0, glu_rows(xp_ref), 0.0)
    gl_scr[HALO:HALO + tt, :] = z[:, cb2:cb2 + ca] * _sigmoid(z[:, cb2 + ca:])
    gl_scr[HALO + tt:, :] = jnp.where(next_ref[i] > 0, glu_rows(xn_ref), 0.0)
    off = HALO - CONV_K // 2
    g_ext = gl_scr[...]
    n_ext = g_ext.shape[0]
    dc = None
    for res in range(SUBLANES):
        taps = [k for k in range(CONV_K) if (off + k) % SUBLANES == res]
        if not taps:
            continue
        shifted = g_ext if res == 0 else pltpu.roll(g_ext, n_ext - res, axis=0)
        for k in taps:
            q = (off + k) // SUBLANES * SUBLANES
            term = shifted[q:q + tt, :] * cw_ref[k:k + 1, :]
            dc = term if dc is None else dc + term
    yb = _ln(dc + cb_ref[...], cg_ref[...], cbb_ref[...])
    y_b = yb * _sigmoid(yb)

    y = _dot(jnp.concatenate([y_a, y_b], axis=1).astype(BF16), wout_ref[...])
    _post_mixer(alpha, x, y, mod_ref, lng_ref, lnb_ref, wr_ref, br_ref,
                x1_ref, h2_ref, te_ref, tg_ref, cnt_ref, cnt_scr)


def _even_layer(l, alpha, meta, x, mod, p):
    t, d = x.shape
    li = l // 2
    nh = TOKEN_TILE // HALO
    n_halo = t // HALO
    ca = p["sgu_g"].shape[-1]
    ep_in, ep_out = _epilogue_specs(l, d, p["n_experts"])
    const3 = lambda i, s, pv, nx: (li, 0, 0)
    in_specs = [
        pl.BlockSpec((TOKEN_TILE, d), lambda i, s, pv, nx: (i, 0)),
        pl.BlockSpec((HALO, d), lambda i, s, pv, nx: (jnp.maximum(i * nh - 1, 0), 0)),
        pl.BlockSpec((HALO, d), lambda i, s, pv, nx: (jnp.minimum((i + 1) * nh, n_halo - 1), 0)),
        pl.BlockSpec((None, None, 6, d), lambda i, s, pv, nx: (l, s[i], 0, 0)),
        pl.BlockSpec((None, d, 4 * ca), const3),
        pl.BlockSpec((None, 1, ca), const3),
        pl.BlockSpec((None, 1, ca), const3),
        pl.BlockSpec((None, G_A, CHUNK, CHUNK), lambda i, s, pv, nx: (li, 0, 0, 0)),
        pl.BlockSpec((None, CHUNK, ca), const3),
        pl.BlockSpec((None, CONV_K, ca), const3),
        pl.BlockSpec((None, 1, ca), const3),
        pl.BlockSpec((None, 1, ca), const3),
        pl.BlockSpec((None, 1, ca), const3),
        pl.BlockSpec((None, 2 * ca, d), const3),
    ] + ep_in
    return pl.pallas_call(
        functools.partial(_even_body, alpha),
        grid_spec=pltpu.PrefetchScalarGridSpec(
            num_scalar_prefetch=3, grid=(t // TOKEN_TILE,),
            in_specs=in_specs, out_specs=ep_out,
            scratch_shapes=[pltpu.VMEM((TOKEN_TILE + 2 * HALO, ca), F32)]
            + _epilogue_scratch(p["n_experts"])),
        out_shape=_epilogue_out_shapes(t, d, p["n_experts"]),
        compiler_params=_params(1),
        name=f"even_mixer_{l}",
    )(meta["seg"], meta["prev"], meta["next"], x, x, x, mod,
      p["w_in"], p["sgu_g"], p["sgu_b"], p["w_sp"], p["b_sp"], p["conv_w"], p["conv_b"],
      p["cln_g"], p["cln_b"], p["w_out_ab"], p["ln_g"], p["ln_b"], p["w_router"], p["b_router"])


def _qkv_body(n_ctx_tiles, seg_ref, x_ref, mod_ref, w_ref, q_ref, k_ref, v_ref, k32_ref, v32_ref):
    d = x_ref.shape[1]
    h = (x_ref[...] * (1.0 + mod_ref[1:2, :]) + mod_ref[0:1, :]).astype(BF16)
    qkv = _dot(h, w_ref[...])
    q_ref[...] = qkv[:, :d].astype(BF16)
    k = qkv[:, d:2 * d]
    v = qkv[:, 2 * d:]
    k_ref[...] = k.astype(BF16)
    v_ref[...] = v.astype(BF16)

    @pl.when(pl.program_id(0) < n_ctx_tiles)
    def _():
        k32_ref[...] = k
        v32_ref[...] = v


def _qkv_proj(l, meta, x, mod, w_qkv, t_ctx):
    t, d = x.shape
    li = l // 2
    n_ctx_tiles = t_ctx // TOKEN_TILE
    tile = pl.BlockSpec((TOKEN_TILE, d), lambda i, s: (i, 0))
    ctx_tile = pl.BlockSpec((TOKEN_TILE, d), lambda i, s: (jnp.minimum(i, n_ctx_tiles - 1), 0))
    return pl.pallas_call(
        functools.partial(_qkv_body, n_ctx_tiles),
        grid_spec=pltpu.PrefetchScalarGridSpec(
            num_scalar_prefetch=1, grid=(t // TOKEN_TILE,),
            in_specs=[tile,
                      pl.BlockSpec((None, None, 6, d), lambda i, s: (l, s[i], 0, 0)),
                      pl.BlockSpec((None, d, 3 * d), lambda i, s: (li, 0, 0))],
            out_specs=[tile] * 3 + [ctx_tile] * 2),
        out_shape=[jax.ShapeDtypeStruct((t, d), BF16)] * 3 + [jax.ShapeDtypeStruct((t_ctx, d), F32)] * 2,
        compiler_params=_params(1),
        name=f"qkv_proj_{l}",
    )(meta["seg"], x, mod, w_qkv)


def _head_pair_attention(q2, k_parts, v_parts, bias_parts, scale):
    lane = lax.broadcasted_iota(jnp.int32, q2.shape, 1)
    outs = []
    for hh in range(2):
        qm = jnp.where((lane >= hh * (LANES // 2)) & (lane < (hh + 1) * (LANES // 2)), q2 * scale,
                       jnp.zeros_like(q2))
        ss = []
        for j, kp in enumerate(k_parts):
            s = _dot_nt(qm, kp)
            if bias_parts[hh][j] is not None:
                s = s + bias_parts[hh][j]
            ss.append(s)
        m = ss[0].max(axis=-1, keepdims=True)
        for s in ss[1:]:
            m = jnp.maximum(m, s.max(axis=-1, keepdims=True))
        den = None
        o = None
        for s, vp in zip(ss, v_parts):
            e = jnp.exp(s - m)
            es = e.sum(axis=-1, keepdims=True)
            den = es if den is None else den + es
            pv = _dot(e.astype(BF16), vp)
            o = pv if o is None else o + pv
        outs.append(o / den)
    return jnp.where(lane < LANES // 2, outs[0], outs[1])


def _attn_ctx_body(scale, q_ref, k_ref, v_ref, o_ref):
    d = q_ref.shape[1]
    for pr in range(d // LANES):
        sl = slice(pr * LANES, (pr + 1) * LANES)
        o = _head_pair_attention(q_ref[:, sl], [k_ref[:, sl]], [v_ref[:, sl]],
                                 [[None], [None]], scale)
        o_ref[:, sl] = o.astype(o_ref.dtype)


def _attn_ctx(q, k, v, n_seq, seq, scale):
    d = q.shape[1]
    blk = pl.BlockSpec((seq, d), lambda b: (b, 0))
    return pl.pallas_call(
        functools.partial(_attn_ctx_body, scale),
        grid=(n_seq,),
        in_specs=[blk, blk, blk],
        out_specs=blk,
        out_shape=jax.ShapeDtypeStruct((n_seq * seq, d), BF16),
        compiler_params=_params(1),
        name="attn_ctx",
    )(q, k, v)


def _attn_lat_body(scale, cls_ref, kb_ref, q_ref, k0_ref, k1_ref, k2_ref, v0_ref, v1_ref, v2_ref,
                   ck_ref, cv_ref, bias_ref, o_ref):
    d = q_ref.shape[1]
    for pr in range(d // LANES):
        sl = slice(pr * LANES, (pr + 1) * LANES)
        k_loc = jnp.concatenate([k0_ref[:, sl], k1_ref[:, sl], k2_ref[:, sl]], axis=0)
        v_loc = jnp.concatenate([v0_ref[:, sl], v1_ref[:, sl], v2_ref[:, sl]], axis=0)
        ck = ck_ref[:, sl].astype(BF16)
        cv = cv_ref[:, sl].astype(BF16)
        o = _head_pair_attention(q_ref[:, sl], [k_loc, ck], [v_loc, cv],
                                 [[bias_ref[2 * pr], None], [bias_ref[2 * pr + 1], None]], scale)
        o_ref[:, sl] = o.astype(o_ref.dtype)


def _latent_window_tables(rows_n):
    wr = min(WIN_ROWS, rows_n)
    n_rt = rows_n // Q_ROWS
    kstart = np.clip(np.arange(n_rt) * Q_ROWS - wr // 2, 0, rows_n - K_ROWS)
    kstart = (kstart // Q_ROWS) * Q_ROWS
    patterns, cls = [], []
    for rt in range(n_rt):
        pat = np.full((Q_ROWS, K_ROWS), -1, np.int64)
        for qi in range(Q_ROWS):
            r = rt * Q_ROWS + qi
            rs = int(np.clip(r - wr // 2, 0, rows_n - wr))
            for kj in range(K_ROWS):
                kr = int(kstart[rt]) + kj
                if rs <= kr < rs + wr:
                    pat[qi, kj] = kr - r + WIN_ROWS - 1
        assert (pat >= 0).sum(axis=1).min() == wr, "key block does not cover the window"
        key = pat.tobytes()
        if key not in [p.tobytes() for p in patterns]:
            patterns.append(pat)
        cls.append([p.tobytes() for p in patterns].index(key))
    return (kstart // Q_ROWS).astype(np.int32), np.asarray(cls, np.int32), np.stack(patterns)


def _latent_bias(rpb, patterns):
    h = rpb.shape[0]
    qc = np.arange(GRID_W)[:, None]
    kc = np.arange(GRID_W)[None, :]
    qcs = np.clip(qc - WIN_COLS // 2, 0, GRID_W - WIN_COLS)
    col_ok = (kc >= qcs) & (kc < qcs + WIN_COLS)
    dc = np.clip(kc - qc + WIN_COLS - 1, 0, 2 * WIN_COLS - 2)
    onehot = (dc[None] == np.arange(2 * WIN_COLS - 1)[:, None, None]) & col_ok[None]
    cm = jnp.einsum("hrd,dqk->hrqk", rpb, jnp.asarray(onehot, F32), precision=lax.Precision.HIGHEST)
    cm = jnp.where(jnp.asarray(col_ok), cm, NEG_INF)
    cx = jnp.concatenate([cm, jnp.full((h, 1, GRID_W, GRID_W), NEG_INF, F32)], axis=1)
    idx = np.where(patterns >= 0, patterns, 2 * WIN_ROWS - 1)
    classes = []
    for pat in idx:
        q_rows = [jnp.concatenate([cx[:, int(dr)] for dr in pat_q], axis=-1) for pat_q in pat]
        classes.append(jnp.concatenate(q_rows, axis=-2))
    return jnp.stack(classes, axis=0)


def _attn_lat(q, k, v, ck, cv, bias, tables, tok0, n_batch, n_tok, scale):
    d = q.shape[1]
    kblk, cls, _ = tables
    n_rt = kblk.shape[0]
    qt = Q_ROWS * GRID_W
    base = tok0 // qt
    per_b = n_tok // qt
    h = bias.shape[1]
    lc = ck.shape[1]

    def kv_spec(j):
        return pl.BlockSpec((qt, d), lambda b, r, c, kb: (base + b * per_b + kb[r] + j, 0))

    return pl.pallas_call(
        functools.partial(_attn_lat_body, scale),
        grid_spec=pltpu.PrefetchScalarGridSpec(
            num_scalar_prefetch=2, grid=(n_batch, n_rt),
            in_specs=[pl.BlockSpec((qt, d), lambda b, r, c, kb: (base + b * per_b + r, 0)),
                      kv_spec(0), kv_spec(1), kv_spec(2), kv_spec(0), kv_spec(1), kv_spec(2),
                      pl.BlockSpec((None, lc, d), lambda b, r, c, kb: (b, 0, 0)),
                      pl.BlockSpec((None, lc, d), lambda b, r, c, kb: (b, 0, 0)),
                      pl.BlockSpec((None, h, qt, K_ROWS * GRID_W), lambda b, r, c, kb: (c[r], 0, 0, 0))],
            out_specs=pl.BlockSpec((qt, d), lambda b, r, c, kb: (b * per_b + r, 0))),
        out_shape=jax.ShapeDtypeStruct((n_batch * n_tok, d), BF16),
        compiler_params=_params(2),
        name="attn_latent",
    )(jnp.asarray(cls), jnp.asarray(kblk), q, k, k, k, v, v, v, ck, cv, bias)


def _proj_body(alpha, n_ctx_tiles, seg_ref, prev_ref, next_ref, x_ref, oc_ref, ol_ref, mod_ref, w_ref,
               lng_ref, lnb_ref, wr_ref, br_ref, x1_ref, h2_ref, te_ref, tg_ref, cnt_ref, cnt_scr):
    o = jnp.where(pl.program_id(0) < n_ctx_tiles, oc_ref[...], ol_ref[...])
    y = _dot(o, w_ref[...])
    _post_mixer(alpha, x_ref[...], y, mod_ref, lng_ref, lnb_ref, wr_ref, br_ref,
                x1_ref, h2_ref, te_ref, tg_ref, cnt_ref, cnt_scr)


def _odd_out_proj(l, alpha, meta, x, o_ctx, o_lat, mod, p):
    t, d = x.shape
    li = l // 2
    nc = o_ctx.shape[0] // TOKEN_TILE
    ep_in, ep_out = _epilogue_specs(l, d, p["n_experts"])
    tile = pl.BlockSpec((TOKEN_TILE, d), lambda i, s, pv, nx: (i, 0))
    return pl.pallas_call(
        functools.partial(_proj_body, alpha, nc),
        grid_spec=pltpu.PrefetchScalarGridSpec(
            num_scalar_prefetch=3, grid=(t // TOKEN_TILE,),
            in_specs=[tile,
                      pl.BlockSpec((TOKEN_TILE, d), lambda i, s, pv, nx: (jnp.minimum(i, nc - 1), 0)),
                      pl.BlockSpec((TOKEN_TILE, d), lambda i, s, pv, nx: (jnp.maximum(i - nc, 0), 0)),
                      pl.BlockSpec((None, None, 6, d), lambda i, s, pv, nx: (l, s[i], 0, 0)),
                      pl.BlockSpec((None, d, d), lambda i, s, pv, nx: (li, 0, 0))] + ep_in,
            out_specs=ep_out, scratch_shapes=_epilogue_scratch(p["n_experts"])),
        out_shape=_epilogue_out_shapes(t, d, p["n_experts"]),
        compiler_params=_params(1),
        name=f"attn_out_proj_{l}",
    )(meta["seg"], meta["prev"], meta["next"], x, o_ctx, o_lat, mod, p["w_out_c"],
      p["ln_g"], p["ln_b"], p["w_router"], p["b_router"])


def _route_tables(te, counts, n_experts):
    t = te.shape[1]
    counts = counts[:n_experts, 0]
    padded = (counts + MOE_TILE - 1) // MOE_TILE * MOE_TILE
    pad_end = jnp.cumsum(padded).astype(jnp.int32)
    pad_start = pad_end - padded
    experts, ranks = te[:TOP_K], te[TOP_K:]
    eid = jnp.arange(n_experts, dtype=jnp.int32)
    sel = experts[None] == eid[:, None, None]
    pos = ranks + jnp.sum(jnp.where(sel, pad_start[:, None, None], 0), axis=0)
    pos = jnp.transpose(pos.reshape(TOP_K, t // TOKEN_TILE, TOKEN_TILE), (1, 2, 0))
    pos = pos.astype(jnp.int32).reshape(t // TOKEN_TILE, 1, TOKEN_TILE * TOP_K)
    return pad_end, pos


def _dispatch_body(pend_ref, pos_ref, h2_ref, xs_hbm, stage, sem, zsem):
    i = pl.program_id(0)
    n_steps = pl.num_programs(0)
    slot = i % 2
    tt = h2_ref.shape[0]
    n_e = pend_ref.shape[0]

    def scatter_wait(s):
        for _ in range(TOP_K):
            pltpu.make_async_copy(stage.at[s], xs_hbm.at[pl.ds(0, tt)], sem.at[s]).wait()

    @pl.when(i == 0)
    def _():
        stage[0] = jnp.zeros(stage.shape[1:], stage.dtype)
        n_tiles = xs_hbm.shape[0] // tt
        n_used = pend_ref[n_e - 1] // tt
        for phase in range(2):
            for e in range(n_e):
                lo = pend_ref[e - 1] if e else 0
                for cond, row0 in ((pend_ref[e] > lo, pend_ref[e] - tt),
                                   (n_used + e < n_tiles, (n_used + e) * tt)):
                    @pl.when(cond)
                    def _():
                        cp = pltpu.make_async_copy(
                            stage.at[0], xs_hbm.at[pl.ds(pl.multiple_of(row0, tt), tt)], zsem)
                        if phase == 0:
                            cp.start()
                        else:
                            cp.wait()

    @pl.when(i >= 2)
    def _():
        scatter_wait(slot)

    stage[slot] = h2_ref[...].reshape(stage.shape[1:])

    def body(r, c):
        for k in range(TOP_K):
            pltpu.make_async_copy(stage.at[slot, r], xs_hbm.at[pos_ref[0, r * TOP_K + k]],
                                  sem.at[slot]).start(priority=k % 2)
        return c
    lax.fori_loop(0, tt, body, 0, unroll=8)

    @pl.when(i == n_steps - 1)
    def _():
        @pl.when(n_steps >= 2)
        def _():
            scatter_wait(1 - slot)
        scatter_wait(slot)


def _moe_dispatch(l, h2, pad_end, pos, n_tiles):
    t, d = h2.shape
    assert TOKEN_TILE == MOE_TILE
    return pl.pallas_call(
        _dispatch_body,
        grid_spec=pltpu.PrefetchScalarGridSpec(
            num_scalar_prefetch=1, grid=(t // TOKEN_TILE,),
            in_specs=[pl.BlockSpec((None, 1, TOKEN_TILE * TOP_K), lambda i, pe: (i, 0, 0),
                                   memory_space=pltpu.SMEM),
                      pl.BlockSpec((TOKEN_TILE, d), lambda i, pe: (i, 0))],
            out_specs=pl.BlockSpec(memory_space=pl.ANY),
            scratch_shapes=[pltpu.VMEM((2, TOKEN_TILE, SUBLANES, LANES), F32),
                            pltpu.SemaphoreType.DMA((2,)),
                            pltpu.SemaphoreType.DMA]),
        out_shape=jax.ShapeDtypeStruct((n_tiles * MOE_TILE, SUBLANES, LANES), F32),
        compiler_params=_params(1),
        name=f"moe_dispatch_{l}",
    )(pad_end, pos, h2)


def _moe_body(l, pend_ref, bgu_ref, bd_ref, xs_hbm, wgu_hbm, wd_hbm, yb_hbm,
              xbuf, ybuf, wgu_st, wd_st, wgu_bf, wd_bf, xsem, ysem, wsem):
    e = pl.program_id(0)
    n_e = pl.num_programs(0)
    tm = xbuf.shape[1]
    de = wd_bf.shape[0]
    par = e % 2
    g_lo = jnp.where(e == 0, 0, pend_ref[jnp.maximum(e - 1, 0)]) // tm
    g_hi = pend_ref[e] // tm
    n_used = pend_ref[n_e - 1] // tm
    n_tiles = yb_hbm.shape[0] // tm

    def weight_copies(ex, p):
        return (pltpu.make_async_copy(wgu_hbm.at[l, ex], wgu_st.at[p], wsem.at[p, 0]),
                pltpu.make_async_copy(wd_hbm.at[l, ex], wd_st.at[p], wsem.at[p, 1]))

    def x_copy(g, s):
        return pltpu.make_async_copy(xs_hbm.at[pl.ds(pl.multiple_of(g * tm, tm), tm)], xbuf.at[s],
                                     xsem.at[s])

    def y_copy(g, s):
        return pltpu.make_async_copy(ybuf.at[s], yb_hbm.at[pl.ds(pl.multiple_of(g * tm, tm), tm)],
                                     ysem.at[s])

    @pl.when(e == 0)
    def _():
        for cp in weight_copies(0, 0):
            cp.start()

        @pl.when(n_used > 0)
        def _():
            x_copy(0, 0).start()

    @pl.when(e + 1 < n_e)
    def _():
        for cp in weight_copies(e + 1, 1 - par):
            cp.start()

    for cp in weight_copies(e, par):
        cp.wait()

    @pl.when(g_hi > g_lo)
    def _():
        wgu_bf[...] = wgu_st[par].astype(BF16)
        wd_bf[...] = wd_st[par].astype(BF16)

    def tile_body(g, carry):
        s = g % 2
        x_copy(g, s).wait()

        @pl.when(g + 1 < n_used)
        def _():
            x_copy(g + 1, 1 - s).start()

        x = xbuf[s].reshape(tm, wgu_bf.shape[0])
        hgu = _dot(x.astype(BF16), wgu_bf[...]) + bgu_ref[...]
        x_glu = jnp.minimum(hgu[:, :de], SWIGLU_LIMIT)
        x_lin = jnp.clip(hgu[:, de:], -SWIGLU_LIMIT, SWIGLU_LIMIT)
        act = x_glu * _sigmoid(SWIGLU_ALPHA * x_glu) * (x_lin + 1.0)
        y = _dot(act.astype(BF16), wd_bf[...]) + bd_ref[...]

        @pl.when(g >= 2)
        def _():
            y_copy(g - 2, s).wait()

        ybuf[s] = y.reshape(ybuf.shape[1:])
        y_copy(g, s).start()
        return carry

    lax.fori_loop(g_lo, g_hi, tile_body, 0)

    @pl.when(e == n_e - 1)
    def _():
        @pl.when(n_used >= 2)
        def _():
            y_copy(n_used - 2, n_used % 2).wait()

        @pl.when(n_used >= 1)
        def _():
            y_copy(n_used - 1, (n_used + 1) % 2).wait()

        ybuf[0] = jnp.zeros(ybuf.shape[1:], ybuf.dtype)
        for phase in range(2):
            for m in range(n_e):
                @pl.when(n_used + m < n_tiles)
                def _():
                    cp = y_copy(n_used + m, 0)
                    if phase == 0:
                        cp.start()
                    else:
                        cp.wait()


def _moe_experts(l, xs, pad_end, w_gu, b_gu, w_d, b_d):
    n_e, de, d = w_d.shape[1], w_d.shape[2], w_d.shape[3]
    row_tile = (MOE_TILE,) + xs.shape[1:]
    return pl.pallas_call(
        functools.partial(_moe_body, l),
        grid_spec=pltpu.PrefetchScalarGridSpec(
            num_scalar_prefetch=1, grid=(n_e,),
            in_specs=[pl.BlockSpec((None, None, 1, 2 * de), lambda e, pe: (l, e, 0, 0)),
                      pl.BlockSpec((None, None, 1, d), lambda e, pe: (l, e, 0, 0)),
                      pl.BlockSpec(memory_space=pl.ANY),
                      pl.BlockSpec(memory_space=pl.ANY),
                      pl.BlockSpec(memory_space=pl.ANY)],
            out_specs=pl.BlockSpec(memory_space=pl.ANY),
            scratch_shapes=[pltpu.VMEM((2,) + row_tile, F32), pltpu.VMEM((2,) + row_tile, F32),
                            pltpu.VMEM((2, d, 2 * de), F32), pltpu.VMEM((2, de, d), F32),
                            pltpu.VMEM((d, 2 * de), BF16), pltpu.VMEM((de, d), BF16),
                            pltpu.SemaphoreType.DMA((2,)), pltpu.SemaphoreType.DMA((2,)),
                            pltpu.SemaphoreType.DMA((2, 2))]),
        out_shape=jax.ShapeDtypeStruct(xs.shape, F32),
        compiler_params=_params(1),
        name=f"moe_experts_{l}",
    )(pad_end, b_gu.reshape(b_gu.shape[0], n_e, 1, 2 * de), b_d.reshape(b_d.shape[0], n_e, 1, d),
      xs, w_gu, w_d)


def _combine_body(alpha, n_ctx_tiles, seg_ref, pos_ref, posn_ref, x1_ref, g_ref, mod_ref, lng_ref,
                  lnb_ref, yb_hbm, o_ref, *rest):
    o2_ref = rest[0] if len(rest) == 3 else None
    buf, sem = rest[-2:]
    i = pl.program_id(0)
    n_steps = pl.num_programs(0)
    slot = i % 2
    tt = x1_ref.shape[0]

    def gather_start(idx_ref, s):
        def body(r, c):
            for k in range(TOP_K):
                pltpu.make_async_copy(yb_hbm.at[idx_ref[0, r * TOP_K + k]], buf.at[s, k, r],
                                      sem.at[s]).start(priority=k % 2)
            return c
        lax.fori_loop(0, tt, body, 0, unroll=8)

    @pl.when(i == 0)
    def _():
        gather_start(pos_ref, 0)

    @pl.when(i + 1 < n_steps)
    def _():
        gather_start(posn_ref, 1 - slot)

    for k in range(TOP_K):
        pltpu.make_async_copy(yb_hbm.at[pl.ds(0, tt)], buf.at[slot, k], sem.at[slot]).wait()
    g = g_ref[...]
    y = None
    for k in range(TOP_K):
        part = g[:, k:k + 1] * buf[slot, k].reshape(x1_ref.shape)
        y = part if y is None else y + part
    out = _ln(alpha * x1_ref[...] + mod_ref[5:6, :] * y, lng_ref[...], lnb_ref[...])
    if o2_ref is None:
        o_ref[...] = out
    else:
        @pl.when(i < n_ctx_tiles)
        def _():
            o_ref[...] = out

        @pl.when(i >= n_ctx_tiles)
        def _():
            o2_ref[...] = out


def _moe_combine(l, alpha, meta, x1, yb, pos, gates, mod, ln_g, ln_b, split_at=None):
    t, d = x1.shape
    nt = t // TOKEN_TILE
    tile = pl.BlockSpec((TOKEN_TILE, d), lambda i, s: (i, 0))
    pos_blk = lambda f: pl.BlockSpec((None, 1, TOKEN_TILE * TOP_K), f, memory_space=pltpu.SMEM)
    if split_at is None:
        nc, out_specs, out_shape = 0, tile, jax.ShapeDtypeStruct((t, d), F32)
    else:
        nc = split_at // TOKEN_TILE
        out_specs = [pl.BlockSpec((TOKEN_TILE, d), lambda i, s: (jnp.minimum(i, nc - 1), 0)),
                     pl.BlockSpec((TOKEN_TILE, d), lambda i, s: (jnp.maximum(i - nc, 0), 0))]
        out_shape = [jax.ShapeDtypeStruct((split_at, d), F32),
                     jax.ShapeDtypeStruct((t - split_at, d), F32)]
    return pl.pallas_call(
        functools.partial(_combine_body, alpha, nc),
        grid_spec=pltpu.PrefetchScalarGridSpec(
            num_scalar_prefetch=1, grid=(nt,),
            in_specs=[pos_blk(lambda i, s: (i, 0, 0)),
                      pos_blk(lambda i, s: (jnp.minimum(i + 1, nt - 1), 0, 0)),
                      tile,
                      pl.BlockSpec((TOKEN_TILE, LANES), lambda i, s: (i, 0)),
                      pl.BlockSpec((None, None, 6, d), lambda i, s: (l, s[i], 0, 0)),
                      pl.BlockSpec((None, None, 1, d), lambda i, s: (l, 1, 0, 0)),
                      pl.BlockSpec((None, None, 1, d), lambda i, s: (l, 1, 0, 0)),
                      pl.BlockSpec(memory_space=pl.ANY)],
            out_specs=out_specs,
            scratch_shapes=[pltpu.VMEM((2, TOP_K, TOKEN_TILE) + yb.shape[1:], F32),
                            pltpu.SemaphoreType.DMA((2,))]),
        out_shape=out_shape,
        compiler_params=_params(1),
        name=f"moe_combine_{l}",
    )(meta["seg"], pos, pos, x1, gates, mod, ln_g, ln_b, yb)


def _token_meta(n_ctx_seq, seq, n_lat, lat_seq):
    seg, prev, nxt = [], [], []
    for n_seq, length, seg_of in ((n_ctx_seq, seq, lambda b: 0), (n_lat, lat_seq, lambda b: 1 + b)):
        per = length // TOKEN_TILE
        for b in range(n_seq):
            for j in range(per):
                seg.append(seg_of(b))
                prev.append(int(j > 0))
                nxt.append(int(j < per - 1))
    as_i32 = lambda a: jnp.asarray(np.asarray(a, np.int32))
    return {"seg": as_i32(seg), "prev": as_i32(prev), "next": as_i32(nxt)}


def kernel(x_prompt, x_sample, c, cache_k, cache_v, c_ctx, w_mod, b_mod, ln_g, ln_b, w_in_ab, sgu_ln_g, sgu_ln_b, w_spatial, b_spatial, conv_w, conv_b, conv_ln_g, conv_ln_b, w_out_ab, w_qkv, rpb, w_out_c, w_router, b_router, w_gate_up, b_gate_up, w_down, b_down):
    n_ctx_seq, seq, d = x_prompt.shape
    n_lat, lat_seq, _ = x_sample.shape
    depth = w_mod.shape[0]
    n_heads, head_dim = cache_k.shape[3], cache_k.shape[4]
    n_experts = w_router.shape[-1]
    ca = sgu_ln_g.shape[-1]
    n_even, n_odd = w_in_ab.shape[0], w_qkv.shape[0]
    t_ctx, t_lat = n_ctx_seq * seq, n_lat * lat_seq
    t = t_ctx + t_lat
    rows_n = lat_seq // GRID_W
    assert seq % TOKEN_TILE == 0 and lat_seq % TOKEN_TILE == 0 and 1 + n_lat <= SUBLANES
    assert TOKEN_TILE % CHUNK == 0 and HALO >= CONV_K // 2 and ca == w_out_ab.shape[1] // 2
    assert rows_n % Q_ROWS == 0 and rows_n >= K_ROWS and t_ctx % (Q_ROWS * GRID_W) == 0
    assert n_heads * head_dim == d and 2 * head_dim == LANES and n_experts <= LANES
    assert (t * TOP_K) % MOE_TILE == 0
    assert d == SUBLANES * LANES, "MoE rows are moved as one (SUBLANES, LANES) f32 tile each"
    alpha = float((2 * depth) ** 0.25)
    scale = float(head_dim ** -0.5)
    assert np.frexp(scale)[0] == 0.5, "the attention scale is folded into the bf16 queries"
    meta = _token_meta(n_ctx_seq, seq, n_lat, lat_seq)

    x = jnp.concatenate([x_prompt.reshape(t_ctx, d), x_sample.reshape(t_lat, d)], axis=0)
    cvec = jnp.zeros((SUBLANES, d), F32).at[0].set(c_ctx).at[1:1 + n_lat].set(c)
    mod = _modulation(cvec, w_mod, b_mod).reshape(depth, SUBLANES, 6, d)

    pad_e = LANES - n_experts
    common = {
        "n_experts": n_experts,
        "ln_g": ln_g.reshape(depth, 2, 1, d), "ln_b": ln_b.reshape(depth, 2, 1, d),
        "w_router": jnp.pad(w_router, ((0, 0), (0, 0), (0, pad_e))).astype(BF16),
        "b_router": jnp.pad(b_router, ((0, 0), (0, pad_e)), constant_values=PAD_LOGIT).reshape(depth, 1, LANES),
    }
    even = dict(common)
    even.update({
        "w_in": w_in_ab.astype(BF16), "sgu_g": sgu_ln_g.reshape(n_even, 1, ca),
        "sgu_b": sgu_ln_b.reshape(n_even, 1, ca), "w_sp": w_spatial.astype(BF16),
        "b_sp": jnp.repeat(jnp.transpose(b_spatial, (0, 2, 1)), ca // G_A, axis=2),
        "conv_w": conv_w, "conv_b": conv_b.reshape(n_even, 1, ca),
        "cln_g": conv_ln_g.reshape(n_even, 1, ca), "cln_b": conv_ln_b.reshape(n_even, 1, ca),
        "w_out_ab": w_out_ab.astype(BF16)})
    odd = dict(common)
    odd["w_out_c"] = w_out_c.astype(BF16)
    w_qkv_bf = w_qkv.astype(BF16)
    lat_tables = _latent_window_tables(rows_n)
    n_tiles = t * TOP_K // MOE_TILE + n_experts

    new_k, new_v = [], []
    for l in range(depth):
        i = l // 2
        if l % 2 == 0:
            x1, h2, top_e, gates, counts = _even_layer(l, alpha, meta, x, mod, even)
        else:
            q, k, v, k32, v32 = _qkv_proj(l, meta, x, mod, w_qkv_bf, t_ctx)
            new_k.append(k32.reshape(n_ctx_seq, seq, n_heads, head_dim))
            new_v.append(v32.reshape(n_ctx_seq, seq, n_heads, head_dim))
            o_ctx = _attn_ctx(q, k, v, n_ctx_seq, seq, scale)
            bias = _latent_bias(rpb[i], lat_tables[2])
            o_lat = _attn_lat(q, k, v, cache_k[:, i].reshape(n_lat, -1, d),
                              cache_v[:, i].reshape(n_lat, -1, d), bias, lat_tables,
                              t_ctx, n_lat, lat_seq, scale)
            x1, h2, top_e, gates, counts = _odd_out_proj(l, alpha, meta, x, o_ctx, o_lat, mod, odd)
        pad_end, pos = _route_tables(top_e, counts, n_experts)
        xs = _moe_dispatch(l, h2, pad_end, pos, n_tiles)
        yb = _moe_experts(l, xs, pad_end, w_gate_up, b_gate_up, w_down, b_down)
        x = _moe_combine(l, alpha, meta, x1, yb, pos, gates, mod, common["ln_g"], common["ln_b"],
                         split_at=t_ctx if l == depth - 1 else None)

    y_prompt = x[0].reshape(n_ctx_seq, seq, d)
    y_sample = x[1].reshape(n_lat, lat_seq, d)
    return (y_prompt, y_sample, jnp.stack(new_k, axis=1), jnp.stack(new_v, axis=1))
```

```python
import functools

import numpy as np
import jax
import jax.numpy as jnp
from jax import lax
from jax.experimental import pallas as pl
from jax.experimental.pallas import tpu as pltpu

F32 = jnp.float32
BF16 = jnp.bfloat16

GRID_W = 64
G_A = 8
CHUNK = 128
CONV_K = 31
WIN_ROWS = 8
WIN_COLS = 16
TOP_K = 4
SWIGLU_ALPHA = 1.702
SWIGLU_LIMIT = 7.0
LN_EPS = 1e-5
NEG_INF = -1e30

LANES = 128
SUBLANES = 8
VMEM_LIMIT = 56 * 1024 * 1024

TOKEN_TILE = 256
HALO = 16
MOE_TILE = 256
Q_ROWS = 4
K_ROWS = 12
PAD_LOGIT = -3e38


def _ln(x, g, b):
    mu = jnp.mean(x, axis=-1, keepdims=True)
    xc = x - mu
    var = jnp.mean(xc * xc, axis=-1, keepdims=True)
    return xc * lax.rsqrt(var + LN_EPS) * g + b


def _gelu(x):
    return 0.5 * x * (1.0 + jnp.tanh(0.7978845608028654 * (x + 0.044715 * (x * x * x))))


def _sigmoid(x):
    return jax.nn.sigmoid(x)


def _dot(a, b):
    return jnp.dot(a, b, preferred_element_type=F32)


def _dot_nt(a, b):
    return lax.dot_general(a, b, (((1,), (1,)), ((), ())), preferred_element_type=F32)


def _params(n_axes):
    return pltpu.CompilerParams(dimension_semantics=("arbitrary",) * n_axes,
                                vmem_limit_bytes=VMEM_LIMIT)


def _mod_body(c_ref, w_ref, b_ref, o_ref):
    c = c_ref[...]
    s = (c * _sigmoid(c)).astype(BF16)
    o_ref[...] = _dot(s, w_ref[...].astype(BF16)) + b_ref[...]


def _modulation(cvec, w_mod, b_mod):
    depth, d, n = w_mod.shape
    tn = n // 4
    return pl.pallas_call(
        _mod_body,
        grid=(depth, n // tn),
        in_specs=[pl.BlockSpec((SUBLANES, d), lambda l, j: (0, 0)),
                  pl.BlockSpec((None, d, tn), lambda l, j: (l, 0, j)),
                  pl.BlockSpec((None, 1, tn), lambda l, j: (l, 0, j))],
        out_specs=pl.BlockSpec((None, SUBLANES, tn), lambda l, j: (l, 0, j)),
        out_shape=jax.ShapeDtypeStruct((depth, SUBLANES, n), F32),
        compiler_params=_params(2),
        name="adaln_modulation",
    )(cvec, w_mod, b_mod.reshape(depth, 1, n))


def _post_mixer(alpha, x, y, mod_ref, lng_ref, lnb_ref, wr_ref, br_ref,
                x1_ref, h2_ref, te_ref, tg_ref, cnt_ref, cnt_scr):
    i = pl.program_id(0)
    x1 = _ln(alpha * x + mod_ref[2:3, :] * y, lng_ref[...], lnb_ref[...])
    x1_ref[...] = x1
    h2 = x1 * (1.0 + mod_ref[4:5, :]) + mod_ref[3:4, :]
    h2_ref[...] = h2
    logits = _dot(h2.astype(BF16), wr_ref[...]) + br_ref[...]
    n_e = cnt_scr.shape[0]
    lt = logits.T[:n_e]
    tt = lt.shape[1]
    eidx = lax.broadcasted_iota(jnp.int32, lt.shape, 0)
    vals, idxs = [], []
    for _ in range(TOP_K):
        m = jnp.max(lt, axis=0, keepdims=True)
        idx = jnp.min(jnp.where(lt == m, eidx, n_e), axis=0, keepdims=True)
        vals.append(m)
        idxs.append(idx)
        lt = jnp.where(eidx == idx, -jnp.inf, lt)
    exps = [jnp.exp(v - vals[0]) for v in vals]
    den = exps[0]
    for e in exps[1:]:
        den = den + e

    @pl.when(i == 0)
    def _():
        cnt_scr[...] = jnp.zeros(cnt_scr.shape, cnt_scr.dtype)

    onehot = jnp.zeros(lt.shape, F32)
    for k in range(TOP_K):
        onehot = onehot + (eidx == idxs[k]).astype(F32)
    row = lax.broadcasted_iota(jnp.int32, (tt, tt), 0)
    col = lax.broadcasted_iota(jnp.int32, (tt, tt), 1)
    before = _dot(onehot.astype(BF16), (row < col).astype(BF16)) + cnt_scr[:, 0:1]
    cnt = cnt_scr[...] + jnp.sum(onehot, axis=1, keepdims=True)
    cnt_scr[...] = cnt
    cnt_ref[...] = cnt.astype(jnp.int32)

    row_te = lax.broadcasted_iota(jnp.int32, te_ref.shape, 0)
    row_tg = lax.broadcasted_iota(jnp.int32, (LANES, tt), 0)
    te = jnp.zeros(te_ref.shape, jnp.int32)
    tg = jnp.zeros((LANES, tt), F32)
    for k in range(TOP_K):
        rank = jnp.sum(jnp.where(eidx == idxs[k], before, 0.0), axis=0, keepdims=True)
        te = jnp.where(row_te == k, idxs[k], te)
        te = jnp.where(row_te == TOP_K + k, rank.astype(jnp.int32), te)
        tg = jnp.where(row_tg == k, exps[k] / den, tg)
    te_ref[...] = te
    tg_ref[...] = tg.T


def _expert_rows(n_experts):
    return -(-n_experts // SUBLANES) * SUBLANES


def _epilogue_specs(l, d, n_experts):
    in_specs = [pl.BlockSpec((None, None, 1, d), lambda i, s, p, n: (l, 0, 0, 0)),
                pl.BlockSpec((None, None, 1, d), lambda i, s, p, n: (l, 0, 0, 0)),
                pl.BlockSpec((None, d, LANES), lambda i, s, p, n: (l, 0, 0)),
                pl.BlockSpec((None, 1, LANES), lambda i, s, p, n: (l, 0, 0))]
    out_specs = [pl.BlockSpec((TOKEN_TILE, d), lambda i, s, p, n: (i, 0)),
                 pl.BlockSpec((TOKEN_TILE, d), lambda i, s, p, n: (i, 0)),
                 pl.BlockSpec((2 * TOP_K, TOKEN_TILE), lambda i, s, p, n: (0, i)),
                 pl.BlockSpec((TOKEN_TILE, LANES), lambda i, s, p, n: (i, 0)),
                 pl.BlockSpec((_expert_rows(n_experts), LANES), lambda i, s, p, n: (0, 0))]
    return in_specs, out_specs


def _epilogue_out_shapes(t, d, n_experts):
    return [jax.ShapeDtypeStruct((t, d), F32), jax.ShapeDtypeStruct((t, d), F32),
            jax.ShapeDtypeStruct((2 * TOP_K, t), jnp.int32), jax.ShapeDtypeStruct((t, LANES), F32),
            jax.ShapeDtypeStruct((_expert_rows(n_experts), LANES), jnp.int32)]


def _epilogue_scratch(n_experts):
    return [pltpu.VMEM((_expert_rows(n_experts), LANES), F32)]


def _even_body(alpha, seg_ref, prev_ref, next_ref,
               x_ref, xp_ref, xn_ref, mod_ref, win_ref, sg_ref, sb_ref, wsp_ref, bsp_ref,
               cw_ref, cb_ref, cg_ref, cbb_ref, wout_ref, lng_ref, lnb_ref, wr_ref, br_ref,
               x1_ref, h2_ref, te_ref, tg_ref, cnt_ref, gl_scr, cnt_scr):
    i = pl.program_id(0)
    tt = x_ref.shape[0]
    ca = sg_ref.shape[-1]
    cb2 = 2 * ca
    x = x_ref[...]
    sc = 1.0 + mod_ref[1:2, :]
    sh = mod_ref[0:1, :]
    z = _dot((x * sc + sh).astype(BF16), win_ref[...])

    u = _gelu(z[:, :ca])
    v = _ln(_gelu(z[:, ca:cb2]), sg_ref[...], sb_ref[...]).astype(BF16)
    half = lax.broadcasted_iota(jnp.int32, (CHUNK, LANES), 1) < (LANES // 2)
    chunks = []
    for ck in range(tt // CHUNK):
        cols = []
        for j in range(ca // LANES):
            vblk = v[ck * CHUNK:(ck + 1) * CHUNK, j * LANES:(j + 1) * LANES]
            cols.append(jnp.where(half, _dot(wsp_ref[2 * j], vblk), _dot(wsp_ref[2 * j + 1], vblk)))
        chunks.append(jnp.concatenate(cols, axis=1) + bsp_ref[...])
    y_a = u * jnp.concatenate(chunks, axis=0)

    def glu_rows(xh_ref):
        zh = _dot((xh_ref[...] * sc + sh).astype(BF16), win_ref[:, cb2:])
        return zh[:, :ca] * _sigmoid(zh[:, ca:])

    gl_scr[0:HALO, :] = jnp.where(prev_ref[i] > 0, glu_rows(xp_ref), 0.0)
    gl_scr[HALO:HALO + tt, :] = z[:, cb2:cb2 + ca] * _sigmoid(z[:, cb2 + ca:])
    gl_scr[HALO + tt:, :] = jnp.where(next_ref[i] > 0, glu_rows(xn_ref), 0.0)
    off = HALO - CONV_K // 2
    g_ext = gl_scr[...]
    n_ext = g_ext.shape[0]
    dc = None
    for res in range(SUBLANES):
        taps = [k for k in range(CONV_K) if (off + k) % SUBLANES == res]
        if not taps:
            continue
        shifted = g_ext if res == 0 else pltpu.roll(g_ext, n_ext - res, axis=0)
        for k in taps:
            q = (off + k) // SUBLANES * SUBLANES
            term = shifted[q:q + tt, :] * cw_ref[k:k + 1, :]
            dc = term if dc is None else dc + term
    yb = _ln(dc + cb_ref[...], cg_ref[...], cbb_ref[...])
    y_b = yb * _sigmoid(yb)

    y = _dot(jnp.concatenate([y_a, y_b], axis=1).astype(BF16), wout_ref[...])
    _post_mixer(alpha, x, y, mod_ref, lng_ref, lnb_ref, wr_ref, br_ref,
                x1_ref, h2_ref, te_ref, tg_ref, cnt_ref, cnt_scr)


def _even_layer(l, alpha, meta, x, mod, p):
    t, d = x.shape
    li = l // 2
    nh = TOKEN_TILE // HALO
    n_halo = t // HALO
    ca = p["sgu_g"].shape[-1]
    ep_in, ep_out = _epilogue_specs(l, d, p["n_experts"])
    const3 = lambda i, s, pv, nx: (li, 0, 0)
    in_specs = [
        pl.BlockSpec((TOKEN_TILE, d), lambda i, s, pv, nx: (i, 0)),
        pl.BlockSpec((HALO, d), lambda i, s, pv, nx: (jnp.maximum(i * nh - 1, 0), 0)),
        pl.BlockSpec((HALO, d), lambda i, s, pv, nx: (jnp.minimum((i + 1) * nh, n_halo - 1), 0)),
        pl.BlockSpec((None, None, 6, d), lambda i, s, pv, nx: (l, s[i], 0, 0)),
        pl.BlockSpec((None, d, 4 * ca), const3),
        pl.BlockSpec((None, 1, ca), const3),
        pl.BlockSpec((None, 1, ca), const3),
        pl.BlockSpec((None, G_A, CHUNK, CHUNK), lambda i, s, pv, nx: (li, 0, 0, 0)),
        pl.BlockSpec((None, CHUNK, ca), const3),
        pl.BlockSpec((None, CONV_K, ca), const3),
        pl.BlockSpec((None, 1, ca), const3),
        pl.BlockSpec((None, 1, ca), const3),
        pl.BlockSpec((None, 1, ca), const3),
        pl.BlockSpec((None, 2 * ca, d), const3),
    ] + ep_in
    return pl.pallas_call(
        functools.partial(_even_body, alpha),
        grid_spec=pltpu.PrefetchScalarGridSpec(
            num_scalar_prefetch=3, grid=(t // TOKEN_TILE,),
            in_specs=in_specs, out_specs=ep_out,
            scratch_shapes=[pltpu.VMEM((TOKEN_TILE + 2 * HALO, ca), F32)]
            + _epilogue_scratch(p["n_experts"])),
        out_shape=_epilogue_out_shapes(t, d, p["n_experts"]),
        compiler_params=_params(1),
        name=f"even_mixer_{l}",
    )(meta["seg"], meta["prev"], meta["next"], x, x, x, mod,
      p["w_in"], p["sgu_g"], p["sgu_b"], p["w_sp"], p["b_sp"], p["conv_w"], p["conv_b"],
      p["cln_g"], p["cln_b"], p["w_out_ab"], p["ln_g"], p["ln_b"], p["w_router"], p["b_router"])


def _qkv_body(n_ctx_tiles, seg_ref, x_ref, mod_ref, w_ref, q_ref, k_ref, v_ref, k32_ref, v32_ref):
    _qkv_from(x_ref[...], mod_ref, w_ref, n_ctx_tiles, q_ref, k_ref, v_ref, k32_ref, v32_ref)


def _qkv_from(x, mod_ref, w_ref, n_ctx_tiles, q_ref, k_ref, v_ref, k32_ref, v32_ref):
    d = x.shape[1]
    h = (x * (1.0 + mod_ref[1:2, :]) + mod_ref[0:1, :]).astype(BF16)
    qkv = _dot(h, w_ref[...])
    q_ref[...] = qkv[:, :d].astype(BF16)
    k = qkv[:, d:2 * d]
    v = qkv[:, 2 * d:]
    k_ref[...] = k.astype(BF16)
    v_ref[...] = v.astype(BF16)

    @pl.when(pl.program_id(0) < n_ctx_tiles)
    def _():
        k32_ref[...] = k
        v32_ref[...] = v


def _qkv_proj(l, meta, x, mod, w_qkv, t_ctx):
    t, d = x.shape
    li = l // 2
    n_ctx_tiles = t_ctx // TOKEN_TILE
    tile = pl.BlockSpec((TOKEN_TILE, d), lambda i, s: (i, 0))
    ctx_tile = pl.BlockSpec((TOKEN_TILE, d), lambda i, s: (jnp.minimum(i, n_ctx_tiles - 1), 0))
    return pl.pallas_call(
        functools.partial(_qkv_body, n_ctx_tiles),
        grid_spec=pltpu.PrefetchScalarGridSpec(
            num_scalar_prefetch=1, grid=(t // TOKEN_TILE,),
            in_specs=[tile,
                      pl.BlockSpec((None, None, 6, d), lambda i, s: (l, s[i], 0, 0)),
                      pl.BlockSpec((None, d, 3 * d), lambda i, s: (li, 0, 0))],
            out_specs=[tile] * 3 + [ctx_tile] * 2),
        out_shape=[jax.ShapeDtypeStruct((t, d), BF16)] * 3 + [jax.ShapeDtypeStruct((t_ctx, d), F32)] * 2,
        compiler_params=_params(1),
        name=f"qkv_proj_{l}",
    )(meta["seg"], x, mod, w_qkv)


def _head_pair_attention(q2, k_parts, v_parts, bias_parts, scale):
    lane = lax.broadcasted_iota(jnp.int32, q2.shape, 1)
    outs = []
    for hh in range(2):
        qm = jnp.where((lane >= hh * (LANES // 2)) & (lane < (hh + 1) * (LANES // 2)), q2 * scale,
                       jnp.zeros_like(q2))
        ss = []
        for j, kp in enumerate(k_parts):
            s = _dot_nt(qm, kp)
            if bias_parts[hh][j] is not None:
                s = s + bias_parts[hh][j]
            ss.append(s)
        m = ss[0].max(axis=-1, keepdims=True)
        for s in ss[1:]:
            m = jnp.maximum(m, s.max(axis=-1, keepdims=True))
        den = None
        o = None
        for s, vp in zip(ss, v_parts):
            e = jnp.exp(s - m)
            es = e.sum(axis=-1, keepdims=True)
            den = es if den is None else den + es
            pv = _dot(e.astype(BF16), vp)
            o = pv if o is None else o + pv
        outs.append(o / den)
    return jnp.where(lane < LANES // 2, outs[0], outs[1])


def _attn_ctx_body(scale, q_ref, k_ref, v_ref, o_ref):
    d = q_ref.shape[1]
    for pr in range(d // LANES):
        sl = slice(pr * LANES, (pr + 1) * LANES)
        o = _head_pair_attention(q_ref[:, sl], [k_ref[:, sl]], [v_ref[:, sl]],
                                 [[None], [None]], scale)
        o_ref[:, sl] = o.astype(o_ref.dtype)


def _attn_ctx(q, k, v, n_seq, seq, scale):
    d = q.shape[1]
    blk = pl.BlockSpec((seq, d), lambda b: (b, 0))
    return pl.pallas_call(
        functools.partial(_attn_ctx_body, scale),
        grid=(n_seq,),
        in_specs=[blk, blk, blk],
        out_specs=blk,
        out_shape=jax.ShapeDtypeStruct((n_seq * seq, d), BF16),
        compiler_params=_params(1),
        name="attn_ctx",
    )(q, k, v)


def _attn_lat_body(scale, cls_ref, kb_ref, q_ref, k0_ref, k1_ref, k2_ref, v0_ref, v1_ref, v2_ref,
                   ck_ref, cv_ref, bias_ref, o_ref):
    d = q_ref.shape[1]
    for pr in range(d // LANES):
        sl = slice(pr * LANES, (pr + 1) * LANES)
        k_loc = jnp.concatenate([k0_ref[:, sl], k1_ref[:, sl], k2_ref[:, sl]], axis=0)
        v_loc = jnp.concatenate([v0_ref[:, sl], v1_ref[:, sl], v2_ref[:, sl]], axis=0)
        ck = ck_ref[:, sl].astype(BF16)
        cv = cv_ref[:, sl].astype(BF16)
        o = _head_pair_attention(q_ref[:, sl], [k_loc, ck], [v_loc, cv],
                                 [[bias_ref[2 * pr], None], [bias_ref[2 * pr + 1], None]], scale)
        o_ref[:, sl] = o.astype(o_ref.dtype)


def _latent_window_tables(rows_n):
    wr = min(WIN_ROWS, rows_n)
    n_rt = rows_n // Q_ROWS
    kstart = np.clip(np.arange(n_rt) * Q_ROWS - wr // 2, 0, rows_n - K_ROWS)
    kstart = (kstart // Q_ROWS) * Q_ROWS
    patterns, cls = [], []
    for rt in range(n_rt):
        pat = np.full((Q_ROWS, K_ROWS), -1, np.int64)
        for qi in range(Q_ROWS):
            r = rt * Q_ROWS + qi
            rs = int(np.clip(r - wr // 2, 0, rows_n - wr))
            for kj in range(K_ROWS):
                kr = int(kstart[rt]) + kj
                if rs <= kr < rs + wr:
                    pat[qi, kj] = kr - r + WIN_ROWS - 1
        assert (pat >= 0).sum(axis=1).min() == wr, "key block does not cover the window"
        key = pat.tobytes()
        if key not in [p.tobytes() for p in patterns]:
            patterns.append(pat)
        cls.append([p.tobytes() for p in patterns].index(key))
    return (kstart // Q_ROWS).astype(np.int32), np.asarray(cls, np.int32), np.stack(patterns)


def _latent_bias(rpb, patterns):
    h = rpb.shape[0]
    qc = np.arange(GRID_W)[:, None]
    kc = np.arange(GRID_W)[None, :]
    qcs = np.clip(qc - WIN_COLS // 2, 0, GRID_W - WIN_COLS)
    col_ok = (kc >= qcs) & (kc < qcs + WIN_COLS)
    dc = np.clip(kc - qc + WIN_COLS - 1, 0, 2 * WIN_COLS - 2)
    onehot = (dc[None] == np.arange(2 * WIN_COLS - 1)[:, None, None]) & col_ok[None]
    cm = jnp.einsum("hrd,dqk->hrqk", rpb, jnp.asarray(onehot, F32), precision=lax.Precision.HIGHEST)
    cm = jnp.where(jnp.asarray(col_ok), cm, NEG_INF)
    cx = jnp.concatenate([cm, jnp.full((h, 1, GRID_W, GRID_W), NEG_INF, F32)], axis=1)
    cx2 = jnp.concatenate([cx, cx], axis=-1)
    idx = np.where(patterns >= 0, patterns, 2 * WIN_ROWS - 1)
    n_cls = idx.shape[0]
    assert 2 * GRID_W == LANES and K_ROWS % 2 == 0

    def body(idx_ref, cx_ref, o_ref):
        c = pl.program_id(0)
        low = lax.broadcasted_iota(jnp.int32, (GRID_W, LANES), 1) < GRID_W
        for qi in range(Q_ROWS):
            for m in range(K_ROWS // 2):
                base = (c * Q_ROWS + qi) * K_ROWS + 2 * m
                blk = jnp.where(low, cx_ref[idx_ref[base]], cx_ref[idx_ref[base + 1]])
                o_ref[qi * GRID_W:(qi + 1) * GRID_W, m * LANES:(m + 1) * LANES] = blk

    return pl.pallas_call(
        body,
        grid_spec=pltpu.PrefetchScalarGridSpec(
            num_scalar_prefetch=1, grid=(n_cls, h),
            in_specs=[pl.BlockSpec((None, cx2.shape[1], GRID_W, LANES), lambda c, j, ix: (j, 0, 0, 0))],
            out_specs=pl.BlockSpec((None, None, Q_ROWS * GRID_W, K_ROWS * GRID_W),
                                   lambda c, j, ix: (c, j, 0, 0))),
        out_shape=jax.ShapeDtypeStruct((n_cls, h, Q_ROWS * GRID_W, K_ROWS * GRID_W), F32),
        compiler_params=_params(2),
        name="latent_bias_table",
    )(jnp.asarray(idx.reshape(-1), jnp.int32), cx2)


def _attn_lat(q, k, v, ck, cv, bias, tables, tok0, n_batch, n_tok, scale):
    d = q.shape[1]
    kblk, cls, _ = tables
    n_rt = kblk.shape[0]
    qt = Q_ROWS * GRID_W
    base = tok0 // qt
    per_b = n_tok // qt
    h = bias.shape[1]
    lc = ck.shape[1]

    def kv_spec(j):
        return pl.BlockSpec((qt, d), lambda b, r, c, kb: (base + b * per_b + kb[r] + j, 0))

    return pl.pallas_call(
        functools.partial(_attn_lat_body, scale),
        grid_spec=pltpu.PrefetchScalarGridSpec(
            num_scalar_prefetch=2, grid=(n_batch, n_rt),
            in_specs=[pl.BlockSpec((qt, d), lambda b, r, c, kb: (base + b * per_b + r, 0)),
                      kv_spec(0), kv_spec(1), kv_spec(2), kv_spec(0), kv_spec(1), kv_spec(2),
                      pl.BlockSpec((None, lc, d), lambda b, r, c, kb: (b, 0, 0)),
                      pl.BlockSpec((None, lc, d), lambda b, r, c, kb: (b, 0, 0)),
                      pl.BlockSpec((None, h, qt, K_ROWS * GRID_W), lambda b, r, c, kb: (c[r], 0, 0, 0))],
            out_specs=pl.BlockSpec((qt, d), lambda b, r, c, kb: (b * per_b + r, 0))),
        out_shape=jax.ShapeDtypeStruct((n_batch * n_tok, d), BF16),
        compiler_params=_params(2),
        name="attn_latent",
    )(jnp.asarray(cls), jnp.asarray(kblk), q, k, k, k, v, v, v, ck, cv, bias)


def _proj_body(alpha, n_ctx_tiles, seg_ref, prev_ref, next_ref, x_ref, oc_ref, ol_ref, mod_ref, w_ref,
               lng_ref, lnb_ref, wr_ref, br_ref, x1_ref, h2_ref, te_ref, tg_ref, cnt_ref, cnt_scr):
    o = jnp.where(pl.program_id(0) < n_ctx_tiles, oc_ref[...], ol_ref[...])
    y = _dot(o, w_ref[...])
    _post_mixer(alpha, x_ref[...], y, mod_ref, lng_ref, lnb_ref, wr_ref, br_ref,
                x1_ref, h2_ref, te_ref, tg_ref, cnt_ref, cnt_scr)


def _odd_out_proj(l, alpha, meta, x, o_ctx, o_lat, mod, p):
    t, d = x.shape
    li = l // 2
    nc = o_ctx.shape[0] // TOKEN_TILE
    ep_in, ep_out = _epilogue_specs(l, d, p["n_experts"])
    tile = pl.BlockSpec((TOKEN_TILE, d), lambda i, s, pv, nx: (i, 0))
    return pl.pallas_call(
        functools.partial(_proj_body, alpha, nc),
        grid_spec=pltpu.PrefetchScalarGridSpec(
            num_scalar_prefetch=3, grid=(t // TOKEN_TILE,),
            in_specs=[tile,
                      pl.BlockSpec((TOKEN_TILE, d), lambda i, s, pv, nx: (jnp.minimum(i, nc - 1), 0)),
                      pl.BlockSpec((TOKEN_TILE, d), lambda i, s, pv, nx: (jnp.maximum(i - nc, 0), 0)),
                      pl.BlockSpec((None, None, 6, d), lambda i, s, pv, nx: (l, s[i], 0, 0)),
                      pl.BlockSpec((None, d, d), lambda i, s, pv, nx: (li, 0, 0))] + ep_in,
            out_specs=ep_out, scratch_shapes=_epilogue_scratch(p["n_experts"])),
        out_shape=_epilogue_out_shapes(t, d, p["n_experts"]),
        compiler_params=_params(1),
        name=f"attn_out_proj_{l}",
    )(meta["seg"], meta["prev"], meta["next"], x, o_ctx, o_lat, mod, p["w_out_c"],
      p["ln_g"], p["ln_b"], p["w_router"], p["b_router"])


def _route_tables(te, counts, n_experts):
    t = te.shape[1]
    counts = counts[:n_experts, 0]
    padded = (counts + MOE_TILE - 1) // MOE_TILE * MOE_TILE
    pad_end = jnp.cumsum(padded).astype(jnp.int32)
    pad_start = pad_end - padded
    experts, ranks = te[:TOP_K], te[TOP_K:]
    eid = jnp.arange(n_experts, dtype=jnp.int32)
    sel = experts[None] == eid[:, None, None]
    pos = ranks + jnp.sum(jnp.where(sel, pad_start[:, None, None], 0), axis=0)
    pos = jnp.transpose(pos.reshape(TOP_K, t // TOKEN_TILE, TOKEN_TILE), (1, 2, 0))
    pos = pos.astype(jnp.int32).reshape(t // TOKEN_TILE, 1, TOKEN_TILE * TOP_K)
    return pad_end, pos


def _dispatch_body(pend_ref, pos_ref, h2_ref, xs_hbm, stage, sem, zsem):
    i = pl.program_id(0)
    n_steps = pl.num_programs(0)
    slot = i % 2
    tt = h2_ref.shape[0]
    n_e = pend_ref.shape[0]

    def scatter_wait(s):
        for _ in range(TOP_K):
            pltpu.make_async_copy(stage.at[s], xs_hbm.at[pl.ds(0, tt)], sem.at[s]).wait()

    @pl.when(i == 0)
    def _():
        stage[0] = jnp.zeros(stage.shape[1:], stage.dtype)
        n_tiles = xs_hbm.shape[0] // tt
        n_used = pend_ref[n_e - 1] // tt
        for phase in range(2):
            for e in range(n_e):
                lo = pend_ref[e - 1] if e else 0
                for cond, row0 in ((pend_ref[e] > lo, pend_ref[e] - tt),
                                   (n_used + e < n_tiles, (n_used + e) * tt)):
                    @pl.when(cond)
                    def _():
                        cp = pltpu.make_async_copy(
                            stage.at[0], xs_hbm.at[pl.ds(pl.multiple_of(row0, tt), tt)], zsem)
                        if phase == 0:
                            cp.start()
                        else:
                            cp.wait()

    @pl.when(i >= 2)
    def _():
        scatter_wait(slot)

    stage[slot] = h2_ref[...].reshape(stage.shape[1:])

    def body(r, c):
        for k in range(TOP_K):
            pltpu.make_async_copy(stage.at[slot, r], xs_hbm.at[pos_ref[0, r * TOP_K + k]],
                                  sem.at[slot]).start(priority=k % 2)
        return c
    lax.fori_loop(0, tt, body, 0, unroll=8)

    @pl.when(i == n_steps - 1)
    def _():
        @pl.when(n_steps >= 2)
        def _():
            scatter_wait(1 - slot)
        scatter_wait(slot)


def _moe_dispatch(l, h2, pad_end, pos, n_tiles):
    t, d = h2.shape
    assert TOKEN_TILE == MOE_TILE
    return pl.pallas_call(
        _dispatch_body,
        grid_spec=pltpu.PrefetchScalarGridSpec(
            num_scalar_prefetch=1, grid=(t // TOKEN_TILE,),
            in_specs=[pl.BlockSpec((None, 1, TOKEN_TILE * TOP_K), lambda i, pe: (i, 0, 0),
                                   memory_space=pltpu.SMEM),
                      pl.BlockSpec((TOKEN_TILE, d), lambda i, pe: (i, 0))],
            out_specs=pl.BlockSpec(memory_space=pl.ANY),
            scratch_shapes=[pltpu.VMEM((2, TOKEN_TILE, SUBLANES, LANES), F32),
                            pltpu.SemaphoreType.DMA((2,)),
                            pltpu.SemaphoreType.DMA]),
        out_shape=jax.ShapeDtypeStruct((n_tiles * MOE_TILE, SUBLANES, LANES), F32),
        compiler_params=_params(1),
        name=f"moe_dispatch_{l}",
    )(pad_end, pos, h2)


def _moe_body(l, pend_ref, bgu_ref, bd_ref, xs_hbm, wgu_hbm, wd_hbm, yb_hbm,
              xbuf, ybuf, wgu_st, wd_st, wgu_bf, wd_bf, xsem, ysem, wsem):
    e = pl.program_id(0)
    n_e = pl.num_programs(0)
    tm = xbuf.shape[1]
    de = wd_bf.shape[0]
    par = e % 2
    g_lo = jnp.where(e == 0, 0, pend_ref[jnp.maximum(e - 1, 0)]) // tm
    g_hi = pend_ref[e] // tm
    n_used = pend_ref[n_e - 1] // tm
    n_tiles = yb_hbm.shape[0] // tm

    def weight_copies(ex, p):
        return (pltpu.make_async_copy(wgu_hbm.at[l, ex], wgu_st.at[p], wsem.at[p, 0]),
                pltpu.make_async_copy(wd_hbm.at[l, ex], wd_st.at[p], wsem.at[p, 1]))

    def x_copy(g, s):
        return pltpu.make_async_copy(xs_hbm.at[pl.ds(pl.multiple_of(g * tm, tm), tm)], xbuf.at[s],
                                     xsem.at[s])

    def y_copy(g, s):
        return pltpu.make_async_copy(ybuf.at[s], yb_hbm.at[pl.ds(pl.multiple_of(g * tm, tm), tm)],
                                     ysem.at[s])

    @pl.when(e == 0)
    def _():
        for cp in weight_copies(0, 0):
            cp.start(priority=1)

        @pl.when(n_used > 0)
        def _():
            x_copy(0, 0).start()

    @pl.when(e + 1 < n_e)
    def _():
        for cp in weight_copies(e + 1, 1 - par):
            cp.start(priority=1)

    for cp in weight_copies(e, par):
        cp.wait()

    @pl.when(g_hi > g_lo)
    def _():
        wgu_bf[...] = wgu_st[par].astype(BF16)
        wd_bf[...] = wd_st[par].astype(BF16)

    def tile_body(g, carry):
        s = g % 2
        x_copy(g, s).wait()

        @pl.when(g + 1 < n_used)
        def _():
            x_copy(g + 1, 1 - s).start()

        x = xbuf[s].reshape(tm, wgu_bf.shape[0])
        hgu = _dot(x.astype(BF16), wgu_bf[...]) + bgu_ref[...]
        x_glu = jnp.minimum(hgu[:, :de], SWIGLU_LIMIT)
        x_lin = jnp.clip(hgu[:, de:], -SWIGLU_LIMIT, SWIGLU_LIMIT)
        act = x_glu * _sigmoid(SWIGLU_ALPHA * x_glu) * (x_lin + 1.0)
        y = _dot(act.astype(BF16), wd_bf[...]) + bd_ref[...]

        @pl.when(g >= 2)
        def _():
            y_copy(g - 2, s).wait()

        ybuf[s] = y.reshape(ybuf.shape[1:])
        y_copy(g, s).start()
        return carry

    lax.fori_loop(g_lo, g_hi, tile_body, 0)

    @pl.when(e == n_e - 1)
    def _():
        @pl.when(n_used >= 2)
        def _():
            y_copy(n_used - 2, n_used % 2).wait()

        @pl.when(n_used >= 1)
        def _():
            y_copy(n_used - 1, (n_used + 1) % 2).wait()

        ybuf[0] = jnp.zeros(ybuf.shape[1:], ybuf.dtype)
        for phase in range(2):
            for m in range(n_e):
                @pl.when(n_used + m < n_tiles)
                def _():
                    cp = y_copy(n_used + m, 0)
                    if phase == 0:
                        cp.start()
                    else:
                        cp.wait()


def _moe_experts(l, xs, pad_end, w_gu, b_gu, w_d, b_d):
    n_e, de, d = w_d.shape[1], w_d.shape[2], w_d.shape[3]
    row_tile = (MOE_TILE,) + xs.shape[1:]
    return pl.pallas_call(
        functools.partial(_moe_body, l),
        grid_spec=pltpu.PrefetchScalarGridSpec(
            num_scalar_prefetch=1, grid=(n_e,),
            in_specs=[pl.BlockSpec((None, None, 1, 2 * de), lambda e, pe: (l, e, 0, 0)),
                      pl.BlockSpec((None, None, 1, d), lambda e, pe: (l, e, 0, 0)),
                      pl.BlockSpec(memory_space=pl.ANY),
                      pl.BlockSpec(memory_space=pl.ANY),
                      pl.BlockSpec(memory_space=pl.ANY)],
            out_specs=pl.BlockSpec(memory_space=pl.ANY),
            scratch_shapes=[pltpu.VMEM((2,) + row_tile, F32), pltpu.VMEM((2,) + row_tile, F32),
                            pltpu.VMEM((2, d, 2 * de), F32), pltpu.VMEM((2, de, d), F32),
                            pltpu.VMEM((d, 2 * de), BF16), pltpu.VMEM((de, d), BF16),
                            pltpu.SemaphoreType.DMA((2,)), pltpu.SemaphoreType.DMA((2,)),
                            pltpu.SemaphoreType.DMA((2, 2))]),
        out_shape=jax.ShapeDtypeStruct(xs.shape, F32),
        compiler_params=_params(1),
        name=f"moe_experts_{l}",
    )(pad_end, b_gu.reshape(b_gu.shape[0], n_e, 1, 2 * de), b_d.reshape(b_d.shape[0], n_e, 1, d),
      xs, w_gu, w_d)


def _combine_rows(alpha, pos_ref, posn_ref, x1_ref, g_ref, mod_ref, lng_ref, lnb_ref, yb_hbm, buf, sem):
    i = pl.program_id(0)
    n_steps = pl.num_programs(0)
    slot = i % 2
    tt = x1_ref.shape[0]

    def gather_start(idx_ref, s):
        def body(r, c):
            for k in range(TOP_K):
                pltpu.make_async_copy(yb_hbm.at[idx_ref[0, r * TOP_K + k]], buf.at[s, k, r],
                                      sem.at[s]).start(priority=k % 2)
            return c
        lax.fori_loop(0, tt, body, 0, unroll=8)

    @pl.when(i == 0)
    def _():
        gather_start(pos_ref, 0)

    @pl.when(i + 1 < n_steps)
    def _():
        gather_start(posn_ref, 1 - slot)

    for k in range(TOP_K):
        pltpu.make_async_copy(yb_hbm.at[pl.ds(0, tt)], buf.at[slot, k], sem.at[slot]).wait()
    g = g_ref[...]
    y = None
    for k in range(TOP_K):
        part = g[:, k:k + 1] * buf[slot, k].reshape(x1_ref.shape)
        y = part if y is None else y + part
    return _ln(alpha * x1_ref[...] + mod_ref[5:6, :] * y, lng_ref[...], lnb_ref[...])


def _combine_body(alpha, n_ctx_tiles, seg_ref, pos_ref, posn_ref, x1_ref, g_ref, mod_ref, lng_ref,
                  lnb_ref, yb_hbm, o_ref, *rest):
    o2_ref = rest[0] if len(rest) == 3 else None
    buf, sem = rest[-2:]
    i = pl.program_id(0)
    out = _combine_rows(alpha, pos_ref, posn_ref, x1_ref, g_ref, mod_ref, lng_ref, lnb_ref, yb_hbm,
                        buf, sem)
    if o2_ref is None:
        o_ref[...] = out
    else:
        @pl.when(i < n_ctx_tiles)
        def _():
            o_ref[...] = out

        @pl.when(i >= n_ctx_tiles)
        def _():
            o2_ref[...] = out


def _combine_qkv_body(alpha, n_ctx_tiles, seg_ref, pos_ref, posn_ref, x1_ref, g_ref, mod_ref, lng_ref,
                      lnb_ref, modn_ref, w_ref, yb_hbm, o_ref, q_ref, k_ref, v_ref, k32_ref, v32_ref,
                      buf, sem):
    out = _combine_rows(alpha, pos_ref, posn_ref, x1_ref, g_ref, mod_ref, lng_ref, lnb_ref, yb_hbm,
                        buf, sem)
    o_ref[...] = out
    _qkv_from(out, modn_ref, w_ref, n_ctx_tiles, q_ref, k_ref, v_ref, k32_ref, v32_ref)


def _moe_combine(l, alpha, meta, x1, yb, pos, gates, mod, ln_g, ln_b, split_at=None):
    t, d = x1.shape
    nt = t // TOKEN_TILE
    tile = pl.BlockSpec((TOKEN_TILE, d), lambda i, s: (i, 0))
    pos_blk = lambda f: pl.BlockSpec((None, 1, TOKEN_TILE * TOP_K), f, memory_space=pltpu.SMEM)
    if split_at is None:
        nc, out_specs, out_shape = 0, tile, jax.ShapeDtypeStruct((t, d), F32)
    else:
        nc = split_at // TOKEN_TILE
        out_specs = [pl.BlockSpec((TOKEN_TILE, d), lambda i, s: (jnp.minimum(i, nc - 1), 0)),
                     pl.BlockSpec((TOKEN_TILE, d), lambda i, s: (jnp.maximum(i - nc, 0), 0))]
        out_shape = [jax.ShapeDtypeStruct((split_at, d), F32),
                     jax.ShapeDtypeStruct((t - split_at, d), F32)]
    return pl.pallas_call(
        functools.partial(_combine_body, alpha, nc),
        grid_spec=pltpu.PrefetchScalarGridSpec(
            num_scalar_prefetch=1, grid=(nt,),
            in_specs=[pos_blk(lambda i, s: (i, 0, 0)),
                      pos_blk(lambda i, s: (jnp.minimum(i + 1, nt - 1), 0, 0)),
                      tile,
                      pl.BlockSpec((TOKEN_TILE, LANES), lambda i, s: (i, 0)),
                      pl.BlockSpec((None, None, 6, d), lambda i, s: (l, s[i], 0, 0)),
                      pl.BlockSpec((None, None, 1, d), lambda i, s: (l, 1, 0, 0)),
                      pl.BlockSpec((None, None, 1, d), lambda i, s: (l, 1, 0, 0)),
                      pl.BlockSpec(memory_space=pl.ANY)],
            out_specs=out_specs,
            scratch_shapes=[pltpu.VMEM((2, TOP_K, TOKEN_TILE) + yb.shape[1:], F32),
                            pltpu.SemaphoreType.DMA((2,))]),
        out_shape=out_shape,
        compiler_params=_params(1),
        name=f"moe_combine_{l}",
    )(meta["seg"], pos, pos, x1, gates, mod, ln_g, ln_b, yb)


def _moe_combine_qkv(l, alpha, meta, x1, yb, pos, gates, mod, ln_g, ln_b, w_qkv, t_ctx):
    t, d = x1.shape
    nt = t // TOKEN_TILE
    nc = t_ctx // TOKEN_TILE
    li = (l + 1) // 2
    tile = pl.BlockSpec((TOKEN_TILE, d), lambda i, s: (i, 0))
    ctx_tile = pl.BlockSpec((TOKEN_TILE, d), lambda i, s: (jnp.minimum(i, nc - 1), 0))
    pos_blk = lambda f: pl.BlockSpec((None, 1, TOKEN_TILE * TOP_K), f, memory_space=pltpu.SMEM)
    return pl.pallas_call(
        functools.partial(_combine_qkv_body, alpha, nc),
        grid_spec=pltpu.PrefetchScalarGridSpec(
            num_scalar_prefetch=1, grid=(nt,),
            in_specs=[pos_blk(lambda i, s: (i, 0, 0)),
                      pos_blk(lambda i, s: (jnp.minimum(i + 1, nt - 1), 0, 0)),
                      tile,
                      pl.BlockSpec((TOKEN_TILE, LANES), lambda i, s: (i, 0)),
                      pl.BlockSpec((None, None, 6, d), lambda i, s: (l, s[i], 0, 0)),
                      pl.BlockSpec((None, None, 1, d), lambda i, s: (l, 1, 0, 0)),
                      pl.BlockSpec((None, None, 1, d), lambda i, s: (l, 1, 0, 0)),
                      pl.BlockSpec((None, None, 6, d), lambda i, s: (l + 1, s[i], 0, 0)),
                      pl.BlockSpec((None, d, 3 * d), lambda i, s: (li, 0, 0)),
                      pl.BlockSpec(memory_space=pl.ANY)],
            out_specs=[tile] * 4 + [ctx_tile] * 2,
            scratch_shapes=[pltpu.VMEM((2, TOP_K, TOKEN_TILE) + yb.shape[1:], F32),
                            pltpu.SemaphoreType.DMA((2,))]),
        out_shape=[jax.ShapeDtypeStruct((t, d), F32)] + [jax.ShapeDtypeStruct((t, d), BF16)] * 3
        + [jax.ShapeDtypeStruct((t_ctx, d), F32)] * 2,
        compiler_params=_params(1),
        name=f"moe_combine_qkv_{l}",
    )(meta["seg"], pos, pos, x1, gates, mod, ln_g, ln_b, mod, w_qkv, yb)


def _token_meta(n_ctx_seq, seq, n_lat, lat_seq):
    seg, prev, nxt = [], [], []
    for n_seq, length, seg_of in ((n_ctx_seq, seq, lambda b: 0), (n_lat, lat_seq, lambda b: 1 + b)):
        per = length // TOKEN_TILE
        for b in range(n_seq):
            for j in range(per):
                seg.append(seg_of(b))
                prev.append(int(j > 0))
                nxt.append(int(j < per - 1))
    as_i32 = lambda a: jnp.asarray(np.asarray(a, np.int32))
    return {"seg": as_i32(seg), "prev": as_i32(prev), "next": as_i32(nxt)}


def kernel(x_prompt, x_sample, c, cache_k, cache_v, c_ctx, w_mod, b_mod, ln_g, ln_b, w_in_ab, sgu_ln_g, sgu_ln_b, w_spatial, b_spatial, conv_w, conv_b, conv_ln_g, conv_ln_b, w_out_ab, w_qkv, rpb, w_out_c, w_router, b_router, w_gate_up, b_gate_up, w_down, b_down):
    n_ctx_seq, seq, d = x_prompt.shape
    n_lat, lat_seq, _ = x_sample.shape
    depth = w_mod.shape[0]
    n_heads, head_dim = cache_k.shape[3], cache_k.shape[4]
    n_experts = w_router.shape[-1]
    ca = sgu_ln_g.shape[-1]
    n_even, n_odd = w_in_ab.shape[0], w_qkv.shape[0]
    t_ctx, t_lat = n_ctx_seq * seq, n_lat * lat_seq
    t = t_ctx + t_lat
    rows_n = lat_seq // GRID_W
    assert seq % TOKEN_TILE == 0 and lat_seq % TOKEN_TILE == 0 and 1 + n_lat <= SUBLANES
    assert TOKEN_TILE % CHUNK == 0 and HALO >= CONV_K // 2 and ca == w_out_ab.shape[1] // 2
    assert rows_n % Q_ROWS == 0 and rows_n >= K_ROWS and t_ctx % (Q_ROWS * GRID_W) == 0
    assert n_heads * head_dim == d and 2 * head_dim == LANES and n_experts <= LANES
    assert (t * TOP_K) % MOE_TILE == 0
    assert d == SUBLANES * LANES, "MoE rows are moved as one (SUBLANES, LANES) f32 tile each"
    alpha = float((2 * depth) ** 0.25)
    scale = float(head_dim ** -0.5)
    assert np.frexp(scale)[0] == 0.5, "the attention scale is folded into the bf16 queries"
    meta = _token_meta(n_ctx_seq, seq, n_lat, lat_seq)

    x = jnp.concatenate([x_prompt.reshape(t_ctx, d), x_sample.reshape(t_lat, d)], axis=0)
    cvec = jnp.zeros((SUBLANES, d), F32).at[0].set(c_ctx).at[1:1 + n_lat].set(c)
    mod = _modulation(cvec, w_mod, b_mod).reshape(depth, SUBLANES, 6, d)

    pad_e = LANES - n_experts
    common = {
        "n_experts": n_experts,
        "ln_g": ln_g.reshape(depth, 2, 1, d), "ln_b": ln_b.reshape(depth, 2, 1, d),
        "w_router": jnp.pad(w_router, ((0, 0), (0, 0), (0, pad_e))).astype(BF16),
        "b_router": jnp.pad(b_router, ((0, 0), (0, pad_e)), constant_values=PAD_LOGIT).reshape(depth, 1, LANES),
    }
    even = dict(common)
    even.update({
        "w_in": w_in_ab.astype(BF16), "sgu_g": sgu_ln_g.reshape(n_even, 1, ca),
        "sgu_b": sgu_ln_b.reshape(n_even, 1, ca), "w_sp": w_spatial.astype(BF16),
        "b_sp": jnp.repeat(jnp.transpose(b_spatial, (0, 2, 1)), ca // G_A, axis=2),
        "conv_w": conv_w, "conv_b": conv_b.reshape(n_even, 1, ca),
        "cln_g": conv_ln_g.reshape(n_even, 1, ca), "cln_b": conv_ln_b.reshape(n_even, 1, ca),
        "w_out_ab": w_out_ab.astype(BF16)})
    odd = dict(common)
    odd["w_out_c"] = w_out_c.astype(BF16)
    w_qkv_bf = w_qkv.astype(BF16)
    lat_tables = _latent_window_tables(rows_n)
    n_tiles = t * TOP_K // MOE_TILE + n_experts

    new_k, new_v = [], []
    qkv = None
    for l in range(depth):
        i = l // 2
        if l % 2 == 0:
            x1, h2, top_e, gates, counts = _even_layer(l, alpha, meta, x, mod, even)
        else:
            if qkv is None:
                qkv = _qkv_proj(l, meta, x, mod, w_qkv_bf, t_ctx)
            q, k, v, k32, v32 = qkv
            new_k.append(k32.reshape(n_ctx_seq, seq, n_heads, head_dim))
            new_v.append(v32.reshape(n_ctx_seq, seq, n_heads, head_dim))
            o_ctx = _attn_ctx(q, k, v, n_ctx_seq, seq, scale)
            bias = _latent_bias(rpb[i], lat_tables[2])
            o_lat = _attn_lat(q, k, v, cache_k[:, i].reshape(n_lat, -1, d),
                              cache_v[:, i].reshape(n_lat, -1, d), bias, lat_tables,
                              t_ctx, n_lat, lat_seq, scale)
            x1, h2, top_e, gates, counts = _odd_out_proj(l, alpha, meta, x, o_ctx, o_lat, mod, odd)
        pad_end, pos = _route_tables(top_e, counts, n_experts)
        xs = _moe_dispatch(l, h2, pad_end, pos, n_tiles)
        yb = _moe_experts(l, xs, pad_end, w_gate_up, b_gate_up, w_down, b_down)
        if l % 2 == 0 and l + 1 < depth:
            x, *qkv = _moe_combine_qkv(l, alpha, meta, x1, yb, pos, gates, mod, common["ln_g"],
                                       common["ln_b"], w_qkv_bf, t_ctx)
        else:
            qkv = None
            x = _moe_combine(l, alpha, meta, x1, yb, pos, gates, mod, common["ln_g"], common["ln_b"],
                             split_at=t_ctx if l == depth - 1 else None)

    y_prompt = x[0].reshape(n_ctx_seq, seq, d)
    y_sample = x[1].reshape(n_lat, lat_seq, d)
    return (y_prompt, y_sample, jnp.stack(new_k, axis=1), jnp.stack(new_v, axis=1))
```

```python
import functools

import numpy as np
import jax
import jax.numpy as jnp
from jax import lax
from jax.experimental import pallas as pl
from jax.experimental.pallas import tpu as pltpu

F32 = jnp.float32
BF16 = jnp.bfloat16

GRID_W = 64
G_A = 8
CHUNK = 128
CONV_K = 31
WIN_ROWS = 8
WIN_COLS = 16
TOP_K = 4
SWIGLU_ALPHA = 1.702
SWIGLU_LIMIT = 7.0
LN_EPS = 1e-5
NEG_INF = -1e30

LANES = 128
SUBLANES = 8
VMEM_LIMIT = 56 * 1024 * 1024

TOKEN_TILE = 256
HALO = 16
MOE_TILE = 256
Q_ROWS = 4
K_ROWS = 12
PAD_LOGIT = -3e38


def _ln(x, g, b):
    mu = jnp.mean(x, axis=-1, keepdims=True)
    xc = x - mu
    var = jnp.mean(xc * xc, axis=-1, keepdims=True)
    return xc * lax.rsqrt(var + LN_EPS) * g + b


def _gelu(x):
    return 0.5 * x * (1.0 + jnp.tanh(0.7978845608028654 * (x + 0.044715 * (x * x * x))))


def _sigmoid(x):
    return jax.nn.sigmoid(x)


def _dot(a, b):
    return jnp.dot(a, b, preferred_element_type=F32)


def _dot_nt(a, b):
    return lax.dot_general(a, b, (((1,), (1,)), ((), ())), preferred_element_type=F32)


def _params(n_axes):
    return pltpu.CompilerParams(dimension_semantics=("arbitrary",) * n_axes,
                                vmem_limit_bytes=VMEM_LIMIT)


def _mod_body(c_ref, w_ref, b_ref, o_ref):
    c = c_ref[...]
    s = (c * _sigmoid(c)).astype(BF16)
    o_ref[...] = _dot(s, w_ref[...].astype(BF16)) + b_ref[...]


def _modulation(cvec, w_mod, b_mod):
    depth, d, n = w_mod.shape
    tn = n // 4
    return pl.pallas_call(
        _mod_body,
        grid=(depth, n // tn),
        in_specs=[pl.BlockSpec((SUBLANES, d), lambda l, j: (0, 0)),
                  pl.BlockSpec((None, d, tn), lambda l, j: (l, 0, j)),
                  pl.BlockSpec((None, 1, tn), lambda l, j: (l, 0, j))],
        out_specs=pl.BlockSpec((None, SUBLANES, tn), lambda l, j: (l, 0, j)),
        out_shape=jax.ShapeDtypeStruct((depth, SUBLANES, n), F32),
        compiler_params=_params(2),
        name="adaln_modulation",
    )(cvec, w_mod, b_mod.reshape(depth, 1, n))


def _post_mixer(alpha, x, y, mod_ref, lng_ref, lnb_ref, wr_ref, br_ref,
                x1_ref, h2_ref, te_ref, tg_ref, cnt_ref, cnt_scr):
    i = pl.program_id(0)
    x1 = _ln(alpha * x + mod_ref[2:3, :] * y, lng_ref[...], lnb_ref[...])
    x1_ref[...] = x1
    h2 = x1 * (1.0 + mod_ref[4:5, :]) + mod_ref[3:4, :]
    h2_ref[...] = h2
    logits = _dot(h2.astype(BF16), wr_ref[...]) + br_ref[...]
    n_e = cnt_scr.shape[0]
    lt = logits.T[:n_e]
    tt = lt.shape[1]
    eidx = lax.broadcasted_iota(jnp.int32, lt.shape, 0)
    vals, idxs = [], []
    for _ in range(TOP_K):
        m = jnp.max(lt, axis=0, keepdims=True)
        idx = jnp.min(jnp.where(lt == m, eidx, n_e), axis=0, keepdims=True)
        vals.append(m)
        idxs.append(idx)
        lt = jnp.where(eidx == idx, -jnp.inf, lt)
    exps = [jnp.exp(v - vals[0]) for v in vals]
    den = exps[0]
    for e in exps[1:]:
        den = den + e

    @pl.when(i == 0)
    def _():
        cnt_scr[...] = jnp.zeros(cnt_scr.shape, cnt_scr.dtype)

    onehot = jnp.zeros(lt.shape, F32)
    for k in range(TOP_K):
        onehot = onehot + (eidx == idxs[k]).astype(F32)
    row = lax.broadcasted_iota(jnp.int32, (tt, tt), 0)
    col = lax.broadcasted_iota(jnp.int32, (tt, tt), 1)
    before = _dot(onehot.astype(BF16), (row < col).astype(BF16)) + cnt_scr[:, 0:1]
    cnt = cnt_scr[...] + jnp.sum(onehot, axis=1, keepdims=True)
    cnt_scr[...] = cnt
    cnt_ref[...] = cnt.astype(jnp.int32)

    row_te = lax.broadcasted_iota(jnp.int32, te_ref.shape, 0)
    row_tg = lax.broadcasted_iota(jnp.int32, (LANES, tt), 0)
    te = jnp.zeros(te_ref.shape, jnp.int32)
    tg = jnp.zeros((LANES, tt), F32)
    for k in range(TOP_K):
        rank = jnp.sum(jnp.where(eidx == idxs[k], before, 0.0), axis=0, keepdims=True)
        te = jnp.where(row_te == k, idxs[k], te)
        te = jnp.where(row_te == TOP_K + k, rank.astype(jnp.int32), te)
        tg = jnp.where(row_tg == k, exps[k] / den, tg)
    te_ref[...] = te
    tg_ref[...] = tg.T


def _expert_rows(n_experts):
    return -(-n_experts // SUBLANES) * SUBLANES


def _epilogue_specs(l, d, n_experts):
    in_specs = [pl.BlockSpec((None, None, 1, d), lambda i, s, p, n: (l, 0, 0, 0)),
                pl.BlockSpec((None, None, 1, d), lambda i, s, p, n: (l, 0, 0, 0)),
                pl.BlockSpec((None, d, LANES), lambda i, s, p, n: (l, 0, 0)),
                pl.BlockSpec((None, 1, LANES), lambda i, s, p, n: (l, 0, 0))]
    out_specs = [pl.BlockSpec((TOKEN_TILE, d), lambda i, s, p, n: (i, 0)),
                 pl.BlockSpec((TOKEN_TILE, d), lambda i, s, p, n: (i, 0)),
                 pl.BlockSpec((2 * TOP_K, TOKEN_TILE), lambda i, s, p, n: (0, i)),
                 pl.BlockSpec((TOKEN_TILE, LANES), lambda i, s, p, n: (i, 0)),
                 pl.BlockSpec((_expert_rows(n_experts), LANES), lambda i, s, p, n: (0, 0))]
    return in_specs, out_specs


def _epilogue_out_shapes(t, d, n_experts):
    return [jax.ShapeDtypeStruct((t, d), F32), jax.ShapeDtypeStruct((t, d), F32),
            jax.ShapeDtypeStruct((2 * TOP_K, t), jnp.int32), jax.ShapeDtypeStruct((t, LANES), F32),
            jax.ShapeDtypeStruct((_expert_rows(n_experts), LANES), jnp.int32)]


def _epilogue_scratch(n_experts):
    return [pltpu.VMEM((_expert_rows(n_experts), LANES), F32)]


def _even_body(alpha, seg_ref, prev_ref, next_ref,
               x_ref, xp_ref, xn_ref, mod_ref, win_ref, sg_ref, sb_ref, wsp_ref, bsp_ref,
               cw_ref, cb_ref, cg_ref, cbb_ref, wout_ref, lng_ref, lnb_ref, wr_ref, br_ref,
               x1_ref, h2_ref, te_ref, tg_ref, cnt_ref, gl_scr, cnt_scr):
    i = pl.program_id(0)
    tt = x_ref.shape[0]
    ca = sg_ref.shape[-1]
    cb2 = 2 * ca
    x = x_ref[...]
    sc = 1.0 + mod_ref[1:2, :]
    sh = mod_ref[0:1, :]
    z = _dot((x * sc + sh).astype(BF16), win_ref[...])

    u = _gelu(z[:, :ca])
    v = _ln(_gelu(z[:, ca:cb2]), sg_ref[...], sb_ref[...]).astype(BF16)
    half = lax.broadcasted_iota(jnp.int32, (CHUNK, LANES), 1) < (LANES // 2)
    chunks = []
    for ck in range(tt // CHUNK):
        cols = []
        for j in range(ca // LANES):
            vblk = v[ck * CHUNK:(ck + 1) * CHUNK, j * LANES:(j + 1) * LANES]
            cols.append(jnp.where(half, _dot(wsp_ref[2 * j], vblk), _dot(wsp_ref[2 * j + 1], vblk)))
        chunks.append(jnp.concatenate(cols, axis=1) + bsp_ref[...])
    y_a = u * jnp.concatenate(chunks, axis=0)

    def glu_rows(xh_ref):
        zh = _dot((xh_ref[...] * sc + sh).astype(BF16), win_ref[:, cb2:])
        return zh[:, :ca] * _sigmoid(zh[:, ca:])

    gl_scr[0:HALO, :] = jnp.where(prev_ref[i] > 0, glu_rows(xp_ref), 0.0)
    gl_scr[HALO:HALO + tt, :] = z[:, cb2:cb2 + ca] * _sigmoid(z[:, cb2 + ca:])
    gl_scr[HALO + tt:, :] = jnp.where(next_ref[i] > 0, glu_rows(xn_ref), 0.0)
    off = HALO - CONV_K // 2
    g_ext = gl_scr[...]
    n_ext = g_ext.shape[0]
    dc = None
    for res in range(SUBLANES):
        taps = [k for k in range(CONV_K) if (off + k) % SUBLANES == res]
        if not taps:
            continue
        shifted = g_ext if res == 0 else pltpu.roll(g_ext, n_ext - res, axis=0)
        for k in taps:
            q = (off + k) // SUBLANES * SUBLANES
            term = shifted[q:q + tt, :] * cw_ref[k:k + 1, :]
            dc = term if dc is None else dc + term
    yb = _ln(dc + cb_ref[...], cg_ref[...], cbb_ref[...])
    y_b = yb * _sigmoid(yb)

    y = _dot(jnp.concatenate([y_a, y_b], axis=1).astype(BF16), wout_ref[...])
    _post_mixer(alpha, x, y, mod_ref, lng_ref, lnb_ref, wr_ref, br_ref,
                x1_ref, h2_ref, te_ref, tg_ref, cnt_ref, cnt_scr)


def _even_layer(l, alpha, meta, x, mod, p):
    t, d = x.shape
    li = l // 2
    nh = TOKEN_TILE // HALO
    n_halo = t // HALO
    ca = p["sgu_g"].shape[-1]
    ep_in, ep_out = _epilogue_specs(l, d, p["n_experts"])
    const3 = lambda i, s, pv, nx: (li, 0, 0)
    in_specs = [
        pl.BlockSpec((TOKEN_TILE, d), lambda i, s, pv, nx: (i, 0)),
        pl.BlockSpec((HALO, d), lambda i, s, pv, nx: (jnp.maximum(i * nh - 1, 0), 0)),
        pl.BlockSpec((HALO, d), lambda i, s, pv, nx: (jnp.minimum((i + 1) * nh, n_halo - 1), 0)),
        pl.BlockSpec((None, None, 6, d), lambda i, s, pv, nx: (l, s[i], 0, 0)),
        pl.BlockSpec((None, d, 4 * ca), const3),
        pl.BlockSpec((None, 1, ca), const3),
        pl.BlockSpec((None, 1, ca), const3),
        pl.BlockSpec((None, G_A, CHUNK, CHUNK), lambda i, s, pv, nx: (li, 0, 0, 0)),
        pl.BlockSpec((None, CHUNK, ca), const3),
        pl.BlockSpec((None, CONV_K, ca), const3),
        pl.BlockSpec((None, 1, ca), const3),
        pl.BlockSpec((None, 1, ca), const3),
        pl.BlockSpec((None, 1, ca), const3),
        pl.BlockSpec((None, 2 * ca, d), const3),
    ] + ep_in
    return pl.pallas_call(
        functools.partial(_even_body, alpha),
        grid_spec=pltpu.PrefetchScalarGridSpec(
            num_scalar_prefetch=3, grid=(t // TOKEN_TILE,),
            in_specs=in_specs, out_specs=ep_out,
            scratch_shapes=[pltpu.VMEM((TOKEN_TILE + 2 * HALO, ca), F32)]
            + _epilogue_scratch(p["n_experts"])),
        out_shape=_epilogue_out_shapes(t, d, p["n_experts"]),
        compiler_params=_params(1),
        name=f"even_mixer_{l}",
    )(meta["seg"], meta["prev"], meta["next"], x, x, x, mod,
      p["w_in"], p["sgu_g"], p["sgu_b"], p["w_sp"], p["b_sp"], p["conv_w"], p["conv_b"],
      p["cln_g"], p["cln_b"], p["w_out_ab"], p["ln_g"], p["ln_b"], p["w_router"], p["b_router"])


def _qkv_body(n_ctx_tiles, seg_ref, x_ref, mod_ref, w_ref, q_ref, k_ref, v_ref, k32_ref, v32_ref):
    _qkv_from(x_ref[...], mod_ref, w_ref, n_ctx_tiles, q_ref, k_ref, v_ref, k32_ref, v32_ref)


def _qkv_from(x, mod_ref, w_ref, n_ctx_tiles, q_ref, k_ref, v_ref, k32_ref, v32_ref, between=()):
    d = x.shape[1]
    between = list(between)
    h = (x * (1.0 + mod_ref[1:2, :]) + mod_ref[0:1, :]).astype(BF16)
    if between:
        parts = []
        for j in range(3):
            if between:
                between.pop(0)()
            parts.append(_dot(h, w_ref[:, j * d:(j + 1) * d]))
        for thunk in between:
            thunk()
        q, k, v = parts
    else:
        qkv = _dot(h, w_ref[...])
        q, k, v = qkv[:, :d], qkv[:, d:2 * d], qkv[:, 2 * d:]
    q_ref[...] = q.astype(BF16)
    k_ref[...] = k.astype(BF16)
    v_ref[...] = v.astype(BF16)

    @pl.when(pl.program_id(0) < n_ctx_tiles)
    def _():
        k32_ref[...] = k
        v32_ref[...] = v


def _qkv_proj(l, meta, x, mod, w_qkv, t_ctx):
    t, d = x.shape
    li = l // 2
    n_ctx_tiles = t_ctx // TOKEN_TILE
    tile = pl.BlockSpec((TOKEN_TILE, d), lambda i, s: (i, 0))
    ctx_tile = pl.BlockSpec((TOKEN_TILE, d), lambda i, s: (jnp.minimum(i, n_ctx_tiles - 1), 0))
    return pl.pallas_call(
        functools.partial(_qkv_body, n_ctx_tiles),
        grid_spec=pltpu.PrefetchScalarGridSpec(
            num_scalar_prefetch=1, grid=(t // TOKEN_TILE,),
            in_specs=[tile,
                      pl.BlockSpec((None, None, 6, d), lambda i, s: (l, s[i], 0, 0)),
                      pl.BlockSpec((None, d, 3 * d), lambda i, s: (li, 0, 0))],
            out_specs=[tile] * 3 + [ctx_tile] * 2),
        out_shape=[jax.ShapeDtypeStruct((t, d), BF16)] * 3 + [jax.ShapeDtypeStruct((t_ctx, d), F32)] * 2,
        compiler_params=_params(1),
        name=f"qkv_proj_{l}",
    )(meta["seg"], x, mod, w_qkv)


def _head_pair_attention(q2, k_parts, v_parts, bias_parts, scale):
    lane = lax.broadcasted_iota(jnp.int32, q2.shape, 1)
    outs = []
    for hh in range(2):
        qm = jnp.where((lane >= hh * (LANES // 2)) & (lane < (hh + 1) * (LANES // 2)), q2 * scale,
                       jnp.zeros_like(q2))
        ss = []
        for j, kp in enumerate(k_parts):
            s = _dot_nt(qm, kp)
            if bias_parts[hh][j] is not None:
                s = s + bias_parts[hh][j]
            ss.append(s)
        m = ss[0].max(axis=-1, keepdims=True)
        for s in ss[1:]:
            m = jnp.maximum(m, s.max(axis=-1, keepdims=True))
        den = None
        o = None
        for s, vp in zip(ss, v_parts):
            e = jnp.exp(s - m)
            es = e.sum(axis=-1, keepdims=True)
            den = es if den is None else den + es
            pv = _dot(e.astype(BF16), vp)
            o = pv if o is None else o + pv
        outs.append(o / den)
    return jnp.where(lane < LANES // 2, outs[0], outs[1])


def _attn_ctx_body(scale, q_ref, k_ref, v_ref, o_ref):
    d = q_ref.shape[1]
    for pr in range(d // LANES):
        sl = slice(pr * LANES, (pr + 1) * LANES)
        o = _head_pair_attention(q_ref[:, sl], [k_ref[:, sl]], [v_ref[:, sl]],
                                 [[None], [None]], scale)
        o_ref[:, sl] = o.astype(o_ref.dtype)


def _attn_ctx(q, k, v, n_seq, seq, scale):
    d = q.shape[1]
    blk = pl.BlockSpec((seq, d), lambda b: (b, 0))
    return pl.pallas_call(
        functools.partial(_attn_ctx_body, scale),
        grid=(n_seq,),
        in_specs=[blk, blk, blk],
        out_specs=blk,
        out_shape=jax.ShapeDtypeStruct((n_seq * seq, d), BF16),
        compiler_params=_params(1),
        name="attn_ctx",
    )(q, k, v)


def _attn_lat_body(scale, cls_ref, kb_ref, q_ref, k0_ref, k1_ref, k2_ref, v0_ref, v1_ref, v2_ref,
                   ck_ref, cv_ref, bias_ref, o_ref):
    d = q_ref.shape[1]
    for pr in range(d // LANES):
        sl = slice(pr * LANES, (pr + 1) * LANES)
        k_loc = jnp.concatenate([k0_ref[:, sl], k1_ref[:, sl], k2_ref[:, sl]], axis=0)
        v_loc = jnp.concatenate([v0_ref[:, sl], v1_ref[:, sl], v2_ref[:, sl]], axis=0)
        ck = ck_ref[:, sl].astype(BF16)
        cv = cv_ref[:, sl].astype(BF16)
        o = _head_pair_attention(q_ref[:, sl], [k_loc, ck], [v_loc, cv],
                                 [[bias_ref[2 * pr], None], [bias_ref[2 * pr + 1], None]], scale)
        o_ref[:, sl] = o.astype(o_ref.dtype)


def _latent_window_tables(rows_n):
    wr = min(WIN_ROWS, rows_n)
    n_rt = rows_n // Q_ROWS
    kstart = np.clip(np.arange(n_rt) * Q_ROWS - wr // 2, 0, rows_n - K_ROWS)
    kstart = (kstart // Q_ROWS) * Q_ROWS
    patterns, cls = [], []
    for rt in range(n_rt):
        pat = np.full((Q_ROWS, K_ROWS), -1, np.int64)
        for qi in range(Q_ROWS):
            r = rt * Q_ROWS + qi
            rs = int(np.clip(r - wr // 2, 0, rows_n - wr))
            for kj in range(K_ROWS):
                kr = int(kstart[rt]) + kj
                if rs <= kr < rs + wr:
                    pat[qi, kj] = kr - r + WIN_ROWS - 1
        assert (pat >= 0).sum(axis=1).min() == wr, "key block does not cover the window"
        key = pat.tobytes()
        if key not in [p.tobytes() for p in patterns]:
            patterns.append(pat)
        cls.append([p.tobytes() for p in patterns].index(key))
    return (kstart // Q_ROWS).astype(np.int32), np.asarray(cls, np.int32), np.stack(patterns)


def _latent_bias(rpb, patterns):
    h = rpb.shape[0]
    qc = np.arange(GRID_W)[:, None]
    kc = np.arange(GRID_W)[None, :]
    qcs = np.clip(qc - WIN_COLS // 2, 0, GRID_W - WIN_COLS)
    col_ok = (kc >= qcs) & (kc < qcs + WIN_COLS)
    dc = np.clip(kc - qc + WIN_COLS - 1, 0, 2 * WIN_COLS - 2)
    onehot = (dc[None] == np.arange(2 * WIN_COLS - 1)[:, None, None]) & col_ok[None]
    cm = jnp.einsum("hrd,dqk->hrqk", rpb, jnp.asarray(onehot, F32), precision=lax.Precision.HIGHEST)
    cm = jnp.where(jnp.asarray(col_ok), cm, NEG_INF)
    cx = jnp.concatenate([cm, jnp.full((h, 1, GRID_W, GRID_W), NEG_INF, F32)], axis=1)
    cx2 = jnp.concatenate([cx, cx], axis=-1)
    idx = np.where(patterns >= 0, patterns, 2 * WIN_ROWS - 1)
    n_cls = idx.shape[0]
    assert 2 * GRID_W == LANES and K_ROWS % 2 == 0

    def body(idx_ref, cx_ref, o_ref):
        c = pl.program_id(0)
        low = lax.broadcasted_iota(jnp.int32, (GRID_W, LANES), 1) < GRID_W
        for qi in range(Q_ROWS):
            for m in range(K_ROWS // 2):
                base = (c * Q_ROWS + qi) * K_ROWS + 2 * m
                blk = jnp.where(low, cx_ref[idx_ref[base]], cx_ref[idx_ref[base + 1]])
                o_ref[qi * GRID_W:(qi + 1) * GRID_W, m * LANES:(m + 1) * LANES] = blk

    return pl.pallas_call(
        body,
        grid_spec=pltpu.PrefetchScalarGridSpec(
            num_scalar_prefetch=1, grid=(n_cls, h),
            in_specs=[pl.BlockSpec((None, cx2.shape[1], GRID_W, LANES), lambda c, j, ix: (j, 0, 0, 0))],
            out_specs=pl.BlockSpec((None, None, Q_ROWS * GRID_W, K_ROWS * GRID_W),
                                   lambda c, j, ix: (c, j, 0, 0))),
        out_shape=jax.ShapeDtypeStruct((n_cls, h, Q_ROWS * GRID_W, K_ROWS * GRID_W), F32),
        compiler_params=_params(2),
        name="latent_bias_table",
    )(jnp.asarray(idx.reshape(-1), jnp.int32), cx2)


def _attn_lat(q, k, v, ck, cv, li, bias, tables, tok0, n_batch, n_tok, scale):
    d = q.shape[1]
    kblk, cls, _ = tables
    n_rt = kblk.shape[0]
    qt = Q_ROWS * GRID_W
    base = tok0 // qt
    per_b = n_tok // qt
    h = bias.shape[1]
    lc = ck.shape[2]

    def kv_spec(j):
        return pl.BlockSpec((qt, d), lambda b, r, c, kb: (base + b * per_b + kb[r] + j, 0))

    return pl.pallas_call(
        functools.partial(_attn_lat_body, scale),
        grid_spec=pltpu.PrefetchScalarGridSpec(
            num_scalar_prefetch=2, grid=(n_batch, n_rt),
            in_specs=[pl.BlockSpec((qt, d), lambda b, r, c, kb: (base + b * per_b + r, 0)),
                      kv_spec(0), kv_spec(1), kv_spec(2), kv_spec(0), kv_spec(1), kv_spec(2),
                      pl.BlockSpec((None, None, lc, d), lambda b, r, c, kb: (b, li, 0, 0)),
                      pl.BlockSpec((None, None, lc, d), lambda b, r, c, kb: (b, li, 0, 0)),
                      pl.BlockSpec((None, h, qt, K_ROWS * GRID_W), lambda b, r, c, kb: (c[r], 0, 0, 0))],
            out_specs=pl.BlockSpec((qt, d), lambda b, r, c, kb: (b * per_b + r, 0))),
        out_shape=jax.ShapeDtypeStruct((n_batch * n_tok, d), BF16),
        compiler_params=_params(2),
        name="attn_latent",
    )(jnp.asarray(cls), jnp.asarray(kblk), q, k, k, k, v, v, v, ck, cv, bias)


def _proj_body(alpha, n_ctx_tiles, seg_ref, prev_ref, next_ref, x_ref, oc_ref, ol_ref, mod_ref, w_ref,
               lng_ref, lnb_ref, wr_ref, br_ref, x1_ref, h2_ref, te_ref, tg_ref, cnt_ref, cnt_scr):
    o = jnp.where(pl.program_id(0) < n_ctx_tiles, oc_ref[...], ol_ref[...])
    y = _dot(o, w_ref[...])
    _post_mixer(alpha, x_ref[...], y, mod_ref, lng_ref, lnb_ref, wr_ref, br_ref,
                x1_ref, h2_ref, te_ref, tg_ref, cnt_ref, cnt_scr)


def _odd_out_proj(l, alpha, meta, x, o_ctx, o_lat, mod, p):
    t, d = x.shape
    li = l // 2
    nc = o_ctx.shape[0] // TOKEN_TILE
    ep_in, ep_out = _epilogue_specs(l, d, p["n_experts"])
    tile = pl.BlockSpec((TOKEN_TILE, d), lambda i, s, pv, nx: (i, 0))
    return pl.pallas_call(
        functools.partial(_proj_body, alpha, nc),
        grid_spec=pltpu.PrefetchScalarGridSpec(
            num_scalar_prefetch=3, grid=(t // TOKEN_TILE,),
            in_specs=[tile,
                      pl.BlockSpec((TOKEN_TILE, d), lambda i, s, pv, nx: (jnp.minimum(i, nc - 1), 0)),
                      pl.BlockSpec((TOKEN_TILE, d), lambda i, s, pv, nx: (jnp.maximum(i - nc, 0), 0)),
                      pl.BlockSpec((None, None, 6, d), lambda i, s, pv, nx: (l, s[i], 0, 0)),
                      pl.BlockSpec((None, d, d), lambda i, s, pv, nx: (li, 0, 0))] + ep_in,
            out_specs=ep_out, scratch_shapes=_epilogue_scratch(p["n_experts"])),
        out_shape=_epilogue_out_shapes(t, d, p["n_experts"]),
        compiler_params=_params(1),
        name=f"attn_out_proj_{l}",
    )(meta["seg"], meta["prev"], meta["next"], x, o_ctx, o_lat, mod, p["w_out_c"],
      p["ln_g"], p["ln_b"], p["w_router"], p["b_router"])


def _route_tables(te, counts, n_experts):
    t = te.shape[1]
    counts = counts[:n_experts, 0]
    padded = (counts + MOE_TILE - 1) // MOE_TILE * MOE_TILE
    pad_end = jnp.cumsum(padded).astype(jnp.int32)
    pad_start = pad_end - padded
    experts, ranks = te[:TOP_K], te[TOP_K:]
    eid = jnp.arange(n_experts, dtype=jnp.int32)
    sel = experts[None] == eid[:, None, None]
    pos = ranks + jnp.sum(jnp.where(sel, pad_start[:, None, None], 0), axis=0)
    return pad_end, pos.astype(jnp.int32)


def _dispatch_body(pend_ref, pos_ref, h2_ref, xs_hbm, stage, sem, zsem):
    i = pl.program_id(0)
    n_steps = pl.num_programs(0)
    slot = i % 2
    tt = h2_ref.shape[0]
    n_e = pend_ref.shape[0]

    def scatter_wait(s):
        for _ in range(TOP_K):
            pltpu.make_async_copy(stage.at[s], xs_hbm.at[pl.ds(0, tt)], sem.at[s]).wait()

    @pl.when(i == 0)
    def _():
        stage[0] = jnp.zeros(stage.shape[1:], stage.dtype)
        n_tiles = xs_hbm.shape[0] // tt
        n_used = pend_ref[n_e - 1] // tt
        for phase in range(2):
            for e in range(n_e):
                lo = pend_ref[e - 1] if e else 0
                for cond, row0 in ((pend_ref[e] > lo, pend_ref[e] - tt),
                                   (n_used + e < n_tiles, (n_used + e) * tt)):
                    @pl.when(cond)
                    def _():
                        cp = pltpu.make_async_copy(
                            stage.at[0], xs_hbm.at[pl.ds(pl.multiple_of(row0, tt), tt)], zsem)
                        if phase == 0:
                            cp.start()
                        else:
                            cp.wait()

    @pl.when(i >= 2)
    def _():
        scatter_wait(slot)

    stage[slot] = h2_ref[...].reshape(stage.shape[1:])

    def body(r, c):
        for k in range(TOP_K):
            pltpu.make_async_copy(stage.at[slot, r], xs_hbm.at[pos_ref[k, r]],
                                  sem.at[slot]).start(priority=k % 2)
        return c
    lax.fori_loop(0, tt, body, 0, unroll=8)

    @pl.when(i == n_steps - 1)
    def _():
        @pl.when(n_steps >= 2)
        def _():
            scatter_wait(1 - slot)
        scatter_wait(slot)


def _moe_dispatch(l, h2, pad_end, pos, n_tiles):
    t, d = h2.shape
    assert TOKEN_TILE == MOE_TILE
    return pl.pallas_call(
        _dispatch_body,
        grid_spec=pltpu.PrefetchScalarGridSpec(
            num_scalar_prefetch=1, grid=(t // TOKEN_TILE,),
            in_specs=[pl.BlockSpec((TOP_K, TOKEN_TILE), lambda i, pe: (0, i),
                                   memory_space=pltpu.SMEM),
                      pl.BlockSpec((TOKEN_TILE, d), lambda i, pe: (i, 0))],
            out_specs=pl.BlockSpec(memory_space=pl.ANY),
            scratch_shapes=[pltpu.VMEM((2, TOKEN_TILE, SUBLANES, LANES), F32),
                            pltpu.SemaphoreType.DMA((2,)),
                            pltpu.SemaphoreType.DMA]),
        out_shape=jax.ShapeDtypeStruct((n_tiles * MOE_TILE, SUBLANES, LANES), F32),
        compiler_params=_params(1),
        name=f"moe_dispatch_{l}",
    )(pad_end, pos, h2)


def _moe_body(l, pend_ref, bgu_ref, bd_ref, xs_hbm, wgu_hbm, wd_hbm, yb_hbm,
              xbuf, ybuf, wgu_st, wd_st, wgu_bf, wd_bf, xsem, ysem, wsem):
    e = pl.program_id(0)
    n_e = pl.num_programs(0)
    tm = xbuf.shape[1]
    de = wd_bf.shape[0]
    par = e % 2
    g_lo = jnp.where(e == 0, 0, pend_ref[jnp.maximum(e - 1, 0)]) // tm
    g_hi = pend_ref[e] // tm
    n_used = pend_ref[n_e - 1] // tm
    n_tiles = yb_hbm.shape[0] // tm

    def weight_copies(ex, p):
        return (pltpu.make_async_copy(wgu_hbm.at[l, ex], wgu_st.at[p], wsem.at[p, 0]),
                pltpu.make_async_copy(wd_hbm.at[l, ex], wd_st.at[p], wsem.at[p, 1]))

    def x_copy(g, s):
        return pltpu.make_async_copy(xs_hbm.at[pl.ds(pl.multiple_of(g * tm, tm), tm)], xbuf.at[s],
                                     xsem.at[s])

    def y_copy(g, s):
        return pltpu.make_async_copy(ybuf.at[s], yb_hbm.at[pl.ds(pl.multiple_of(g * tm, tm), tm)],
                                     ysem.at[s])

    @pl.when(e == 0)
    def _():
        for cp in weight_copies(0, 0):
            cp.start(priority=1)

        @pl.when(n_used > 0)
        def _():
            x_copy(0, 0).start()

    @pl.when(e + 1 < n_e)
    def _():
        for cp in weight_copies(e + 1, 1 - par):
            cp.start(priority=1)

    for cp in weight_copies(e, par):
        cp.wait()

    @pl.when(g_hi > g_lo)
    def _():
        wgu_bf[...] = wgu_st[par].astype(BF16)
        wd_bf[...] = wd_st[par].astype(BF16)

    def tile_body(g, carry):
        s = g % 2
        x_copy(g, s).wait()

        @pl.when(g + 1 < n_used)
        def _():
            x_copy(g + 1, 1 - s).start()

        x = xbuf[s].reshape(tm, wgu_bf.shape[0])
        hgu = _dot(x.astype(BF16), wgu_bf[...]) + bgu_ref[...]
        x_glu = jnp.minimum(hgu[:, :de], SWIGLU_LIMIT)
        x_lin = jnp.clip(hgu[:, de:], -SWIGLU_LIMIT, SWIGLU_LIMIT)
        act = x_glu * _sigmoid(SWIGLU_ALPHA * x_glu) * (x_lin + 1.0)
        y = _dot(act.astype(BF16), wd_bf[...]) + bd_ref[...]

        @pl.when(g >= 2)
        def _():
            y_copy(g - 2, s).wait()

        ybuf[s] = y.reshape(ybuf.shape[1:])
        y_copy(g, s).start()
        return carry

    lax.fori_loop(g_lo, g_hi, tile_body, 0)

    @pl.when(e == n_e - 1)
    def _():
        @pl.when(n_used >= 2)
        def _():
            y_copy(n_used - 2, n_used % 2).wait()

        @pl.when(n_used >= 1)
        def _():
            y_copy(n_used - 1, (n_used + 1) % 2).wait()

        ybuf[0] = jnp.zeros(ybuf.shape[1:], ybuf.dtype)
        for phase in range(2):
            for m in range(n_e):
                @pl.when(n_used + m < n_tiles)
                def _():
                    cp = y_copy(n_used + m, 0)
                    if phase == 0:
                        cp.start()
                    else:
                        cp.wait()


def _moe_experts(l, xs, pad_end, w_gu, b_gu, w_d, b_d):
    n_e, de, d = w_d.shape[1], w_d.shape[2], w_d.shape[3]
    row_tile = (MOE_TILE,) + xs.shape[1:]
    return pl.pallas_call(
        functools.partial(_moe_body, l),
        grid_spec=pltpu.PrefetchScalarGridSpec(
            num_scalar_prefetch=1, grid=(n_e,),
            in_specs=[pl.BlockSpec((None, None, 1, 2 * de), lambda e, pe: (l, e, 0, 0)),
                      pl.BlockSpec((None, None, 1, d), lambda e, pe: (l, e, 0, 0)),
                      pl.BlockSpec(memory_space=pl.ANY),
                      pl.BlockSpec(memory_space=pl.ANY),
                      pl.BlockSpec(memory_space=pl.ANY)],
            out_specs=pl.BlockSpec(memory_space=pl.ANY),
            scratch_shapes=[pltpu.VMEM((2,) + row_tile, F32), pltpu.VMEM((2,) + row_tile, F32),
                            pltpu.VMEM((2, d, 2 * de), F32), pltpu.VMEM((2, de, d), F32),
                            pltpu.VMEM((d, 2 * de), BF16), pltpu.VMEM((de, d), BF16),
                            pltpu.SemaphoreType.DMA((2,)), pltpu.SemaphoreType.DMA((2,)),
                            pltpu.SemaphoreType.DMA((2, 2))]),
        out_shape=jax.ShapeDtypeStruct(xs.shape, F32),
        compiler_params=_params(1),
        name=f"moe_experts_{l}",
    )(pad_end, b_gu.reshape(b_gu.shape[0], n_e, 1, 2 * de), b_d.reshape(b_d.shape[0], n_e, 1, d),
      xs, w_gu, w_d)


def _combine_rows(alpha, pos_ref, posn_ref, x1_ref, g_ref, mod_ref, lng_ref, lnb_ref, yb_hbm, buf, sem,
                  n_chunks=1):
    i = pl.program_id(0)
    n_steps = pl.num_programs(0)
    slot = i % 2
    tt = x1_ref.shape[0]
    rows_per = tt // n_chunks

    def gather_start(idx_ref, s, r0, r1):
        def body(r, c):
            for k in range(TOP_K):
                pltpu.make_async_copy(yb_hbm.at[idx_ref[k, r]], buf.at[s, k, r],
                                      sem.at[s]).start(priority=k % 2)
            return c
        lax.fori_loop(r0, r1, body, 0, unroll=8)

    def next_chunk(c):
        def issue():
            @pl.when(i + 1 < n_steps)
            def _():
                gather_start(posn_ref, 1 - slot, c * rows_per, (c + 1) * rows_per)
        return issue

    pending = [next_chunk(c) for c in range(n_chunks)]

    @pl.when(i == 0)
    def _():
        gather_start(pos_ref, 0, 0, tt)

    pending.pop(0)()
    for k in range(TOP_K):
        pltpu.make_async_copy(yb_hbm.at[pl.ds(0, tt)], buf.at[slot, k], sem.at[slot]).wait()
    g = g_ref[...]
    y = None
    for k in range(TOP_K):
        part = g[:, k:k + 1] * buf[slot, k].reshape(x1_ref.shape)
        y = part if y is None else y + part
        if len(pending) > 3:
            pending.pop(0)()
    out = _ln(alpha * x1_ref[...] + mod_ref[5:6, :] * y, lng_ref[...], lnb_ref[...])
    return out, pending


def _combine_body(alpha, n_ctx_tiles, seg_ref, pos_ref, posn_ref, x1_ref, g_ref, mod_ref, lng_ref,
                  lnb_ref, yb_hbm, o_ref, *rest):
    o2_ref = rest[0] if len(rest) == 3 else None
    buf, sem = rest[-2:]
    i = pl.program_id(0)
    out, _ = _combine_rows(alpha, pos_ref, posn_ref, x1_ref, g_ref, mod_ref, lng_ref, lnb_ref, yb_hbm,
                           buf, sem)
    if o2_ref is None:
        o_ref[...] = out
    else:
        @pl.when(i < n_ctx_tiles)
        def _():
            o_ref[...] = out

        @pl.when(i >= n_ctx_tiles)
        def _():
            o2_ref[...] = out


def _combine_qkv_body(alpha, n_ctx_tiles, seg_ref, pos_ref, posn_ref, x1_ref, g_ref, mod_ref, lng_ref,
                      lnb_ref, modn_ref, w_ref, yb_hbm, o_ref, q_ref, k_ref, v_ref, k32_ref, v32_ref,
                      buf, sem):
    out, pending = _combine_rows(alpha, pos_ref, posn_ref, x1_ref, g_ref, mod_ref, lng_ref, lnb_ref,
                                 yb_hbm, buf, sem, n_chunks=8)
    o_ref[...] = out
    _qkv_from(out, modn_ref, w_ref, n_ctx_tiles, q_ref, k_ref, v_ref, k32_ref, v32_ref,
              between=pending)


def _moe_combine(l, alpha, meta, x1, yb, pos, gates, mod, ln_g, ln_b, split_at=None):
    t, d = x1.shape
    nt = t // TOKEN_TILE
    tile = pl.BlockSpec((TOKEN_TILE, d), lambda i, s: (i, 0))
    pos_blk = lambda f: pl.BlockSpec((TOP_K, TOKEN_TILE), f, memory_space=pltpu.SMEM)
    if split_at is None:
        nc, out_specs, out_shape = 0, tile, jax.ShapeDtypeStruct((t, d), F32)
    else:
        nc = split_at // TOKEN_TILE
        out_specs = [pl.BlockSpec((TOKEN_TILE, d), lambda i, s: (jnp.minimum(i, nc - 1), 0)),
                     pl.BlockSpec((TOKEN_TILE, d), lambda i, s: (jnp.maximum(i - nc, 0), 0))]
        out_shape = [jax.ShapeDtypeStruct((split_at, d), F32),
                     jax.ShapeDtypeStruct((t - split_at, d), F32)]
    return pl.pallas_call(
        functools.partial(_combine_body, alpha, nc),
        grid_spec=pltpu.PrefetchScalarGridSpec(
            num_scalar_prefetch=1, grid=(nt,),
            in_specs=[pos_blk(lambda i, s: (0, i)),
                      pos_blk(lambda i, s: (0, jnp.minimum(i + 1, nt - 1))),
                      tile,
                      pl.BlockSpec((TOKEN_TILE, LANES), lambda i, s: (i, 0)),
                      pl.BlockSpec((None, None, 6, d), lambda i, s: (l, s[i], 0, 0)),
                      pl.BlockSpec((None, None, 1, d), lambda i, s: (l, 1, 0, 0)),
                      pl.BlockSpec((None, None, 1, d), lambda i, s: (l, 1, 0, 0)),
                      pl.BlockSpec(memory_space=pl.ANY)],
            out_specs=out_specs,
            scratch_shapes=[pltpu.VMEM((2, TOP_K, TOKEN_TILE) + yb.shape[1:], F32),
                            pltpu.SemaphoreType.DMA((2,))]),
        out_shape=out_shape,
        compiler_params=_params(1),
        name=f"moe_combine_{l}",
    )(meta["seg"], pos, pos, x1, gates, mod, ln_g, ln_b, yb)


def _moe_combine_qkv(l, alpha, meta, x1, yb, pos, gates, mod, ln_g, ln_b, w_qkv, t_ctx):
    t, d = x1.shape
    nt = t // TOKEN_TILE
    nc = t_ctx // TOKEN_TILE
    li = (l + 1) // 2
    tile = pl.BlockSpec((TOKEN_TILE, d), lambda i, s: (i, 0))
    ctx_tile = pl.BlockSpec((TOKEN_TILE, d), lambda i, s: (jnp.minimum(i, nc - 1), 0))
    pos_blk = lambda f: pl.BlockSpec((TOP_K, TOKEN_TILE), f, memory_space=pltpu.SMEM)
    return pl.pallas_call(
        functools.partial(_combine_qkv_body, alpha, nc),
        grid_spec=pltpu.PrefetchScalarGridSpec(
            num_scalar_prefetch=1, grid=(nt,),
            in_specs=[pos_blk(lambda i, s: (0, i)),
                      pos_blk(lambda i, s: (0, jnp.minimum(i + 1, nt - 1))),
                      tile,
                      pl.BlockSpec((TOKEN_TILE, LANES), lambda i, s: (i, 0)),
                      pl.BlockSpec((None, None, 6, d), lambda i, s: (l, s[i], 0, 0)),
                      pl.BlockSpec((None, None, 1, d), lambda i, s: (l, 1, 0, 0)),
                      pl.BlockSpec((None, None, 1, d), lambda i, s: (l, 1, 0, 0)),
                      pl.BlockSpec((None, None, 6, d), lambda i, s: (l + 1, s[i], 0, 0)),
                      pl.BlockSpec((None, d, 3 * d), lambda i, s: (li, 0, 0)),
                      pl.BlockSpec(memory_space=pl.ANY)],
            out_specs=[tile] * 4 + [ctx_tile] * 2,
            scratch_shapes=[pltpu.VMEM((2, TOP_K, TOKEN_TILE) + yb.shape[1:], F32),
                            pltpu.SemaphoreType.DMA((2,))]),
        out_shape=[jax.ShapeDtypeStruct((t, d), F32)] + [jax.ShapeDtypeStruct((t, d), BF16)] * 3
        + [jax.ShapeDtypeStruct((t_ctx, d), F32)] * 2,
        compiler_params=_params(1),
        name=f"moe_combine_qkv_{l}",
    )(meta["seg"], pos, pos, x1, gates, mod, ln_g, ln_b, mod, w_qkv, yb)


def _token_meta(n_ctx_seq, seq, n_lat, lat_seq):
    seg, prev, nxt = [], [], []
    for n_seq, length, seg_of in ((n_ctx_seq, seq, lambda b: 0), (n_lat, lat_seq, lambda b: 1 + b)):
        per = length // TOKEN_TILE
        for b in range(n_seq):
            for j in range(per):
                seg.append(seg_of(b))
                prev.append(int(j > 0))
                nxt.append(int(j < per - 1))
    as_i32 = lambda a: jnp.asarray(np.asarray(a, np.int32))
    return {"seg": as_i32(seg), "prev": as_i32(prev), "next": as_i32(nxt)}


def kernel(x_prompt, x_sample, c, cache_k, cache_v, c_ctx, w_mod, b_mod, ln_g, ln_b, w_in_ab, sgu_ln_g, sgu_ln_b, w_spatial, b_spatial, conv_w, conv_b, conv_ln_g, conv_ln_b, w_out_ab, w_qkv, rpb, w_out_c, w_router, b_router, w_gate_up, b_gate_up, w_down, b_down):
    n_ctx_seq, seq, d = x_prompt.shape
    n_lat, lat_seq, _ = x_sample.shape
    depth = w_mod.shape[0]
    n_heads, head_dim = cache_k.shape[3], cache_k.shape[4]
    n_experts = w_router.shape[-1]
    ca = sgu_ln_g.shape[-1]
    n_even, n_odd = w_in_ab.shape[0], w_qkv.shape[0]
    t_ctx, t_lat = n_ctx_seq * seq, n_lat * lat_seq
    t = t_ctx + t_lat
    rows_n = lat_seq // GRID_W
    assert seq % TOKEN_TILE == 0 and lat_seq % TOKEN_TILE == 0 and 1 + n_lat <= SUBLANES
    assert TOKEN_TILE % CHUNK == 0 and HALO >= CONV_K // 2 and ca == w_out_ab.shape[1] // 2
    assert rows_n % Q_ROWS == 0 and rows_n >= K_ROWS and t_ctx % (Q_ROWS * GRID_W) == 0
    assert n_heads * head_dim == d and 2 * head_dim == LANES and n_experts <= LANES
    assert (t * TOP_K) % MOE_TILE == 0
    assert d == SUBLANES * LANES, "MoE rows are moved as one (SUBLANES, LANES) f32 tile each"
    alpha = float((2 * depth) ** 0.25)
    scale = float(head_dim ** -0.5)
    assert np.frexp(scale)[0] == 0.5, "the attention scale is folded into the bf16 queries"
    meta = _token_meta(n_ctx_seq, seq, n_lat, lat_seq)

    x = jnp.concatenate([x_prompt.reshape(t_ctx, d), x_sample.reshape(t_lat, d)], axis=0)
    cvec = jnp.zeros((SUBLANES, d), F32).at[0].set(c_ctx).at[1:1 + n_lat].set(c)
    mod = _modulation(cvec, w_mod, b_mod).reshape(depth, SUBLANES, 6, d)

    pad_e = LANES - n_experts
    common = {
        "n_experts": n_experts,
        "ln_g": ln_g.reshape(depth, 2, 1, d), "ln_b": ln_b.reshape(depth, 2, 1, d),
        "w_router": jnp.pad(w_router, ((0, 0), (0, 0), (0, pad_e))).astype(BF16),
        "b_router": jnp.pad(b_router, ((0, 0), (0, pad_e)), constant_values=PAD_LOGIT).reshape(depth, 1, LANES),
    }
    even = dict(common)
    even.update({
        "w_in": w_in_ab.astype(BF16), "sgu_g": sgu_ln_g.reshape(n_even, 1, ca),
        "sgu_b": sgu_ln_b.reshape(n_even, 1, ca), "w_sp": w_spatial.astype(BF16),
        "b_sp": jnp.repeat(jnp.transpose(b_spatial, (0, 2, 1)), ca // G_A, axis=2),
        "conv_w": conv_w, "conv_b": conv_b.reshape(n_even, 1, ca),
        "cln_g": conv_ln_g.reshape(n_even, 1, ca), "cln_b": conv_ln_b.reshape(n_even, 1, ca),
        "w_out_ab": w_out_ab.astype(BF16)})
    odd = dict(common)
    odd["w_out_c"] = w_out_c.astype(BF16)
    w_qkv_bf = w_qkv.astype(BF16)
    lat_tables = _latent_window_tables(rows_n)
    ck_all = cache_k.reshape(n_lat, n_odd, -1, d)
    cv_all = cache_v.reshape(n_lat, n_odd, -1, d)
    n_tiles = t * TOP_K // MOE_TILE + n_experts

    new_k, new_v = [], []
    qkv = None
    for l in range(depth):
        i = l // 2
        if l % 2 == 0:
            x1, h2, top_e, gates, counts = _even_layer(l, alpha, meta, x, mod, even)
        else:
            if qkv is None:
                qkv = _qkv_proj(l, meta, x, mod, w_qkv_bf, t_ctx)
            q, k, v, k32, v32 = qkv
            new_k.append(k32.reshape(n_ctx_seq, seq, n_heads, head_dim))
            new_v.append(v32.reshape(n_ctx_seq, seq, n_heads, head_dim))
            o_ctx = _attn_ctx(q, k, v, n_ctx_seq, seq, scale)
            bias = _latent_bias(rpb[i], lat_tables[2])
            o_lat = _attn_lat(q, k, v, ck_all, cv_all, i, bias, lat_tables, t_ctx, n_lat, lat_seq, scale)
            x1, h2, top_e, gates, counts = _odd_out_proj(l, alpha, meta, x, o_ctx, o_lat, mod, odd)
        pad_end, pos = _route_tables(top_e, counts, n_experts)
        xs = _moe_dispatch(l, h2, pad_end, pos, n_tiles)
        yb = _moe_experts(l, xs, pad_end, w_gate_up, b_gate_up, w_down, b_down)
        if l % 2 == 0 and l + 1 < depth:
            x, *qkv = _moe_combine_qkv(l, alpha, meta, x1, yb, pos, gates, mod, common["ln_g"],
                                       common["ln_b"], w_qkv_bf, t_ctx)
        else:
            qkv = None
            x = _moe_combine(l, alpha, meta, x1, yb, pos, gates, mod, common["ln_g"], common["ln_b"],
                             split_at=t_ctx if l == depth - 1 else None)

    y_prompt = x[0].reshape(n_ctx_seq, seq, d)
    y_sample = x[1].reshape(n_lat, lat_seq, d)
    return (y_prompt, y_sample, jnp.stack(new_k, axis=1), jnp.stack(new_v, axis=1))
```

```python
import functools

import numpy as np
import jax
import jax.numpy as jnp
from jax import lax
from jax.experimental import pallas as pl
from jax.experimental.pallas import tpu as pltpu

F32 = jnp.float32
BF16 = jnp.bfloat16

GRID_W = 64
G_A = 8
CHUNK = 128
CONV_K = 31
WIN_ROWS = 8
WIN_COLS = 16
TOP_K = 4
SWIGLU_ALPHA = 1.702
SWIGLU_LIMIT = 7.0
LN_EPS = 1e-5
NEG_INF = -1e30

LANES = 128
SUBLANES = 8
VMEM_LIMIT = 56 * 1024 * 1024

TOKEN_TILE = 256
HALO = 16
MOE_TILE = 256
Q_ROWS = 4
K_ROWS = 12
PAD_LOGIT = -3e38


def _ln(x, g, b):
    mu = jnp.mean(x, axis=-1, keepdims=True)
    xc = x - mu
    var = jnp.mean(xc * xc, axis=-1, keepdims=True)
    return xc * lax.rsqrt(var + LN_EPS) * g + b


def _gelu(x):
    return 0.5 * x * (1.0 + jnp.tanh(0.7978845608028654 * (x + 0.044715 * (x * x * x))))


def _sigmoid(x):
    return jax.nn.sigmoid(x)


def _dot(a, b):
    return jnp.dot(a, b, preferred_element_type=F32)


def _dot_nt(a, b):
    return lax.dot_general(a, b, (((1,), (1,)), ((), ())), preferred_element_type=F32)


def _params(n_axes):
    return pltpu.CompilerParams(dimension_semantics=("arbitrary",) * n_axes,
                                vmem_limit_bytes=VMEM_LIMIT)


def _mod_body(c_ref, w_ref, b_ref, o_ref):
    c = c_ref[...]
    s = (c * _sigmoid(c)).astype(BF16)
    o_ref[...] = _dot(s, w_ref[...].astype(BF16)) + b_ref[...]


def _modulation(cvec, w_mod, b_mod):
    depth, d, n = w_mod.shape
    tn = n // 4
    return pl.pallas_call(
        _mod_body,
        grid=(depth, n // tn),
        in_specs=[pl.BlockSpec((SUBLANES, d), lambda l, j: (0, 0)),
                  pl.BlockSpec((None, d, tn), lambda l, j: (l, 0, j)),
                  pl.BlockSpec((None, 1, tn), lambda l, j: (l, 0, j))],
        out_specs=pl.BlockSpec((None, SUBLANES, tn), lambda l, j: (l, 0, j)),
        out_shape=jax.ShapeDtypeStruct((depth, SUBLANES, n), F32),
        compiler_params=_params(2),
        name="adaln_modulation",
    )(cvec, w_mod, b_mod.reshape(depth, 1, n))


def _post_mixer(alpha, x, y, mod_ref, lng_ref, lnb_ref, wr_ref, br_ref,
                x1_ref, h2_ref, te_ref, tg_ref, cnt_ref, cnt_scr):
    i = pl.program_id(0)
    x1 = _ln(alpha * x + mod_ref[2:3, :] * y, lng_ref[...], lnb_ref[...])
    x1_ref[...] = x1
    h2 = x1 * (1.0 + mod_ref[4:5, :]) + mod_ref[3:4, :]
    h2_ref[...] = h2
    logits = _dot(h2.astype(BF16), wr_ref[...]) + br_ref[...]
    n_e = cnt_scr.shape[0]
    lt = logits.T[:n_e]
    tt = lt.shape[1]
    eidx = lax.broadcasted_iota(jnp.int32, lt.shape, 0)
    vals, idxs = [], []
    for _ in range(TOP_K):
        m = jnp.max(lt, axis=0, keepdims=True)
        idx = jnp.min(jnp.where(lt == m, eidx, n_e), axis=0, keepdims=True)
        vals.append(m)
        idxs.append(idx)
        lt = jnp.where(eidx == idx, -jnp.inf, lt)
    exps = [jnp.exp(v - vals[0]) for v in vals]
    den = exps[0]
    for e in exps[1:]:
        den = den + e

    @pl.when(i == 0)
    def _():
        cnt_scr[...] = jnp.zeros(cnt_scr.shape, cnt_scr.dtype)

    onehot = jnp.zeros(lt.shape, F32)
    for k in range(TOP_K):
        onehot = onehot + (eidx == idxs[k]).astype(F32)
    row = lax.broadcasted_iota(jnp.int32, (tt, tt), 0)
    col = lax.broadcasted_iota(jnp.int32, (tt, tt), 1)
    before = _dot(onehot.astype(BF16), (row < col).astype(BF16)) + cnt_scr[:, 0:1]
    cnt = cnt_scr[...] + jnp.sum(onehot, axis=1, keepdims=True)
    cnt_scr[...] = cnt
    cnt_ref[...] = cnt.astype(jnp.int32)

    row_te = lax.broadcasted_iota(jnp.int32, te_ref.shape, 0)
    row_tg = lax.broadcasted_iota(jnp.int32, (LANES, tt), 0)
    te = jnp.zeros(te_ref.shape, jnp.int32)
    tg = jnp.zeros((LANES, tt), F32)
    for k in range(TOP_K):
        rank = jnp.sum(jnp.where(eidx == idxs[k], before, 0.0), axis=0, keepdims=True)
        te = jnp.where(row_te == k, idxs[k], te)
        te = jnp.where(row_te == TOP_K + k, rank.astype(jnp.int32), te)
        tg = jnp.where(row_tg == k, exps[k] / den, tg)
    te_ref[...] = te
    tg_ref[...] = tg.T


def _expert_rows(n_experts):
    return -(-n_experts // SUBLANES) * SUBLANES


def _epilogue_specs(l, d, n_experts):
    in_specs = [pl.BlockSpec((None, None, 1, d), lambda i, s, p, n: (l, 0, 0, 0)),
                pl.BlockSpec((None, None, 1, d), lambda i, s, p, n: (l, 0, 0, 0)),
                pl.BlockSpec((None, d, LANES), lambda i, s, p, n: (l, 0, 0)),
                pl.BlockSpec((None, 1, LANES), lambda i, s, p, n: (l, 0, 0))]
    out_specs = [pl.BlockSpec((TOKEN_TILE, d), lambda i, s, p, n: (i, 0)),
                 pl.BlockSpec((TOKEN_TILE, d), lambda i, s, p, n: (i, 0)),
                 pl.BlockSpec((2 * TOP_K, TOKEN_TILE), lambda i, s, p, n: (0, i)),
                 pl.BlockSpec((TOKEN_TILE, LANES), lambda i, s, p, n: (i, 0)),
                 pl.BlockSpec((_expert_rows(n_experts), LANES), lambda i, s, p, n: (0, 0))]
    return in_specs, out_specs


def _epilogue_out_shapes(t, d, n_experts):
    return [jax.ShapeDtypeStruct((t, d), F32), jax.ShapeDtypeStruct((t, d), F32),
            jax.ShapeDtypeStruct((2 * TOP_K, t), jnp.int32), jax.ShapeDtypeStruct((t, LANES), F32),
            jax.ShapeDtypeStruct((_expert_rows(n_experts), LANES), jnp.int32)]


def _epilogue_scratch(n_experts):
    return [pltpu.VMEM((_expert_rows(n_experts), LANES), F32)]


def _even_body(alpha, seg_ref, prev_ref, next_ref,
               x_ref, xp_ref, xn_ref, mod_ref, win_ref, sg_ref, sb_ref, wsp_ref, bsp_ref,
               cw_ref, cb_ref, cg_ref, cbb_ref, wout_ref, lng_ref, lnb_ref, wr_ref, br_ref,
               x1_ref, h2_ref, te_ref, tg_ref, cnt_ref, gl_scr, cnt_scr):
    i = pl.program_id(0)
    tt = x_ref.shape[0]
    ca = sg_ref.shape[-1]
    cb2 = 2 * ca
    x = x_ref[...]
    sc = 1.0 + mod_ref[1:2, :]
    sh = mod_ref[0:1, :]
    z = _dot((x * sc + sh).astype(BF16), win_ref[...])

    u = _gelu(z[:, :ca])
    v = _ln(_gelu(z[:, ca:cb2]), sg_ref[...], sb_ref[...]).astype(BF16)
    half = lax.broadcasted_iota(jnp.int32, (CHUNK, LANES), 1) < (LANES // 2)
    chunks = []
    for ck in range(tt // CHUNK):
        cols = []
        for j in range(ca // LANES):
            vblk = v[ck * CHUNK:(ck + 1) * CHUNK, j * LANES:(j + 1) * LANES]
            cols.append(jnp.where(half, _dot(wsp_ref[2 * j], vblk), _dot(wsp_ref[2 * j + 1], vblk)))
        chunks.append(jnp.concatenate(cols, axis=1) + bsp_ref[...])
    y_a = u * jnp.concatenate(chunks, axis=0)

    def glu_rows(xh_ref):
        zh = _dot((xh_ref[...] * sc + sh).astype(BF16), win_ref[:, cb2:])
        return zh[:, :ca] * _sigmoid(zh[:, ca:])

    gl_scr[0:HALO, :] = jnp.where(prev_ref[i] > 0, glu_rows(xp_ref), 0.0)
    gl_scr[HALO:HALO + tt, :] = z[:, cb2:cb2 + ca] * _sigmoid(z[:, cb2 + ca:])
    gl_scr[HALO + tt:, :] = jnp.where(next_ref[i] > 0, glu_rows(xn_ref), 0.0)
    off = HALO - CONV_K // 2
    g_ext = gl_scr[...]
    n_ext = g_ext.shape[0]
    dc = None
    for res in range(SUBLANES):
        taps = [k for k in range(CONV_K) if (off + k) % SUBLANES == res]
        if not taps:
            continue
        shifted = g_ext if res == 0 else pltpu.roll(g_ext, n_ext - res, axis=0)
        for k in taps:
            q = (off + k) // SUBLANES * SUBLANES
            term = shifted[q:q + tt, :] * cw_ref[k:k + 1, :]
            dc = term if dc is None else dc + term
    yb = _ln(dc + cb_ref[...], cg_ref[...], cbb_ref[...])
    y_b = yb * _sigmoid(yb)

    y = _dot(jnp.concatenate([y_a, y_b], axis=1).astype(BF16), wout_ref[...])
    _post_mixer(alpha, x, y, mod_ref, lng_ref, lnb_ref, wr_ref, br_ref,
                x1_ref, h2_ref, te_ref, tg_ref, cnt_ref, cnt_scr)


def _even_layer(l, alpha, meta, x, mod, p):
    t, d = x.shape
    li = l // 2
    nh = TOKEN_TILE // HALO
    n_halo = t // HALO
    ca = p["sgu_g"].shape[-1]
    ep_in, ep_out = _epilogue_specs(l, d, p["n_experts"])
    const3 = lambda i, s, pv, nx: (li, 0, 0)
    in_specs = [
        pl.BlockSpec((TOKEN_TILE, d), lambda i, s, pv, nx: (i, 0)),
        pl.BlockSpec((HALO, d), lambda i, s, pv, nx: (jnp.maximum(i * nh - 1, 0), 0)),
        pl.BlockSpec((HALO, d), lambda i, s, pv, nx: (jnp.minimum((i + 1) * nh, n_halo - 1), 0)),
        pl.BlockSpec((None, None, 6, d), lambda i, s, pv, nx: (l, s[i], 0, 0)),
        pl.BlockSpec((None, d, 4 * ca), const3),
        pl.BlockSpec((None, 1, ca), const3),
        pl.BlockSpec((None, 1, ca), const3),
        pl.BlockSpec((None, G_A, CHUNK, CHUNK), lambda i, s, pv, nx: (li, 0, 0, 0)),
        pl.BlockSpec((None, CHUNK, ca), const3),
        pl.BlockSpec((None, CONV_K, ca), const3),
        pl.BlockSpec((None, 1, ca), const3),
        pl.BlockSpec((None, 1, ca), const3),
        pl.BlockSpec((None, 1, ca), const3),
        pl.BlockSpec((None, 2 * ca, d), const3),
    ] + ep_in
    return pl.pallas_call(
        functools.partial(_even_body, alpha),
        grid_spec=pltpu.PrefetchScalarGridSpec(
            num_scalar_prefetch=3, grid=(t // TOKEN_TILE,),
            in_specs=in_specs, out_specs=ep_out,
            scratch_shapes=[pltpu.VMEM((TOKEN_TILE + 2 * HALO, ca), F32)]
            + _epilogue_scratch(p["n_experts"])),
        out_shape=_epilogue_out_shapes(t, d, p["n_experts"]),
        compiler_params=_params(1),
        name=f"even_mixer_{l}",
    )(meta["seg"], meta["prev"], meta["next"], x, x, x, mod,
      p["w_in"], p["sgu_g"], p["sgu_b"], p["w_sp"], p["b_sp"], p["conv_w"], p["conv_b"],
      p["cln_g"], p["cln_b"], p["w_out_ab"], p["ln_g"], p["ln_b"], p["w_router"], p["b_router"])


def _qkv_body(n_ctx_tiles, seg_ref, x_ref, mod_ref, w_ref, q_ref, k_ref, v_ref, k32_ref, v32_ref):
    _qkv_from(x_ref[...], mod_ref, w_ref, n_ctx_tiles, q_ref, k_ref, v_ref, k32_ref, v32_ref)


def _qkv_from(x, mod_ref, w_ref, n_ctx_tiles, q_ref, k_ref, v_ref, k32_ref, v32_ref):
    d = x.shape[1]
    h = (x * (1.0 + mod_ref[1:2, :]) + mod_ref[0:1, :]).astype(BF16)
    qkv = _dot(h, w_ref[...])
    q_ref[...] = qkv[:, :d].astype(BF16)
    k = qkv[:, d:2 * d]
    v = qkv[:, 2 * d:]
    k_ref[...] = k.astype(BF16)
    v_ref[...] = v.astype(BF16)

    @pl.when(pl.program_id(0) < n_ctx_tiles)
    def _():
        k32_ref[...] = k
        v32_ref[...] = v


def _qkv_proj(l, meta, x, mod, w_qkv, t_ctx):
    t, d = x.shape
    li = l // 2
    n_ctx_tiles = t_ctx // TOKEN_TILE
    tile = pl.BlockSpec((TOKEN_TILE, d), lambda i, s: (i, 0))
    ctx_tile = pl.BlockSpec((TOKEN_TILE, d), lambda i, s: (jnp.minimum(i, n_ctx_tiles - 1), 0))
    return pl.pallas_call(
        functools.partial(_qkv_body, n_ctx_tiles),
        grid_spec=pltpu.PrefetchScalarGridSpec(
            num_scalar_prefetch=1, grid=(t // TOKEN_TILE,),
            in_specs=[tile,
                      pl.BlockSpec((None, None, 6, d), lambda i, s: (l, s[i], 0, 0)),
                      pl.BlockSpec((None, d, 3 * d), lambda i, s: (li, 0, 0))],
            out_specs=[tile] * 3 + [ctx_tile] * 2),
        out_shape=[jax.ShapeDtypeStruct((t, d), BF16)] * 3 + [jax.ShapeDtypeStruct((t_ctx, d), F32)] * 2,
        compiler_params=_params(1),
        name=f"qkv_proj_{l}",
    )(meta["seg"], x, mod, w_qkv)


def _head_pair_attention(q2, k_parts, v_parts, bias_parts, scale):
    m_rows = q2.shape[0]
    lane = lax.broadcasted_iota(jnp.int32, q2.shape, 1)
    qs = q2 * scale
    zero = jnp.zeros_like(q2)
    q_both = jnp.concatenate([jnp.where(lane < LANES // 2, qs, zero),
                              jnp.where(lane >= LANES // 2, qs, zero)], axis=0)
    ss = []
    for j, kp in enumerate(k_parts):
        s = _dot_nt(q_both, kp)
        if bias_parts[0][j] is not None:
            s = s + jnp.concatenate([bias_parts[0][j], bias_parts[1][j]], axis=0)
        ss.append(s)
    m = ss[0].max(axis=-1, keepdims=True)
    for s in ss[1:]:
        m = jnp.maximum(m, s.max(axis=-1, keepdims=True))
    den = None
    o = None
    for s, vp in zip(ss, v_parts):
        e = jnp.exp(s - m)
        es = e.sum(axis=-1, keepdims=True)
        den = es if den is None else den + es
        pv = _dot(e.astype(BF16), vp)
        o = pv if o is None else o + pv
    o = o / den
    return jnp.where(lane < LANES // 2, o[:m_rows], o[m_rows:])


def _attn_ctx_body(scale, q_ref, k_ref, v_ref, o_ref):
    d = q_ref.shape[1]
    for pr in range(d // LANES):
        sl = slice(pr * LANES, (pr + 1) * LANES)
        o = _head_pair_attention(q_ref[:, sl], [k_ref[:, sl]], [v_ref[:, sl]],
                                 [[None], [None]], scale)
        o_ref[:, sl] = o.astype(o_ref.dtype)


def _attn_ctx(q, k, v, n_seq, seq, scale):
    d = q.shape[1]
    blk = pl.BlockSpec((seq, d), lambda b: (b, 0))
    return pl.pallas_call(
        functools.partial(_attn_ctx_body, scale),
        grid=(n_seq,),
        in_specs=[blk, blk, blk],
        out_specs=blk,
        out_shape=jax.ShapeDtypeStruct((n_seq * seq, d), BF16),
        compiler_params=_params(1),
        name="attn_ctx",
    )(q, k, v)


def _attn_lat_body(scale, cls_ref, kb_ref, q_ref, k0_ref, k1_ref, k2_ref, v0_ref, v1_ref, v2_ref,
                   ck_ref, cv_ref, bias_ref, o_ref):
    d = q_ref.shape[1]
    for pr in range(d // LANES):
        sl = slice(pr * LANES, (pr + 1) * LANES)
        k_loc = jnp.concatenate([k0_ref[:, sl], k1_ref[:, sl], k2_ref[:, sl]], axis=0)
        v_loc = jnp.concatenate([v0_ref[:, sl], v1_ref[:, sl], v2_ref[:, sl]], axis=0)
        ck = ck_ref[:, sl].astype(BF16)
        cv = cv_ref[:, sl].astype(BF16)
        o = _head_pair_attention(q_ref[:, sl], [k_loc, ck], [v_loc, cv],
                                 [[bias_ref[2 * pr], None], [bias_ref[2 * pr + 1], None]], scale)
        o_ref[:, sl] = o.astype(o_ref.dtype)


def _latent_window_tables(rows_n):
    wr = min(WIN_ROWS, rows_n)
    n_rt = rows_n // Q_ROWS
    kstart = np.clip(np.arange(n_rt) * Q_ROWS - wr // 2, 0, rows_n - K_ROWS)
    kstart = (kstart // Q_ROWS) * Q_ROWS
    patterns, cls = [], []
    for rt in range(n_rt):
        pat = np.full((Q_ROWS, K_ROWS), -1, np.int64)
        for qi in range(Q_ROWS):
            r = rt * Q_ROWS + qi
            rs = int(np.clip(r - wr // 2, 0, rows_n - wr))
            for kj in range(K_ROWS):
                kr = int(kstart[rt]) + kj
                if rs <= kr < rs + wr:
                    pat[qi, kj] = kr - r + WIN_ROWS - 1
        assert (pat >= 0).sum(axis=1).min() == wr, "key block does not cover the window"
        key = pat.tobytes()
        if key not in [p.tobytes() for p in patterns]:
            patterns.append(pat)
        cls.append([p.tobytes() for p in patterns].index(key))
    return (kstart // Q_ROWS).astype(np.int32), np.asarray(cls, np.int32), np.stack(patterns)


def _latent_bias(rpb, patterns):
    h = rpb.shape[0]
    qc = np.arange(GRID_W)[:, None]
    kc = np.arange(GRID_W)[None, :]
    qcs = np.clip(qc - WIN_COLS // 2, 0, GRID_W - WIN_COLS)
    col_ok = (kc >= qcs) & (kc < qcs + WIN_COLS)
    dc = np.clip(kc - qc + WIN_COLS - 1, 0, 2 * WIN_COLS - 2)
    onehot = (dc[None] == np.arange(2 * WIN_COLS - 1)[:, None, None]) & col_ok[None]
    cm = jnp.einsum("hrd,dqk->hrqk", rpb, jnp.asarray(onehot, F32), precision=lax.Precision.HIGHEST)
    cm = jnp.where(jnp.asarray(col_ok), cm, NEG_INF)
    cx = jnp.concatenate([cm, jnp.full((h, 1, GRID_W, GRID_W), NEG_INF, F32)], axis=1)
    cx2 = jnp.concatenate([cx, cx], axis=-1)
    idx = np.where(patterns >= 0, patterns, 2 * WIN_ROWS - 1)
    n_cls = idx.shape[0]
    assert 2 * GRID_W == LANES and K_ROWS % 2 == 0

    def body(idx_ref, cx_ref, o_ref):
        c = pl.program_id(0)
        low = lax.broadcasted_iota(jnp.int32, (GRID_W, LANES), 1) < GRID_W
        for qi in range(Q_ROWS):
            for m in range(K_ROWS // 2):
                base = (c * Q_ROWS + qi) * K_ROWS + 2 * m
                blk = jnp.where(low, cx_ref[idx_ref[base]], cx_ref[idx_ref[base + 1]])
                o_ref[qi * GRID_W:(qi + 1) * GRID_W, m * LANES:(m + 1) * LANES] = blk

    return pl.pallas_call(
        body,
        grid_spec=pltpu.PrefetchScalarGridSpec(
            num_scalar_prefetch=1, grid=(n_cls, h),
            in_specs=[pl.BlockSpec((None, cx2.shape[1], GRID_W, LANES), lambda c, j, ix: (j, 0, 0, 0))],
            out_specs=pl.BlockSpec((None, None, Q_ROWS * GRID_W, K_ROWS * GRID_W),
                                   lambda c, j, ix: (c, j, 0, 0))),
        out_shape=jax.ShapeDtypeStruct((n_cls, h, Q_ROWS * GRID_W, K_ROWS * GRID_W), F32),
        compiler_params=_params(2),
        name="latent_bias_table",
    )(jnp.asarray(idx.reshape(-1), jnp.int32), cx2)


def _attn_lat(q, k, v, ck, cv, bias, tables, tok0, n_batch, n_tok, scale):
    d = q.shape[1]
    kblk, cls, _ = tables
    n_rt = kblk.shape[0]
    qt = Q_ROWS * GRID_W
    base = tok0 // qt
    per_b = n_tok // qt
    h = bias.shape[1]
    lc = ck.shape[1]

    def kv_spec(j):
        return pl.BlockSpec((qt, d), lambda b, r, c, kb: (base + b * per_b + kb[r] + j, 0))

    return pl.pallas_call(
        functools.partial(_attn_lat_body, scale),
        grid_spec=pltpu.PrefetchScalarGridSpec(
            num_scalar_prefetch=2, grid=(n_batch, n_rt),
            in_specs=[pl.BlockSpec((qt, d), lambda b, r, c, kb: (base + b * per_b + r, 0)),
                      kv_spec(0), kv_spec(1), kv_spec(2), kv_spec(0), kv_spec(1), kv_spec(2),
                      pl.BlockSpec((None, lc, d), lambda b, r, c, kb: (b, 0, 0)),
                      pl.BlockSpec((None, lc, d), lambda b, r, c, kb: (b, 0, 0)),
                      pl.BlockSpec((None, h, qt, K_ROWS * GRID_W), lambda b, r, c, kb: (c[r], 0, 0, 0))],
            out_specs=pl.BlockSpec((qt, d), lambda b, r, c, kb: (b * per_b + r, 0))),
        out_shape=jax.ShapeDtypeStruct((n_batch * n_tok, d), BF16),
        compiler_params=_params(2),
        name="attn_latent",
    )(jnp.asarray(cls), jnp.asarray(kblk), q, k, k, k, v, v, v, ck, cv, bias)


def _proj_body(alpha, n_ctx_tiles, seg_ref, prev_ref, next_ref, x_ref, oc_ref, ol_ref, mod_ref, w_ref,
               lng_ref, lnb_ref, wr_ref, br_ref, x1_ref, h2_ref, te_ref, tg_ref, cnt_ref, cnt_scr):
    o = jnp.where(pl.program_id(0) < n_ctx_tiles, oc_ref[...], ol_ref[...])
    y = _dot(o, w_ref[...])
    _post_mixer(alpha, x_ref[...], y, mod_ref, lng_ref, lnb_ref, wr_ref, br_ref,
                x1_ref, h2_ref, te_ref, tg_ref, cnt_ref, cnt_scr)


def _odd_out_proj(l, alpha, meta, x, o_ctx, o_lat, mod, p):
    t, d = x.shape
    li = l // 2
    nc = o_ctx.shape[0] // TOKEN_TILE
    ep_in, ep_out = _epilogue_specs(l, d, p["n_experts"])
    tile = pl.BlockSpec((TOKEN_TILE, d), lambda i, s, pv, nx: (i, 0))
    return pl.pallas_call(
        functools.partial(_proj_body, alpha, nc),
        grid_spec=pltpu.PrefetchScalarGridSpec(
            num_scalar_prefetch=3, grid=(t // TOKEN_TILE,),
            in_specs=[tile,
                      pl.BlockSpec((TOKEN_TILE, d), lambda i, s, pv, nx: (jnp.minimum(i, nc - 1), 0)),
                      pl.BlockSpec((TOKEN_TILE, d), lambda i, s, pv, nx: (jnp.maximum(i - nc, 0), 0)),
                      pl.BlockSpec((None, None, 6, d), lambda i, s, pv, nx: (l, s[i], 0, 0)),
                      pl.BlockSpec((None, d, d), lambda i, s, pv, nx: (li, 0, 0))] + ep_in,
            out_specs=ep_out, scratch_shapes=_epilogue_scratch(p["n_experts"])),
        out_shape=_epilogue_out_shapes(t, d, p["n_experts"]),
        compiler_params=_params(1),
        name=f"attn_out_proj_{l}",
    )(meta["seg"], meta["prev"], meta["next"], x, o_ctx, o_lat, mod, p["w_out_c"],
      p["ln_g"], p["ln_b"], p["w_router"], p["b_router"])


def _route_tables(te, counts, n_experts):
    t = te.shape[1]
    counts = counts[:n_experts, 0]
    padded = (counts + MOE_TILE - 1) // MOE_TILE * MOE_TILE
    pad_end = jnp.cumsum(padded).astype(jnp.int32)
    pad_start = pad_end - padded
    experts, ranks = te[:TOP_K], te[TOP_K:]
    eid = jnp.arange(n_experts, dtype=jnp.int32)
    sel = experts[None] == eid[:, None, None]
    pos = ranks + jnp.sum(jnp.where(sel, pad_start[:, None, None], 0), axis=0)
    pos = jnp.transpose(pos.reshape(TOP_K, t // TOKEN_TILE, TOKEN_TILE), (1, 2, 0))
    pos = pos.astype(jnp.int32).reshape(t // TOKEN_TILE, 1, TOKEN_TILE * TOP_K)
    return pad_end, pos


def _dispatch_body(pend_ref, pos_ref, h2_ref, xs_hbm, stage, sem, zsem):
    i = pl.program_id(0)
    n_steps = pl.num_programs(0)
    slot = i % 2
    tt = h2_ref.shape[0]
    n_e = pend_ref.shape[0]

    def scatter_wait(s):
        for _ in range(TOP_K):
            pltpu.make_async_copy(stage.at[s], xs_hbm.at[pl.ds(0, tt)], sem.at[s]).wait()

    @pl.when(i == 0)
    def _():
        stage[0] = jnp.zeros(stage.shape[1:], stage.dtype)
        n_tiles = xs_hbm.shape[0] // tt
        n_used = pend_ref[n_e - 1] // tt
        for phase in range(2):
            for e in range(n_e):
                lo = pend_ref[e - 1] if e else 0
                for cond, row0 in ((pend_ref[e] > lo, pend_ref[e] - tt),
                                   (n_used + e < n_tiles, (n_used + e) * tt)):
                    @pl.when(cond)
                    def _():
                        cp = pltpu.make_async_copy(
                            stage.at[0], xs_hbm.at[pl.ds(pl.multiple_of(row0, tt), tt)], zsem)
                        if phase == 0:
                            cp.start()
                        else:
                            cp.wait()

    @pl.when(i >= 2)
    def _():
        scatter_wait(slot)

    stage[slot] = h2_ref[...].reshape(stage.shape[1:])

    def body(r, c):
        for k in range(TOP_K):
            pltpu.make_async_copy(stage.at[slot, r], xs_hbm.at[pos_ref[0, r * TOP_K + k]],
                                  sem.at[slot]).start(priority=k % 2)
        return c
    lax.fori_loop(0, tt, body, 0, unroll=8)

    @pl.when(i == n_steps - 1)
    def _():
        @pl.when(n_steps >= 2)
        def _():
            scatter_wait(1 - slot)
        scatter_wait(slot)


def _moe_dispatch(l, h2, pad_end, pos, n_tiles):
    t, d = h2.shape
    assert TOKEN_TILE == MOE_TILE
    return pl.pallas_call(
        _dispatch_body,
        grid_spec=pltpu.PrefetchScalarGridSpec(
            num_scalar_prefetch=1, grid=(t // TOKEN_TILE,),
            in_specs=[pl.BlockSpec((None, 1, TOKEN_TILE * TOP_K), lambda i, pe: (i, 0, 0),
                                   memory_space=pltpu.SMEM),
                      pl.BlockSpec((TOKEN_TILE, d), lambda i, pe: (i, 0))],
            out_specs=pl.BlockSpec(memory_space=pl.ANY),
            scratch_shapes=[pltpu.VMEM((2, TOKEN_TILE, SUBLANES, LANES), F32),
                            pltpu.SemaphoreType.DMA((2,)),
                            pltpu.SemaphoreType.DMA]),
        out_shape=jax.ShapeDtypeStruct((n_tiles * MOE_TILE, SUBLANES, LANES), F32),
        compiler_params=_params(1),
        name=f"moe_dispatch_{l}",
    )(pad_end, pos, h2)


def _moe_body(l, pend_ref, bgu_ref, bd_ref, xs_hbm, wgu_hbm, wd_hbm, yb_hbm,
              xbuf, ybuf, wgu_st, wd_st, wgu_bf, wd_bf, xsem, ysem, wsem):
    e = pl.program_id(0)
    n_e = pl.num_programs(0)
    tm = xbuf.shape[1]
    de = wd_bf.shape[0]
    par = e % 2
    g_lo = jnp.where(e == 0, 0, pend_ref[jnp.maximum(e - 1, 0)]) // tm
    g_hi = pend_ref[e] // tm
    n_used = pend_ref[n_e - 1] // tm
    n_tiles = yb_hbm.shape[0] // tm

    def weight_copies(ex, p):
        return (pltpu.make_async_copy(wgu_hbm.at[l, ex], wgu_st.at[p], wsem.at[p, 0]),
                pltpu.make_async_copy(wd_hbm.at[l, ex], wd_st.at[p], wsem.at[p, 1]))

    def x_copy(g, s):
        return pltpu.make_async_copy(xs_hbm.at[pl.ds(pl.multiple_of(g * tm, tm), tm)], xbuf.at[s],
                                     xsem.at[s])

    def y_copy(g, s):
        return pltpu.make_async_copy(ybuf.at[s], yb_hbm.at[pl.ds(pl.multiple_of(g * tm, tm), tm)],
                                     ysem.at[s])

    @pl.when(e == 0)
    def _():
        for cp in weight_copies(0, 0):
            cp.start(priority=1)

        @pl.when(n_used > 0)
        def _():
            x_copy(0, 0).start()

    @pl.when(e + 1 < n_e)
    def _():
        for cp in weight_copies(e + 1, 1 - par):
            cp.start(priority=1)

    for cp in weight_copies(e, par):
        cp.wait()

    @pl.when(g_hi > g_lo)
    def _():
        wgu_bf[...] = wgu_st[par].astype(BF16)
        wd_bf[...] = wd_st[par].astype(BF16)

    def tile_body(g, carry):
        s = g % 2
        x_copy(g, s).wait()

        @pl.when(g + 1 < n_used)
        def _():
            x_copy(g + 1, 1 - s).start()

        x = xbuf[s].reshape(tm, wgu_bf.shape[0])
        hgu = _dot(x.astype(BF16), wgu_bf[...]) + bgu_ref[...]
        x_glu = jnp.minimum(hgu[:, :de], SWIGLU_LIMIT)
        x_lin = jnp.clip(hgu[:, de:], -SWIGLU_LIMIT, SWIGLU_LIMIT)
        act = x_glu * _sigmoid(SWIGLU_ALPHA * x_glu) * (x_lin + 1.0)
        y = _dot(act.astype(BF16), wd_bf[...]) + bd_ref[...]

        @pl.when(g >= 2)
        def _():
            y_copy(g - 2, s).wait()

        ybuf[s] = y.reshape(ybuf.shape[1:])
        y_copy(g, s).start()
        return carry

    lax.fori_loop(g_lo, g_hi, tile_body, 0)

    @pl.when(e == n_e - 1)
    def _():
        @pl.when(n_used >= 2)
        def _():
            y_copy(n_used - 2, n_used % 2).wait()

        @pl.when(n_used >= 1)
        def _():
            y_copy(n_used - 1, (n_used + 1) % 2).wait()

        ybuf[0] = jnp.zeros(ybuf.shape[1:], ybuf.dtype)
        for phase in range(2):
            for m in range(n_e):
                @pl.when(n_used + m < n_tiles)
                def _():
                    cp = y_copy(n_used + m, 0)
                    if phase == 0:
                        cp.start()
                    else:
                        cp.wait()


def _moe_experts(l, xs, pad_end, w_gu, b_gu, w_d, b_d):
    n_e, de, d = w_d.shape[1], w_d.shape[2], w_d.shape[3]
    row_tile = (MOE_TILE,) + xs.shape[1:]
    return pl.pallas_call(
        functools.partial(_moe_body, l),
        grid_spec=pltpu.PrefetchScalarGridSpec(
            num_scalar_prefetch=1, grid=(n_e,),
            in_specs=[pl.BlockSpec((None, None, 1, 2 * de), lambda e, pe: (l, e, 0, 0)),
                      pl.BlockSpec((None, None, 1, d), lambda e, pe: (l, e, 0, 0)),
                      pl.BlockSpec(memory_space=pl.ANY),
                      pl.BlockSpec(memory_space=pl.ANY),
                      pl.BlockSpec(memory_space=pl.ANY)],
            out_specs=pl.BlockSpec(memory_space=pl.ANY),
            scratch_shapes=[pltpu.VMEM((2,) + row_tile, F32), pltpu.VMEM((2,) + row_tile, F32),
                            pltpu.VMEM((2, d, 2 * de), F32), pltpu.VMEM((2, de, d), F32),
                            pltpu.VMEM((d, 2 * de), BF16), pltpu.VMEM((de, d), BF16),
                            pltpu.SemaphoreType.DMA((2,)), pltpu.SemaphoreType.DMA((2,)),
                            pltpu.SemaphoreType.DMA((2, 2))]),
        out_shape=jax.ShapeDtypeStruct(xs.shape, F32),
        compiler_params=_params(1),
        name=f"moe_experts_{l}",
    )(pad_end, b_gu.reshape(b_gu.shape[0], n_e, 1, 2 * de), b_d.reshape(b_d.shape[0], n_e, 1, d),
      xs, w_gu, w_d)


def _combine_rows(alpha, pos_ref, posn_ref, x1_ref, g_ref, mod_ref, lng_ref, lnb_ref, yb_hbm, buf, sem):
    i = pl.program_id(0)
    n_steps = pl.num_programs(0)
    slot = i % 2
    tt = x1_ref.shape[0]

    def gather_start(idx_ref, s):
        def body(r, c):
            for k in range(TOP_K):
                pltpu.make_async_copy(yb_hbm.at[idx_ref[0, r * TOP_K + k]], buf.at[s, k, r],
                                      sem.at[s]).start(priority=k % 2)
            return c
        lax.fori_loop(0, tt, body, 0, unroll=8)

    @pl.when(i == 0)
    def _():
        gather_start(pos_ref, 0)

    @pl.when(i + 1 < n_steps)
    def _():
        gather_start(posn_ref, 1 - slot)

    for k in range(TOP_K):
        pltpu.make_async_copy(yb_hbm.at[pl.ds(0, tt)], buf.at[slot, k], sem.at[slot]).wait()
    g = g_ref[...]
    y = None
    for k in range(TOP_K):
        part = g[:, k:k + 1] * buf[slot, k].reshape(x1_ref.shape)
        y = part if y is None else y + part
    return _ln(alpha * x1_ref[...] + mod_ref[5:6, :] * y, lng_ref[...], lnb_ref[...])


def _combine_body(alpha, n_ctx_tiles, seg_ref, pos_ref, posn_ref, x1_ref, g_ref, mod_ref, lng_ref,
                  lnb_ref, yb_hbm, o_ref, *rest):
    o2_ref = rest[0] if len(rest) == 3 else None
    buf, sem = rest[-2:]
    i = pl.program_id(0)
    out = _combine_rows(alpha, pos_ref, posn_ref, x1_ref, g_ref, mod_ref, lng_ref, lnb_ref, yb_hbm,
                        buf, sem)
    if o2_ref is None:
        o_ref[...] = out
    else:
        @pl.when(i < n_ctx_tiles)
        def _():
            o_ref[...] = out

        @pl.when(i >= n_ctx_tiles)
        def _():
            o2_ref[...] = out


def _combine_qkv_body(alpha, n_ctx_tiles, seg_ref, pos_ref, posn_ref, x1_ref, g_ref, mod_ref, lng_ref,
                      lnb_ref, modn_ref, w_ref, yb_hbm, o_ref, q_ref, k_ref, v_ref, k32_ref, v32_ref,
                      buf, sem):
    out = _combine_rows(alpha, pos_ref, posn_ref, x1_ref, g_ref, mod_ref, lng_ref, lnb_ref, yb_hbm,
                        buf, sem)
    o_ref[...] = out
    _qkv_from(out, modn_ref, w_ref, n_ctx_tiles, q_ref, k_ref, v_ref, k32_ref, v32_ref)


def _moe_combine(l, alpha, meta, x1, yb, pos, gates, mod, ln_g, ln_b, split_at=None):
    t, d = x1.shape
    nt = t // TOKEN_TILE
    tile = pl.BlockSpec((TOKEN_TILE, d), lambda i, s: (i, 0))
    pos_blk = lambda f: pl.BlockSpec((None, 1, TOKEN_TILE * TOP_K), f, memory_space=pltpu.SMEM)
    if split_at is None:
        nc, out_specs, out_shape = 0, tile, jax.ShapeDtypeStruct((t, d), F32)
    else:
        nc = split_at // TOKEN_TILE
        out_specs = [pl.BlockSpec((TOKEN_TILE, d), lambda i, s: (jnp.minimum(i, nc - 1), 0)),
                     pl.BlockSpec((TOKEN_TILE, d), lambda i, s: (jnp.maximum(i - nc, 0), 0))]
        out_shape = [jax.ShapeDtypeStruct((split_at, d), F32),
                     jax.ShapeDtypeStruct((t - split_at, d), F32)]
    return pl.pallas_call(
        functools.partial(_combine_body, alpha, nc),
        grid_spec=pltpu.PrefetchScalarGridSpec(
            num_scalar_prefetch=1, grid=(nt,),
            in_specs=[pos_blk(lambda i, s: (i, 0, 0)),
                      pos_blk(lambda i, s: (jnp.minimum(i + 1, nt - 1), 0, 0)),
                      tile,
                      pl.BlockSpec((TOKEN_TILE, LANES), lambda i, s: (i, 0)),
                      pl.BlockSpec((None, None, 6, d), lambda i, s: (l, s[i], 0, 0)),
                      pl.BlockSpec((None, None, 1, d), lambda i, s: (l, 1, 0, 0)),
                      pl.BlockSpec((None, None, 1, d), lambda i, s: (l, 1, 0, 0)),
                      pl.BlockSpec(memory_space=pl.ANY)],
            out_specs=out_specs,
            scratch_shapes=[pltpu.VMEM((2, TOP_K, TOKEN_TILE) + yb.shape[1:], F32),
                            pltpu.SemaphoreType.DMA((2,))]),
        out_shape=out_shape,
        compiler_params=_params(1),
        name=f"moe_combine_{l}",
    )(meta["seg"], pos, pos, x1, gates, mod, ln_g, ln_b, yb)


def _moe_combine_qkv(l, alpha, meta, x1, yb, pos, gates, mod, ln_g, ln_b, w_qkv, t_ctx):
    t, d = x1.shape
    nt = t // TOKEN_TILE
    nc = t_ctx // TOKEN_TILE
    li = (l + 1) // 2
    tile = pl.BlockSpec((TOKEN_TILE, d), lambda i, s: (i, 0))
    ctx_tile = pl.BlockSpec((TOKEN_TILE, d), lambda i, s: (jnp.minimum(i, nc - 1), 0))
    pos_blk = lambda f: pl.BlockSpec((None, 1, TOKEN_TILE * TOP_K), f, memory_space=pltpu.SMEM)
    return pl.pallas_call(
        functools.partial(_combine_qkv_body, alpha, nc),
        grid_spec=pltpu.PrefetchScalarGridSpec(
            num_scalar_prefetch=1, grid=(nt,),
            in_specs=[pos_blk(lambda i, s: (i, 0, 0)),
                      pos_blk(lambda i, s: (jnp.minimum(i + 1, nt - 1), 0, 0)),
                      tile,
                      pl.BlockSpec((TOKEN_TILE, LANES), lambda i, s: (i, 0)),
                      pl.BlockSpec((None, None, 6, d), lambda i, s: (l, s[i], 0, 0)),
                      pl.BlockSpec((None, None, 1, d), lambda i, s: (l, 1, 0, 0)),
                      pl.BlockSpec((None, None, 1, d), lambda i, s: (l, 1, 0, 0)),
                      pl.BlockSpec((None, None, 6, d), lambda i, s: (l + 1, s[i], 0, 0)),
                      pl.BlockSpec((None, d, 3 * d), lambda i, s: (li, 0, 0)),
                      pl.BlockSpec(memory_space=pl.ANY)],
            out_specs=[tile] * 4 + [ctx_tile] * 2,
            scratch_shapes=[pltpu.VMEM((2, TOP_K, TOKEN_TILE) + yb.shape[1:], F32),
                            pltpu.SemaphoreType.DMA((2,))]),
        out_shape=[jax.ShapeDtypeStruct((t, d), F32)] + [jax.ShapeDtypeStruct((t, d), BF16)] * 3
        + [jax.ShapeDtypeStruct((t_ctx, d), F32)] * 2,
        compiler_params=_params(1),
        name=f"moe_combine_qkv_{l}",
    )(meta["seg"], pos, pos, x1, gates, mod, ln_g, ln_b, mod, w_qkv, yb)


def _token_meta(n_ctx_seq, seq, n_lat, lat_seq):
    seg, prev, nxt = [], [], []
    for n_seq, length, seg_of in ((n_ctx_seq, seq, lambda b: 0), (n_lat, lat_seq, lambda b: 1 + b)):
        per = length // TOKEN_TILE
        for b in range(n_seq):
            for j in range(per):
                seg.append(seg_of(b))
                prev.append(int(j > 0))
                nxt.append(int(j < per - 1))
    as_i32 = lambda a: jnp.asarray(np.asarray(a, np.int32))
    return {"seg": as_i32(seg), "prev": as_i32(prev), "next": as_i32(nxt)}


def kernel(x_prompt, x_sample, c, cache_k, cache_v, c_ctx, w_mod, b_mod, ln_g, ln_b, w_in_ab, sgu_ln_g, sgu_ln_b, w_spatial, b_spatial, conv_w, conv_b, conv_ln_g, conv_ln_b, w_out_ab, w_qkv, rpb, w_out_c, w_router, b_router, w_gate_up, b_gate_up, w_down, b_down):
    n_ctx_seq, seq, d = x_prompt.shape
    n_lat, lat_seq, _ = x_sample.shape
    depth = w_mod.shape[0]
    n_heads, head_dim = cache_k.shape[3], cache_k.shape[4]
    n_experts = w_router.shape[-1]
    ca = sgu_ln_g.shape[-1]
    n_even, n_odd = w_in_ab.shape[0], w_qkv.shape[0]
    t_ctx, t_lat = n_ctx_seq * seq, n_lat * lat_seq
    t = t_ctx + t_lat
    rows_n = lat_seq // GRID_W
    assert seq % TOKEN_TILE == 0 and lat_seq % TOKEN_TILE == 0 and 1 + n_lat <= SUBLANES
    assert TOKEN_TILE % CHUNK == 0 and HALO >= CONV_K // 2 and ca == w_out_ab.shape[1] // 2
    assert rows_n % Q_ROWS == 0 and rows_n >= K_ROWS and t_ctx % (Q_ROWS * GRID_W) == 0
    assert n_heads * head_dim == d and 2 * head_dim == LANES and n_experts <= LANES
    assert (t * TOP_K) % MOE_TILE == 0
    assert d == SUBLANES * LANES, "MoE rows are moved as one (SUBLANES, LANES) f32 tile each"
    alpha = float((2 * depth) ** 0.25)
    scale = float(head_dim ** -0.5)
    assert np.frexp(scale)[0] == 0.5, "the attention scale is folded into the bf16 queries"
    meta = _token_meta(n_ctx_seq, seq, n_lat, lat_seq)

    x = jnp.concatenate([x_prompt.reshape(t_ctx, d), x_sample.reshape(t_lat, d)], axis=0)
    cvec = jnp.zeros((SUBLANES, d), F32).at[0].set(c_ctx).at[1:1 + n_lat].set(c)
    mod = _modulation(cvec, w_mod, b_mod).reshape(depth, SUBLANES, 6, d)

    pad_e = LANES - n_experts
    common = {
        "n_experts": n_experts,
        "ln_g": ln_g.reshape(depth, 2, 1, d), "ln_b": ln_b.reshape(depth, 2, 1, d),
        "w_router": jnp.pad(w_router, ((0, 0), (0, 0), (0, pad_e))).astype(BF16),
        "b_router": jnp.pad(b_router, ((0, 0), (0, pad_e)), constant_values=PAD_LOGIT).reshape(depth, 1, LANES),
    }
    even = dict(common)
    even.update({
        "w_in": w_in_ab.astype(BF16), "sgu_g": sgu_ln_g.reshape(n_even, 1, ca),
        "sgu_b": sgu_ln_b.reshape(n_even, 1, ca), "w_sp": w_spatial.astype(BF16),
        "b_sp": jnp.repeat(jnp.transpose(b_spatial, (0, 2, 1)), ca // G_A, axis=2),
        "conv_w": conv_w, "conv_b": conv_b.reshape(n_even, 1, ca),
        "cln_g": conv_ln_g.reshape(n_even, 1, ca), "cln_b": conv_ln_b.reshape(n_even, 1, ca),
        "w_out_ab": w_out_ab.astype(BF16)})
    odd = dict(common)
    odd["w_out_c"] = w_out_c.astype(BF16)
    w_qkv_bf = w_qkv.astype(BF16)
    lat_tables = _latent_window_tables(rows_n)
    n_tiles = t * TOP_K // MOE_TILE + n_experts

    new_k, new_v = [], []
    qkv = None
    for l in range(depth):
        i = l // 2
        if l % 2 == 0:
            x1, h2, top_e, gates, counts = _even_layer(l, alpha, meta, x, mod, even)
        else:
            if qkv is None:
                qkv = _qkv_proj(l, meta, x, mod, w_qkv_bf, t_ctx)
            q, k, v, k32, v32 = qkv
            new_k.append(k32.reshape(n_ctx_seq, seq, n_heads, head_dim))
            new_v.append(v32.reshape(n_ctx_seq, seq, n_heads, head_dim))
            o_ctx = _attn_ctx(q, k, v, n_ctx_seq, seq, scale)
            bias = _latent_bias(rpb[i], lat_tables[2])
            o_lat = _attn_lat(q, k, v, cache_k[:, i].reshape(n_lat, -1, d),
                              cache_v[:, i].reshape(n_lat, -1, d), bias, lat_tables,
                              t_ctx, n_lat, lat_seq, scale)
            x1, h2, top_e, gates, counts = _odd_out_proj(l, alpha, meta, x, o_ctx, o_lat, mod, odd)
        pad_end, pos = _route_tables(top_e, counts, n_experts)
        xs = _moe_dispatch(l, h2, pad_end, pos, n_tiles)
        yb = _moe_experts(l, xs, pad_end, w_gate_up, b_gate_up, w_down, b_down)
        if l % 2 == 0 and l + 1 < depth:
            x, *qkv = _moe_combine_qkv(l, alpha, meta, x1, yb, pos, gates, mod, common["ln_g"],
                                       common["ln_b"], w_qkv_bf, t_ctx)
        else:
            qkv = None
            x = _moe_combine(l, alpha, meta, x1, yb, pos, gates, mod, common["ln_g"], common["ln_b"],
                             split_at=t_ctx if l == depth - 1 else None)

    y_prompt = x[0].reshape(n_ctx_seq, seq, d)
    y_sample = x[1].reshape(n_lat, lat_seq, d)
    return (y_prompt, y_sample, jnp.stack(new_k, axis=1), jnp.stack(new_v, axis=1))
```

```python
import functools

import numpy as np
import jax
import jax.numpy as jnp
from jax import lax
from jax.experimental import pallas as pl
from jax.experimental.pallas import tpu as pltpu

F32 = jnp.float32
BF16 = jnp.bfloat16

GRID_W = 64
G_A = 8
CHUNK = 128
CONV_K = 31
WIN_ROWS = 8
WIN_COLS = 16
TOP_K = 4
SWIGLU_ALPHA = 1.702
SWIGLU_LIMIT = 7.0
LN_EPS = 1e-5
NEG_INF = -1e30

LANES = 128
SUBLANES = 8
VMEM_LIMIT = 56 * 1024 * 1024

TOKEN_TILE = 256
HALO = 16
MOE_TILE = 256
Q_ROWS = 4
K_ROWS = 12
PAD_LOGIT = -3e38


def _ln(x, g, b):
    mu = jnp.mean(x, axis=-1, keepdims=True)
    xc = x - mu
    var = jnp.mean(xc * xc, axis=-1, keepdims=True)
    return xc * lax.rsqrt(var + LN_EPS) * g + b


def _gelu(x):
    return 0.5 * x * (1.0 + jnp.tanh(0.7978845608028654 * (x + 0.044715 * (x * x * x))))


def _sigmoid(x):
    return jax.nn.sigmoid(x)


def _dot(a, b):
    return jnp.dot(a, b, preferred_element_type=F32)


def _dot_nt(a, b):
    return lax.dot_general(a, b, (((1,), (1,)), ((), ())), preferred_element_type=F32)


def _params(n_axes):
    return pltpu.CompilerParams(dimension_semantics=("arbitrary",) * n_axes,
                                vmem_limit_bytes=VMEM_LIMIT)


def _mod_body(c_ref, w_ref, b_ref, o_ref):
    c = c_ref[...]
    s = (c * _sigmoid(c)).astype(BF16)
    o_ref[...] = _dot(s, w_ref[...].astype(BF16)) + b_ref[...]


def _modulation(cvec, w_mod, b_mod):
    depth, d, n = w_mod.shape
    tn = n // 4
    return pl.pallas_call(
        _mod_body,
        grid=(depth, n // tn),
        in_specs=[pl.BlockSpec((SUBLANES, d), lambda l, j: (0, 0)),
                  pl.BlockSpec((None, d, tn), lambda l, j: (l, 0, j)),
                  pl.BlockSpec((None, 1, tn), lambda l, j: (l, 0, j))],
        out_specs=pl.BlockSpec((None, SUBLANES, tn), lambda l, j: (l, 0, j)),
        out_shape=jax.ShapeDtypeStruct((depth, SUBLANES, n), F32),
        compiler_params=_params(2),
        name="adaln_modulation",
    )(cvec, w_mod, b_mod.reshape(depth, 1, n))


def _post_mixer(alpha, x, y, mod_ref, lng_ref, lnb_ref, wr_ref, br_ref,
                x1_ref, h2_ref, te_ref, tg_ref, cnt_ref, cnt_scr):
    i = pl.program_id(0)
    x1 = _ln(alpha * x + mod_ref[2:3, :] * y, lng_ref[...], lnb_ref[...])
    x1_ref[...] = x1
    h2 = x1 * (1.0 + mod_ref[4:5, :]) + mod_ref[3:4, :]
    h2_ref[...] = h2
    logits = _dot(h2.astype(BF16), wr_ref[...]) + br_ref[...]
    n_e = cnt_scr.shape[0]
    lt = logits.T[:n_e]
    tt = lt.shape[1]
    eidx = lax.broadcasted_iota(jnp.int32, lt.shape, 0)
    vals, idxs = [], []
    for _ in range(TOP_K):
        m = jnp.max(lt, axis=0, keepdims=True)
        idx = jnp.min(jnp.where(lt == m, eidx, n_e), axis=0, keepdims=True)
        vals.append(m)
        idxs.append(idx)
        lt = jnp.where(eidx == idx, -jnp.inf, lt)
    exps = [jnp.exp(v - vals[0]) for v in vals]
    den = exps[0]
    for e in exps[1:]:
        den = den + e

    @pl.when(i == 0)
    def _():
        cnt_scr[...] = jnp.zeros(cnt_scr.shape, cnt_scr.dtype)

    onehot = jnp.zeros(lt.shape, F32)
    for k in range(TOP_K):
        onehot = onehot + (eidx == idxs[k]).astype(F32)
    row = lax.broadcasted_iota(jnp.int32, (tt, tt), 0)
    col = lax.broadcasted_iota(jnp.int32, (tt, tt), 1)
    before = _dot(onehot.astype(BF16), (row < col).astype(BF16)) + cnt_scr[:, 0:1]
    cnt = cnt_scr[...] + jnp.sum(onehot, axis=1, keepdims=True)
    cnt_scr[...] = cnt
    cnt_ref[...] = cnt.astype(jnp.int32)

    row_te = lax.broadcasted_iota(jnp.int32, te_ref.shape, 0)
    row_tg = lax.broadcasted_iota(jnp.int32, (LANES, tt), 0)
    te = jnp.zeros(te_ref.shape, jnp.int32)
    tg = jnp.zeros((LANES, tt), F32)
    for k in range(TOP_K):
        rank = jnp.sum(jnp.where(eidx == idxs[k], before, 0.0), axis=0, keepdims=True)
        te = jnp.where(row_te == k, idxs[k], te)
        te = jnp.where(row_te == TOP_K + k, rank.astype(jnp.int32), te)
        tg = jnp.where(row_tg == k, exps[k] / den, tg)
    te_ref[...] = te
    tg_ref[...] = tg.T


def _expert_rows(n_experts):
    return -(-n_experts // SUBLANES) * SUBLANES


def _epilogue_specs(l, d, n_experts):
    in_specs = [pl.BlockSpec((None, None, 1, d), lambda i, s, p, n: (l, 0, 0, 0)),
                pl.BlockSpec((None, None, 1, d), lambda i, s, p, n: (l, 0, 0, 0)),
                pl.BlockSpec((None, d, LANES), lambda i, s, p, n: (l, 0, 0)),
                pl.BlockSpec((None, 1, LANES), lambda i, s, p, n: (l, 0, 0))]
    out_specs = [pl.BlockSpec((TOKEN_TILE, d), lambda i, s, p, n: (i, 0)),
                 pl.BlockSpec((TOKEN_TILE, d), lambda i, s, p, n: (i, 0)),
                 pl.BlockSpec((2 * TOP_K, TOKEN_TILE), lambda i, s, p, n: (0, i)),
                 pl.BlockSpec((TOKEN_TILE, LANES), lambda i, s, p, n: (i, 0)),
                 pl.BlockSpec((_expert_rows(n_experts), LANES), lambda i, s, p, n: (0, 0))]
    return in_specs, out_specs


def _epilogue_out_shapes(t, d, n_experts):
    return [jax.ShapeDtypeStruct((t, d), F32), jax.ShapeDtypeStruct((t, d), F32),
            jax.ShapeDtypeStruct((2 * TOP_K, t), jnp.int32), jax.ShapeDtypeStruct((t, LANES), F32),
            jax.ShapeDtypeStruct((_expert_rows(n_experts), LANES), jnp.int32)]


def _epilogue_scratch(n_experts):
    return [pltpu.VMEM((_expert_rows(n_experts), LANES), F32)]


def _even_body(alpha, seg_ref, prev_ref, next_ref,
               x_ref, xp_ref, xn_ref, mod_ref, win_ref, sg_ref, sb_ref, wsp_ref, bsp_ref,
               cw_ref, cb_ref, cg_ref, cbb_ref, wout_ref, lng_ref, lnb_ref, wr_ref, br_ref,
               x1_ref, h2_ref, te_ref, tg_ref, cnt_ref, gl_scr, cnt_scr):
    i = pl.program_id(0)
    tt = x_ref.shape[0]
    ca = sg_ref.shape[-1]
    cb2 = 2 * ca
    x = x_ref[...]
    sc = 1.0 + mod_ref[1:2, :]
    sh = mod_ref[0:1, :]
    z = _dot((x * sc + sh).astype(BF16), win_ref[...])

    u = _gelu(z[:, :ca])
    v = _ln(_gelu(z[:, ca:cb2]), sg_ref[...], sb_ref[...]).astype(BF16)
    half = lax.broadcasted_iota(jnp.int32, (CHUNK, LANES), 1) < (LANES // 2)
    chunks = []
    for ck in range(tt // CHUNK):
        cols = []
        for j in range(ca // LANES):
            vblk = v[ck * CHUNK:(ck + 1) * CHUNK, j * LANES:(j + 1) * LANES]
            cols.append(jnp.where(half, _dot(wsp_ref[2 * j], vblk), _dot(wsp_ref[2 * j + 1], vblk)))
        chunks.append(jnp.concatenate(cols, axis=1) + bsp_ref[...])
    y_a = u * jnp.concatenate(chunks, axis=0)

    def glu_rows(xh_ref):
        zh = _dot((xh_ref[...] * sc + sh).astype(BF16), win_ref[:, cb2:])
        return zh[:, :ca] * _sigmoid(zh[:, ca:])

    gl_scr[0:HALO, :] = jnp.where(prev_ref[i] > 0, glu_rows(xp_ref), 0.0)
    gl_scr[HALO:HALO + tt, :] = z[:, cb2:cb2 + ca] * _sigmoid(z[:, cb2 + ca:])
    gl_scr[HALO + tt:, :] = jnp.where(next_ref[i] > 0, glu_rows(xn_ref), 0.0)
    off = HALO - CONV_K // 2
    g_ext = gl_scr[...]
    n_ext = g_ext.shape[0]
    dc = None
    for res in range(SUBLANES):
        taps = [k for k in range(CONV_K) if (off + k) % SUBLANES == res]
        if not taps:
            continue
        shifted = g_ext if res == 0 else pltpu.roll(g_ext, n_ext - res, axis=0)
        for k in taps:
            q = (off + k) // SUBLANES * SUBLANES
            term = shifted[q:q + tt, :] * cw_ref[k:k + 1, :]
            dc = term if dc is None else dc + term
    yb = _ln(dc + cb_ref[...], cg_ref[...], cbb_ref[...])
    y_b = yb * _sigmoid(yb)

    y = _dot(jnp.concatenate([y_a, y_b], axis=1).astype(BF16), wout_ref[...])
    _post_mixer(alpha, x, y, mod_ref, lng_ref, lnb_ref, wr_ref, br_ref,
                x1_ref, h2_ref, te_ref, tg_ref, cnt_ref, cnt_scr)


def _even_layer(l, alpha, meta, x, mod, p):
    t, d = x.shape
    li = l // 2
    nh = TOKEN_TILE // HALO
    n_halo = t // HALO
    ca = p["sgu_g"].shape[-1]
    ep_in, ep_out = _epilogue_specs(l, d, p["n_experts"])
    const3 = lambda i, s, pv, nx: (li, 0, 0)
    in_specs = [
        pl.BlockSpec((TOKEN_TILE, d), lambda i, s, pv, nx: (i, 0)),
        pl.BlockSpec((HALO, d), lambda i, s, pv, nx: (jnp.maximum(i * nh - 1, 0), 0)),
        pl.BlockSpec((HALO, d), lambda i, s, pv, nx: (jnp.minimum((i + 1) * nh, n_halo - 1), 0)),
        pl.BlockSpec((None, None, 6, d), lambda i, s, pv, nx: (l, s[i], 0, 0)),
        pl.BlockSpec((None, d, 4 * ca), const3),
        pl.BlockSpec((None, 1, ca), const3),
        pl.BlockSpec((None, 1, ca), const3),
        pl.BlockSpec((None, G_A, CHUNK, CHUNK), lambda i, s, pv, nx: (li, 0, 0, 0)),
        pl.BlockSpec((None, CHUNK, ca), const3),
        pl.BlockSpec((None, CONV_K, ca), const3),
        pl.BlockSpec((None, 1, ca), const3),
        pl.BlockSpec((None, 1, ca), const3),
        pl.BlockSpec((None, 1, ca), const3),
        pl.BlockSpec((None, 2 * ca, d), const3),
    ] + ep_in
    return pl.pallas_call(
        functools.partial(_even_body, alpha),
        grid_spec=pltpu.PrefetchScalarGridSpec(
            num_scalar_prefetch=3, grid=(t // TOKEN_TILE,),
            in_specs=in_specs, out_specs=ep_out,
            scratch_shapes=[pltpu.VMEM((TOKEN_TILE + 2 * HALO, ca), F32)]
            + _epilogue_scratch(p["n_experts"])),
        out_shape=_epilogue_out_shapes(t, d, p["n_experts"]),
        compiler_params=_params(1),
        name=f"even_mixer_{l}",
    )(meta["seg"], meta["prev"], meta["next"], x, x, x, mod,
      p["w_in"], p["sgu_g"], p["sgu_b"], p["w_sp"], p["b_sp"], p["conv_w"], p["conv_b"],
      p["cln_g"], p["cln_b"], p["w_out_ab"], p["ln_g"], p["ln_b"], p["w_router"], p["b_router"])


def _qkv_from(x, mod_ref, w_ref, n_ctx_tiles, q_ref, k_ref, v_ref, k32_ref, v32_ref):
    d = x.shape[1]
    h = (x * (1.0 + mod_ref[1:2, :]) + mod_ref[0:1, :]).astype(BF16)
    qkv = _dot(h, w_ref[...])
    q_ref[...] = qkv[:, :d].astype(BF16)
    k = qkv[:, d:2 * d]
    v = qkv[:, 2 * d:]
    k_ref[...] = k.astype(BF16)
    v_ref[...] = v.astype(BF16)

    @pl.when(pl.program_id(0) < n_ctx_tiles)
    def _():
        for ref, val in ((k32_ref, k), (v32_ref, v)):
            if len(ref.shape) == 3:
                ref[0] = val
                if ref.shape[0] > 1:
                    ref[1:] = jnp.zeros((ref.shape[0] - 1,) + ref.shape[1:], ref.dtype)
            else:
                ref[...] = val


def _head_pair_attention(q2, k_parts, v_parts, bias_parts, scale):
    m_rows = q2.shape[0]
    lane = lax.broadcasted_iota(jnp.int32, q2.shape, 1)
    qs = q2 * scale
    zero = jnp.zeros_like(q2)
    q_both = jnp.concatenate([jnp.where(lane < LANES // 2, qs, zero),
                              jnp.where(lane >= LANES // 2, qs, zero)], axis=0)
    ss = []
    for j, kp in enumerate(k_parts):
        s = _dot_nt(q_both, kp)
        if bias_parts[0][j] is not None:
            s = s + jnp.concatenate([bias_parts[0][j], bias_parts[1][j]], axis=0)
        ss.append(s)
    m = ss[0].max(axis=-1, keepdims=True)
    for s in ss[1:]:
        m = jnp.maximum(m, s.max(axis=-1, keepdims=True))
    den = None
    o = None
    for s, vp in zip(ss, v_parts):
        e = jnp.exp(s - m)
        es = e.sum(axis=-1, keepdims=True)
        den = es if den is None else den + es
        pv = _dot(e.astype(BF16), vp)
        o = pv if o is None else o + pv
    o = o / den
    return jnp.where(lane < LANES // 2, o[:m_rows], o[m_rows:])


def _attn_ctx_body(scale, q_ref, k_ref, v_ref, o_ref):
    d = q_ref.shape[1]
    for pr in range(d // LANES):
        sl = slice(pr * LANES, (pr + 1) * LANES)
        o = _head_pair_attention(q_ref[:, sl], [k_ref[:, sl]], [v_ref[:, sl]],
                                 [[None], [None]], scale)
        o_ref[:, sl] = o.astype(o_ref.dtype)


def _attn_ctx(q, k, v, n_seq, seq, scale):
    d = q.shape[1]
    blk = pl.BlockSpec((seq, d), lambda b: (b, 0))
    return pl.pallas_call(
        functools.partial(_attn_ctx_body, scale),
        grid=(n_seq,),
        in_specs=[blk, blk, blk],
        out_specs=blk,
        out_shape=jax.ShapeDtypeStruct((n_seq * seq, d), BF16),
        compiler_params=_params(1),
        name="attn_ctx",
    )(q, k, v)


def _attn_lat_body(scale, cls_ref, kb_ref, q_ref, k0_ref, k1_ref, k2_ref, v0_ref, v1_ref, v2_ref,
                   ck_ref, cv_ref, bias_ref, o_ref):
    d = q_ref.shape[1]
    for pr in range(d // LANES):
        sl = slice(pr * LANES, (pr + 1) * LANES)
        k_loc = jnp.concatenate([k0_ref[:, sl], k1_ref[:, sl], k2_ref[:, sl]], axis=0)
        v_loc = jnp.concatenate([v0_ref[:, sl], v1_ref[:, sl], v2_ref[:, sl]], axis=0)
        ck = ck_ref[:, sl].astype(BF16)
        cv = cv_ref[:, sl].astype(BF16)
        o = _head_pair_attention(q_ref[:, sl], [k_loc, ck], [v_loc, cv],
                                 [[bias_ref[2 * pr], None], [bias_ref[2 * pr + 1], None]], scale)
        o_ref[:, sl] = o.astype(o_ref.dtype)


def _latent_window_tables(rows_n):
    wr = min(WIN_ROWS, rows_n)
    n_rt = rows_n // Q_ROWS
    kstart = np.clip(np.arange(n_rt) * Q_ROWS - wr // 2, 0, rows_n - K_ROWS)
    kstart = (kstart // Q_ROWS) * Q_ROWS
    patterns, cls = [], []
    for rt in range(n_rt):
        pat = np.full((Q_ROWS, K_ROWS), -1, np.int64)
        for qi in range(Q_ROWS):
            r = rt * Q_ROWS + qi
            rs = int(np.clip(r - wr // 2, 0, rows_n - wr))
            for kj in range(K_ROWS):
                kr = int(kstart[rt]) + kj
                if rs <= kr < rs + wr:
                    pat[qi, kj] = kr - r + WIN_ROWS - 1
        assert (pat >= 0).sum(axis=1).min() == wr, "key block does not cover the window"
        key = pat.tobytes()
        if key not in [p.tobytes() for p in patterns]:
            patterns.append(pat)
        cls.append([p.tobytes() for p in patterns].index(key))
    return (kstart // Q_ROWS).astype(np.int32), np.asarray(cls, np.int32), np.stack(patterns)


def _latent_bias(rpb, patterns):
    h = rpb.shape[0]
    qc = np.arange(GRID_W)[:, None]
    kc = np.arange(GRID_W)[None, :]
    qcs = np.clip(qc - WIN_COLS // 2, 0, GRID_W - WIN_COLS)
    col_ok = (kc >= qcs) & (kc < qcs + WIN_COLS)
    dc = np.clip(kc - qc + WIN_COLS - 1, 0, 2 * WIN_COLS - 2)
    onehot = (dc[None] == np.arange(2 * WIN_COLS - 1)[:, None, None]) & col_ok[None]
    cm = jnp.einsum("hrd,dqk->hrqk", rpb, jnp.asarray(onehot, F32), precision=lax.Precision.HIGHEST)
    cm = jnp.where(jnp.asarray(col_ok), cm, NEG_INF)
    cx = jnp.concatenate([cm, jnp.full((h, 1, GRID_W, GRID_W), NEG_INF, F32)], axis=1)
    cx2 = jnp.concatenate([cx, cx], axis=-1)
    idx = np.where(patterns >= 0, patterns, 2 * WIN_ROWS - 1)
    n_cls = idx.shape[0]
    assert 2 * GRID_W == LANES and K_ROWS % 2 == 0

    def body(idx_ref, cx_ref, o_ref):
        c = pl.program_id(0)
        low = lax.broadcasted_iota(jnp.int32, (GRID_W, LANES), 1) < GRID_W
        for qi in range(Q_ROWS):
            for m in range(K_ROWS // 2):
                base = (c * Q_ROWS + qi) * K_ROWS + 2 * m
                blk = jnp.where(low, cx_ref[idx_ref[base]], cx_ref[idx_ref[base + 1]])
                o_ref[qi * GRID_W:(qi + 1) * GRID_W, m * LANES:(m + 1) * LANES] = blk

    return pl.pallas_call(
        body,
        grid_spec=pltpu.PrefetchScalarGridSpec(
            num_scalar_prefetch=1, grid=(n_cls, h),
            in_specs=[pl.BlockSpec((None, cx2.shape[1], GRID_W, LANES), lambda c, j, ix: (j, 0, 0, 0))],
            out_specs=pl.BlockSpec((None, None, Q_ROWS * GRID_W, K_ROWS * GRID_W),
                                   lambda c, j, ix: (c, j, 0, 0))),
        out_shape=jax.ShapeDtypeStruct((n_cls, h, Q_ROWS * GRID_W, K_ROWS * GRID_W), F32),
        compiler_params=_params(2),
        name="latent_bias_table",
    )(jnp.asarray(idx.reshape(-1), jnp.int32), cx2)


def _attn_lat(q, k, v, ck, cv, li, bias, tables, tok0, n_batch, n_tok, scale):
    d = q.shape[1]
    kblk, cls, _ = tables
    n_rt = kblk.shape[0]
    qt = Q_ROWS * GRID_W
    base = tok0 // qt
    per_b = n_tok // qt
    h = bias.shape[1]
    lc = ck.shape[2]

    def kv_spec(j):
        return pl.BlockSpec((qt, d), lambda b, r, c, kb: (base + b * per_b + kb[r] + j, 0))

    return pl.pallas_call(
        functools.partial(_attn_lat_body, scale),
        grid_spec=pltpu.PrefetchScalarGridSpec(
            num_scalar_prefetch=2, grid=(n_batch, n_rt),
            in_specs=[pl.BlockSpec((qt, d), lambda b, r, c, kb: (base + b * per_b + r, 0)),
                      kv_spec(0), kv_spec(1), kv_spec(2), kv_spec(0), kv_spec(1), kv_spec(2),
                      pl.BlockSpec((None, None, lc, d), lambda b, r, c, kb: (b, li, 0, 0)),
                      pl.BlockSpec((None, None, lc, d), lambda b, r, c, kb: (b, li, 0, 0)),
                      pl.BlockSpec((None, h, qt, K_ROWS * GRID_W), lambda b, r, c, kb: (c[r], 0, 0, 0))],
            out_specs=pl.BlockSpec((qt, d), lambda b, r, c, kb: (b * per_b + r, 0))),
        out_shape=jax.ShapeDtypeStruct((n_batch * n_tok, d), BF16),
        compiler_params=_params(2),
        name="attn_latent",
    )(jnp.asarray(cls), jnp.asarray(kblk), q, k, k, k, v, v, v, ck, cv, bias)


def _proj_body(alpha, n_ctx_tiles, seg_ref, prev_ref, next_ref, x_ref, oc_ref, ol_ref, mod_ref, w_ref,
               lng_ref, lnb_ref, wr_ref, br_ref, x1_ref, h2_ref, te_ref, tg_ref, cnt_ref, cnt_scr):
    o = jnp.where(pl.program_id(0) < n_ctx_tiles, oc_ref[...], ol_ref[...])
    y = _dot(o, w_ref[...])
    _post_mixer(alpha, x_ref[...], y, mod_ref, lng_ref, lnb_ref, wr_ref, br_ref,
                x1_ref, h2_ref, te_ref, tg_ref, cnt_ref, cnt_scr)


def _odd_out_proj(l, alpha, meta, x, o_ctx, o_lat, mod, p):
    t, d = x.shape
    li = l // 2
    nc = o_ctx.shape[0] // TOKEN_TILE
    ep_in, ep_out = _epilogue_specs(l, d, p["n_experts"])
    tile = pl.BlockSpec((TOKEN_TILE, d), lambda i, s, pv, nx: (i, 0))
    return pl.pallas_call(
        functools.partial(_proj_body, alpha, nc),
        grid_spec=pltpu.PrefetchScalarGridSpec(
            num_scalar_prefetch=3, grid=(t // TOKEN_TILE,),
            in_specs=[tile,
                      pl.BlockSpec((TOKEN_TILE, d), lambda i, s, pv, nx: (jnp.minimum(i, nc - 1), 0)),
                      pl.BlockSpec((TOKEN_TILE, d), lambda i, s, pv, nx: (jnp.maximum(i - nc, 0), 0)),
                      pl.BlockSpec((None, None, 6, d), lambda i, s, pv, nx: (l, s[i], 0, 0)),
                      pl.BlockSpec((None, d, d), lambda i, s, pv, nx: (li, 0, 0))] + ep_in,
            out_specs=ep_out, scratch_shapes=_epilogue_scratch(p["n_experts"])),
        out_shape=_epilogue_out_shapes(t, d, p["n_experts"]),
        compiler_params=_params(1),
        name=f"attn_out_proj_{l}",
    )(meta["seg"], meta["prev"], meta["next"], x, o_ctx, o_lat, mod, p["w_out_c"],
      p["ln_g"], p["ln_b"], p["w_router"], p["b_router"])


def _route_tables(te, counts, n_experts):
    t = te.shape[1]
    counts = counts[:n_experts, 0]
    padded = (counts + MOE_TILE - 1) // MOE_TILE * MOE_TILE
    pad_end = jnp.cumsum(padded).astype(jnp.int32)
    pad_start = pad_end - padded
    experts, ranks = te[:TOP_K], te[TOP_K:]
    eid = jnp.arange(n_experts, dtype=jnp.int32)
    sel = experts[None] == eid[:, None, None]
    pos = ranks + jnp.sum(jnp.where(sel, pad_start[:, None, None], 0), axis=0)
    pos = jnp.transpose(pos.reshape(TOP_K, t // TOKEN_TILE, TOKEN_TILE), (1, 2, 0))
    pos = pos.astype(jnp.int32).reshape(t // TOKEN_TILE, 1, TOKEN_TILE * TOP_K)
    return pad_end, pos


def _dispatch_body(pend_ref, pos_ref, h2_ref, xs_hbm, stage, sem, zsem):
    i = pl.program_id(0)
    n_steps = pl.num_programs(0)
    slot = i % 2
    tt = h2_ref.shape[0]
    n_e = pend_ref.shape[0]

    def scatter_wait(s):
        for _ in range(TOP_K):
            pltpu.make_async_copy(stage.at[s], xs_hbm.at[pl.ds(0, tt)], sem.at[s]).wait()

    @pl.when(i == 0)
    def _():
        stage[0] = jnp.zeros(stage.shape[1:], stage.dtype)
        n_tiles = xs_hbm.shape[0] // tt
        n_used = pend_ref[n_e - 1] // tt
        for phase in range(2):
            for e in range(n_e):
                lo = pend_ref[e - 1] if e else 0
                for cond, row0 in ((pend_ref[e] > lo, pend_ref[e] - tt),
                                   (n_used + e < n_tiles, (n_used + e) * tt)):
                    @pl.when(cond)
                    def _():
                        cp = pltpu.make_async_copy(
                            stage.at[0], xs_hbm.at[pl.ds(pl.multiple_of(row0, tt), tt)], zsem)
                        if phase == 0:
                            cp.start()
                        else:
                            cp.wait()

    @pl.when(i >= 2)
    def _():
        scatter_wait(slot)

    stage[slot] = h2_ref[...].reshape(stage.shape[1:])

    def body(r, c):
        for k in range(TOP_K):
            pltpu.make_async_copy(stage.at[slot, r], xs_hbm.at[pos_ref[0, r * TOP_K + k]],
                                  sem.at[slot]).start(priority=k % 2)
        return c
    lax.fori_loop(0, tt, body, 0, unroll=8)

    @pl.when(i == n_steps - 1)
    def _():
        @pl.when(n_steps >= 2)
        def _():
            scatter_wait(1 - slot)
        scatter_wait(slot)


def _moe_dispatch(l, h2, pad_end, pos, n_tiles):
    t, d = h2.shape
    assert TOKEN_TILE == MOE_TILE
    return pl.pallas_call(
        _dispatch_body,
        grid_spec=pltpu.PrefetchScalarGridSpec(
            num_scalar_prefetch=1, grid=(t // TOKEN_TILE,),
            in_specs=[pl.BlockSpec((None, 1, TOKEN_TILE * TOP_K), lambda i, pe: (i, 0, 0),
                                   memory_space=pltpu.SMEM),
                      pl.BlockSpec((TOKEN_TILE, d), lambda i, pe: (i, 0))],
            out_specs=pl.BlockSpec(memory_space=pl.ANY),
            scratch_shapes=[pltpu.VMEM((2, TOKEN_TILE, SUBLANES, LANES), F32),
                            pltpu.SemaphoreType.DMA((2,)),
                            pltpu.SemaphoreType.DMA]),
        out_shape=jax.ShapeDtypeStruct((n_tiles * MOE_TILE, SUBLANES, LANES), F32),
        compiler_params=_params(1),
        name=f"moe_dispatch_{l}",
    )(pad_end, pos, h2)


def _moe_body(l, pend_ref, bgu_ref, bd_ref, xs_hbm, wgu_hbm, wd_hbm, yb_hbm,
              xbuf, ybuf, wgu_st, wd_st, wgu_bf, wd_bf, xsem, ysem, wsem):
    e = pl.program_id(0)
    n_e = pl.num_programs(0)
    tm = xbuf.shape[1]
    de = wd_bf.shape[0]
    par = e % 2
    g_lo = jnp.where(e == 0, 0, pend_ref[jnp.maximum(e - 1, 0)]) // tm
    g_hi = pend_ref[e] // tm
    n_used = pend_ref[n_e - 1] // tm
    n_tiles = yb_hbm.shape[0] // tm

    def weight_copies(ex, p):
        return (pltpu.make_async_copy(wgu_hbm.at[l, ex], wgu_st.at[p], wsem.at[p, 0]),
                pltpu.make_async_copy(wd_hbm.at[l, ex], wd_st.at[p], wsem.at[p, 1]))

    def x_copy(g, s):
        return pltpu.make_async_copy(xs_hbm.at[pl.ds(pl.multiple_of(g * tm, tm), tm)], xbuf.at[s],
                                     xsem.at[s])

    def y_copy(g, s):
        return pltpu.make_async_copy(ybuf.at[s], yb_hbm.at[pl.ds(pl.multiple_of(g * tm, tm), tm)],
                                     ysem.at[s])

    @pl.when(e == 0)
    def _():
        for cp in weight_copies(0, 0):
            cp.start(priority=1)

        @pl.when(n_used > 0)
        def _():
            x_copy(0, 0).start()

    @pl.when(e + 1 < n_e)
    def _():
        for cp in weight_copies(e + 1, 1 - par):
            cp.start(priority=1)

    for cp in weight_copies(e, par):
        cp.wait()

    @pl.when(g_hi > g_lo)
    def _():
        wgu_bf[...] = wgu_st[par].astype(BF16)
        wd_bf[...] = wd_st[par].astype(BF16)

    def tile_body(g, carry):
        s = g % 2
        x_copy(g, s).wait()

        @pl.when(g + 1 < n_used)
        def _():
            x_copy(g + 1, 1 - s).start()

        x = xbuf[s].reshape(tm, wgu_bf.shape[0])
        hgu = _dot(x.astype(BF16), wgu_bf[...]) + bgu_ref[...]
        x_glu = jnp.minimum(hgu[:, :de], SWIGLU_LIMIT)
        x_lin = jnp.clip(hgu[:, de:], -SWIGLU_LIMIT, SWIGLU_LIMIT)
        act = x_glu * _sigmoid(SWIGLU_ALPHA * x_glu) * (x_lin + 1.0)
        y = _dot(act.astype(BF16), wd_bf[...]) + bd_ref[...]

        @pl.when(g >= 2)
        def _():
            y_copy(g - 2, s).wait()

        ybuf[s] = y.reshape(ybuf.shape[1:])
        y_copy(g, s).start()
        return carry

    lax.fori_loop(g_lo, g_hi, tile_body, 0)

    @pl.when(e == n_e - 1)
    def _():
        @pl.when(n_used >= 2)
        def _():
            y_copy(n_used - 2, n_used % 2).wait()

        @pl.when(n_used >= 1)
        def _():
            y_copy(n_used - 1, (n_used + 1) % 2).wait()

        ybuf[0] = jnp.zeros(ybuf.shape[1:], ybuf.dtype)
        for phase in range(2):
            for m in range(n_e):
                @pl.when(n_used + m < n_tiles)
                def _():
                    cp = y_copy(n_used + m, 0)
                    if phase == 0:
                        cp.start()
                    else:
                        cp.wait()


def _moe_experts(l, xs, pad_end, w_gu, b_gu, w_d, b_d):
    n_e, de, d = w_d.shape[1], w_d.shape[2], w_d.shape[3]
    row_tile = (MOE_TILE,) + xs.shape[1:]
    return pl.pallas_call(
        functools.partial(_moe_body, l),
        grid_spec=pltpu.PrefetchScalarGridSpec(
            num_scalar_prefetch=1, grid=(n_e,),
            in_specs=[pl.BlockSpec((None, None, 1, 2 * de), lambda e, pe: (l, e, 0, 0)),
                      pl.BlockSpec((None, None, 1, d), lambda e, pe: (l, e, 0, 0)),
                      pl.BlockSpec(memory_space=pl.ANY),
                      pl.BlockSpec(memory_space=pl.ANY),
                      pl.BlockSpec(memory_space=pl.ANY)],
            out_specs=pl.BlockSpec(memory_space=pl.ANY),
            scratch_shapes=[pltpu.VMEM((2,) + row_tile, F32), pltpu.VMEM((2,) + row_tile, F32),
                            pltpu.VMEM((2, d, 2 * de), F32), pltpu.VMEM((2, de, d), F32),
                            pltpu.VMEM((d, 2 * de), BF16), pltpu.VMEM((de, d), BF16),
                            pltpu.SemaphoreType.DMA((2,)), pltpu.SemaphoreType.DMA((2,)),
                            pltpu.SemaphoreType.DMA((2, 2))]),
        out_shape=jax.ShapeDtypeStruct(xs.shape, F32),
        compiler_params=_params(1),
        name=f"moe_experts_{l}",
    )(pad_end, b_gu.reshape(b_gu.shape[0], n_e, 1, 2 * de), b_d.reshape(b_d.shape[0], n_e, 1, d),
      xs, w_gu, w_d)


def _combine_rows(alpha, pos_ref, posn_ref, x1_ref, g_ref, mod_ref, lng_ref, lnb_ref, yb_hbm, buf, sem):
    i = pl.program_id(0)
    n_steps = pl.num_programs(0)
    slot = i % 2
    tt = x1_ref.shape[0]

    def gather_start(idx_ref, s):
        def body(r, c):
            for k in range(TOP_K):
                pltpu.make_async_copy(yb_hbm.at[idx_ref[0, r * TOP_K + k]], buf.at[s, k, r],
                                      sem.at[s]).start(priority=k % 2)
            return c
        lax.fori_loop(0, tt, body, 0, unroll=8)

    @pl.when(i == 0)
    def _():
        gather_start(pos_ref, 0)

    @pl.when(i + 1 < n_steps)
    def _():
        gather_start(posn_ref, 1 - slot)

    for k in range(TOP_K):
        pltpu.make_async_copy(yb_hbm.at[pl.ds(0, tt)], buf.at[slot, k], sem.at[slot]).wait()
    g = g_ref[...]
    y = None
    for k in range(TOP_K):
        part = g[:, k:k + 1] * buf[slot, k].reshape(x1_ref.shape)
        y = part if y is None else y + part
    return _ln(alpha * x1_ref[...] + mod_ref[5:6, :] * y, lng_ref[...], lnb_ref[...])


def _combine_body(alpha, n_ctx_tiles, seg_ref, pos_ref, posn_ref, x1_ref, g_ref, mod_ref, lng_ref,
                  lnb_ref, yb_hbm, o_ref, *rest):
    o2_ref = rest[0] if len(rest) == 3 else None
    buf, sem = rest[-2:]
    i = pl.program_id(0)
    out = _combine_rows(alpha, pos_ref, posn_ref, x1_ref, g_ref, mod_ref, lng_ref, lnb_ref, yb_hbm,
                        buf, sem)
    if o2_ref is None:
        o_ref[...] = out
    else:
        @pl.when(i < n_ctx_tiles)
        def _():
            o_ref[...] = out

        @pl.when(i >= n_ctx_tiles)
        def _():
            o2_ref[...] = out


def _combine_qkv_body(alpha, n_ctx_tiles, n_aliased, seg_ref, pos_ref, posn_ref, x1_ref, g_ref, mod_ref,
                      lng_ref, lnb_ref, modn_ref, w_ref, yb_hbm, *rest):
    o_ref, q_ref, k_ref, v_ref, k32_ref, v32_ref, buf, sem = rest[n_aliased:]
    out = _combine_rows(alpha, pos_ref, posn_ref, x1_ref, g_ref, mod_ref, lng_ref, lnb_ref, yb_hbm,
                        buf, sem)
    o_ref[...] = out
    _qkv_from(out, modn_ref, w_ref, n_ctx_tiles, q_ref, k_ref, v_ref, k32_ref, v32_ref)


def _moe_combine(l, alpha, meta, x1, yb, pos, gates, mod, ln_g, ln_b, split_at=None):
    t, d = x1.shape
    nt = t // TOKEN_TILE
    tile = pl.BlockSpec((TOKEN_TILE, d), lambda i, s: (i, 0))
    pos_blk = lambda f: pl.BlockSpec((None, 1, TOKEN_TILE * TOP_K), f, memory_space=pltpu.SMEM)
    if split_at is None:
        nc, out_specs, out_shape = 0, tile, jax.ShapeDtypeStruct((t, d), F32)
    else:
        nc = split_at // TOKEN_TILE
        out_specs = [pl.BlockSpec((TOKEN_TILE, d), lambda i, s: (jnp.minimum(i, nc - 1), 0)),
                     pl.BlockSpec((TOKEN_TILE, d), lambda i, s: (jnp.maximum(i - nc, 0), 0))]
        out_shape = [jax.ShapeDtypeStruct((split_at, d), F32),
                     jax.ShapeDtypeStruct((t - split_at, d), F32)]
    return pl.pallas_call(
        functools.partial(_combine_body, alpha, nc),
        grid_spec=pltpu.PrefetchScalarGridSpec(
            num_scalar_prefetch=1, grid=(nt,),
            in_specs=[pos_blk(lambda i, s: (i, 0, 0)),
                      pos_blk(lambda i, s: (jnp.minimum(i + 1, nt - 1), 0, 0)),
                      tile,
                      pl.BlockSpec((TOKEN_TILE, LANES), lambda i, s: (i, 0)),
                      pl.BlockSpec((None, None, 6, d), lambda i, s: (l, s[i], 0, 0)),
                      pl.BlockSpec((None, None, 1, d), lambda i, s: (l, 1, 0, 0)),
                      pl.BlockSpec((None, None, 1, d), lambda i, s: (l, 1, 0, 0)),
                      pl.BlockSpec(memory_space=pl.ANY)],
            out_specs=out_specs,
            scratch_shapes=[pltpu.VMEM((2, TOP_K, TOKEN_TILE) + yb.shape[1:], F32),
                            pltpu.SemaphoreType.DMA((2,))]),
        out_shape=out_shape,
        compiler_params=_params(1),
        name=f"moe_combine_{l}",
    )(meta["seg"], pos, pos, x1, gates, mod, ln_g, ln_b, yb)


def _moe_combine_qkv(l, alpha, meta, x1, yb, pos, gates, mod, ln_g, ln_b, w_qkv, cache_dims, prev_kv):
    t, d = x1.shape
    n_seq, n_odd, seq = cache_dims
    nt = t // TOKEN_TILE
    per_seq = seq // TOKEN_TILE
    nc = n_seq * per_seq
    li = (l + 1) // 2
    tile = pl.BlockSpec((TOKEN_TILE, d), lambda i, s: (i, 0))
    seq_of = lambda i: jnp.minimum(i, nc - 1) // per_seq
    blk_of = lambda i: jnp.minimum(i, nc - 1) % per_seq
    if prev_kv is None:
        ctx_tile = pl.BlockSpec((None, n_odd, TOKEN_TILE, d), lambda i, s: (seq_of(i), 0, blk_of(i), 0))
        extra_in, extra_args, aliases = [], (), {}
    else:
        ctx_tile = pl.BlockSpec((None, None, TOKEN_TILE, d), lambda i, s: (seq_of(i), li, blk_of(i), 0))
        extra_in, extra_args = [pl.BlockSpec(memory_space=pl.ANY)] * 2, tuple(prev_kv)
        aliases = {11: 4, 12: 5}
    pos_blk = lambda f: pl.BlockSpec((None, 1, TOKEN_TILE * TOP_K), f, memory_space=pltpu.SMEM)
    return pl.pallas_call(
        functools.partial(_combine_qkv_body, alpha, nc, len(extra_in)),
        input_output_aliases=aliases,
        grid_spec=pltpu.PrefetchScalarGridSpec(
            num_scalar_prefetch=1, grid=(nt,),
            in_specs=[pos_blk(lambda i, s: (i, 0, 0)),
                      pos_blk(lambda i, s: (jnp.minimum(i + 1, nt - 1), 0, 0)),
                      tile,
                      pl.BlockSpec((TOKEN_TILE, LANES), lambda i, s: (i, 0)),
                      pl.BlockSpec((None, None, 6, d), lambda i, s: (l, s[i], 0, 0)),
                      pl.BlockSpec((None, None, 1, d), lambda i, s: (l, 1, 0, 0)),
                      pl.BlockSpec((None, None, 1, d), lambda i, s: (l, 1, 0, 0)),
                      pl.BlockSpec((None, None, 6, d), lambda i, s: (l + 1, s[i], 0, 0)),
                      pl.BlockSpec((None, d, 3 * d), lambda i, s: (li, 0, 0)),
                      pl.BlockSpec(memory_space=pl.ANY)] + extra_in,
            out_specs=[tile] * 4 + [ctx_tile] * 2,
            scratch_shapes=[pltpu.VMEM((2, TOP_K, TOKEN_TILE) + yb.shape[1:], F32),
                            pltpu.SemaphoreType.DMA((2,))]),
        out_shape=[jax.ShapeDtypeStruct((t, d), F32)] + [jax.ShapeDtypeStruct((t, d), BF16)] * 3
        + [jax.ShapeDtypeStruct((n_seq, n_odd, seq, d), F32)] * 2,
        compiler_params=_params(1),
        name=f"moe_combine_qkv_{l}",
    )(meta["seg"], pos, pos, x1, gates, mod, ln_g, ln_b, mod, w_qkv, yb, *extra_args)


def _token_meta(n_ctx_seq, seq, n_lat, lat_seq):
    seg, prev, nxt = [], [], []
    for n_seq, length, seg_of in ((n_ctx_seq, seq, lambda b: 0), (n_lat, lat_seq, lambda b: 1 + b)):
        per = length // TOKEN_TILE
        for b in range(n_seq):
            for j in range(per):
                seg.append(seg_of(b))
                prev.append(int(j > 0))
                nxt.append(int(j < per - 1))
    as_i32 = lambda a: jnp.asarray(np.asarray(a, np.int32))
    return {"seg": as_i32(seg), "prev": as_i32(prev), "next": as_i32(nxt)}


def kernel(x_prompt, x_sample, c, cache_k, cache_v, c_ctx, w_mod, b_mod, ln_g, ln_b, w_in_ab, sgu_ln_g, sgu_ln_b, w_spatial, b_spatial, conv_w, conv_b, conv_ln_g, conv_ln_b, w_out_ab, w_qkv, rpb, w_out_c, w_router, b_router, w_gate_up, b_gate_up, w_down, b_down):
    n_ctx_seq, seq, d = x_prompt.shape
    n_lat, lat_seq, _ = x_sample.shape
    depth = w_mod.shape[0]
    n_heads, head_dim = cache_k.shape[3], cache_k.shape[4]
    n_experts = w_router.shape[-1]
    ca = sgu_ln_g.shape[-1]
    n_even, n_odd = w_in_ab.shape[0], w_qkv.shape[0]
    t_ctx, t_lat = n_ctx_seq * seq, n_lat * lat_seq
    t = t_ctx + t_lat
    rows_n = lat_seq // GRID_W
    assert seq % TOKEN_TILE == 0 and lat_seq % TOKEN_TILE == 0 and 1 + n_lat <= SUBLANES
    assert TOKEN_TILE % CHUNK == 0 and HALO >= CONV_K // 2 and ca == w_out_ab.shape[1] // 2
    assert rows_n % Q_ROWS == 0 and rows_n >= K_ROWS and t_ctx % (Q_ROWS * GRID_W) == 0
    assert n_heads * head_dim == d and 2 * head_dim == LANES and n_experts <= LANES
    assert (t * TOP_K) % MOE_TILE == 0
    assert d == SUBLANES * LANES, "MoE rows are moved as one (SUBLANES, LANES) f32 tile each"
    alpha = float((2 * depth) ** 0.25)
    scale = float(head_dim ** -0.5)
    assert np.frexp(scale)[0] == 0.5, "the attention scale is folded into the bf16 queries"
    meta = _token_meta(n_ctx_seq, seq, n_lat, lat_seq)

    x = jnp.concatenate([x_prompt.reshape(t_ctx, d), x_sample.reshape(t_lat, d)], axis=0)
    cvec = jnp.zeros((SUBLANES, d), F32).at[0].set(c_ctx).at[1:1 + n_lat].set(c)
    mod = _modulation(cvec, w_mod, b_mod).reshape(depth, SUBLANES, 6, d)

    pad_e = LANES - n_experts
    common = {
        "n_experts": n_experts,
        "ln_g": ln_g.reshape(depth, 2, 1, d), "ln_b": ln_b.reshape(depth, 2, 1, d),
        "w_router": jnp.pad(w_router, ((0, 0), (0, 0), (0, pad_e))).astype(BF16),
        "b_router": jnp.pad(b_router, ((0, 0), (0, pad_e)), constant_values=PAD_LOGIT).reshape(depth, 1, LANES),
    }
    even = dict(common)
    even.update({
        "w_in": w_in_ab.astype(BF16), "sgu_g": sgu_ln_g.reshape(n_even, 1, ca),
        "sgu_b": sgu_ln_b.reshape(n_even, 1, ca), "w_sp": w_spatial.astype(BF16),
        "b_sp": jnp.repeat(jnp.transpose(b_spatial, (0, 2, 1)), ca // G_A, axis=2),
        "conv_w": conv_w, "conv_b": conv_b.reshape(n_even, 1, ca),
        "cln_g": conv_ln_g.reshape(n_even, 1, ca), "cln_b": conv_ln_b.reshape(n_even, 1, ca),
        "w_out_ab": w_out_ab.astype(BF16)})
    odd = dict(common)
    odd["w_out_c"] = w_out_c.astype(BF16)
    w_qkv_bf = w_qkv.astype(BF16)
    lat_tables = _latent_window_tables(rows_n)
    n_tiles = t * TOP_K // MOE_TILE + n_experts

    ck_all = cache_k.reshape(n_lat, n_odd, -1, d)
    cv_all = cache_v.reshape(n_lat, n_odd, -1, d)
    new_kv = None
    qkv = None
    for l in range(depth):
        i = l // 2
        if l % 2 == 0:
            x1, h2, top_e, gates, counts = _even_layer(l, alpha, meta, x, mod, even)
        else:
            q, k, v = qkv
            o_ctx = _attn_ctx(q, k, v, n_ctx_seq, seq, scale)
            bias = _latent_bias(rpb[i], lat_tables[2])
            o_lat = _attn_lat(q, k, v, ck_all, cv_all, i, bias, lat_tables, t_ctx, n_lat, lat_seq, scale)
            x1, h2, top_e, gates, counts = _odd_out_proj(l, alpha, meta, x, o_ctx, o_lat, mod, odd)
        pad_end, pos = _route_tables(top_e, counts, n_experts)
        xs = _moe_dispatch(l, h2, pad_end, pos, n_tiles)
        yb = _moe_experts(l, xs, pad_end, w_gate_up, b_gate_up, w_down, b_down)
        if l % 2 == 0 and l + 1 < depth:
            x, q, k, v, k32, v32 = _moe_combine_qkv(
                l, alpha, meta, x1, yb, pos, gates, mod, common["ln_g"], common["ln_b"], w_qkv_bf,
                (n_ctx_seq, n_odd, seq), new_kv)
            qkv, new_kv = (q, k, v), (k32, v32)
        else:
            x = _moe_combine(l, alpha, meta, x1, yb, pos, gates, mod, common["ln_g"], common["ln_b"],
                             split_at=t_ctx if l == depth - 1 else None)

    y_prompt = x[0].reshape(n_ctx_seq, seq, d)
    y_sample = x[1].reshape(n_lat, lat_seq, d)
    cache_shape = (n_ctx_seq, n_odd, seq, n_heads, head_dim)
    return (y_prompt, y_sample, new_kv[0].reshape(cache_shape), new_kv[1].reshape(cache_shape))
```

```python
import functools

import numpy as np
import jax
import jax.numpy as jnp
from jax import lax
from jax.experimental import pallas as pl
from jax.experimental.pallas import tpu as pltpu

F32 = jnp.float32
BF16 = jnp.bfloat16

GRID_W = 64
G_A = 8
CHUNK = 128
CONV_K = 31
WIN_ROWS = 8
WIN_COLS = 16
TOP_K = 4
SWIGLU_ALPHA = 1.702
SWIGLU_LIMIT = 7.0
LN_EPS = 1e-5
NEG_INF = -1e30

LANES = 128
SUBLANES = 8
VMEM_LIMIT = 56 * 1024 * 1024

TOKEN_TILE = 256
HALO = 16
MOE_TILE = 256
Q_ROWS = 4
K_ROWS = 12
PAD_LOGIT = -3e38


def _ln(x, g, b):
    mu = jnp.mean(x, axis=-1, keepdims=True)
    xc = x - mu
    var = jnp.mean(xc * xc, axis=-1, keepdims=True)
    return xc * lax.rsqrt(var + LN_EPS) * g + b


def _gelu(x):
    return 0.5 * x * (1.0 + jnp.tanh(0.7978845608028654 * (x + 0.044715 * (x * x * x))))


def _sigmoid(x):
    return jax.nn.sigmoid(x)


def _dot(a, b):
    return jnp.dot(a, b, preferred_element_type=F32)


def _dot_nt(a, b):
    return lax.dot_general(a, b, (((1,), (1,)), ((), ())), preferred_element_type=F32)


def _params(n_axes):
    return pltpu.CompilerParams(dimension_semantics=("arbitrary",) * n_axes,
                                vmem_limit_bytes=VMEM_LIMIT)


def _mod_body(c_ref, w_ref, b_ref, o_ref):
    c = c_ref[...]
    s = (c * _sigmoid(c)).astype(BF16)
    o_ref[...] = _dot(s, w_ref[...].astype(BF16)) + b_ref[...]


def _modulation(cvec, w_mod, b_mod):
    depth, d, n = w_mod.shape
    tn = n // 4
    return pl.pallas_call(
        _mod_body,
        grid=(depth, n // tn),
        in_specs=[pl.BlockSpec((SUBLANES, d), lambda l, j: (0, 0)),
                  pl.BlockSpec((None, d, tn), lambda l, j: (l, 0, j)),
                  pl.BlockSpec((None, 1, tn), lambda l, j: (l, 0, j))],
        out_specs=pl.BlockSpec((None, SUBLANES, tn), lambda l, j: (l, 0, j)),
        out_shape=jax.ShapeDtypeStruct((depth, SUBLANES, n), F32),
        compiler_params=_params(2),
        name="adaln_modulation",
    )(cvec, w_mod, b_mod.reshape(depth, 1, n))


def _post_mixer(alpha, x, y, mod_ref, lng_ref, lnb_ref, wr_ref, br_ref,
                x1_ref, h2_ref, te_ref, tg_ref, cnt_ref, cnt_scr):
    i = pl.program_id(0)
    x1 = _ln(alpha * x + mod_ref[2:3, :] * y, lng_ref[...], lnb_ref[...])
    x1_ref[...] = x1
    h2 = x1 * (1.0 + mod_ref[4:5, :]) + mod_ref[3:4, :]
    h2_ref[...] = h2
    logits = _dot(h2.astype(BF16), wr_ref[...]) + br_ref[...]
    n_e = cnt_scr.shape[0]
    lt = logits.T[:n_e]
    tt = lt.shape[1]
    eidx = lax.broadcasted_iota(jnp.int32, lt.shape, 0)
    vals, idxs = [], []
    for _ in range(TOP_K):
        m = jnp.max(lt, axis=0, keepdims=True)
        idx = jnp.min(jnp.where(lt == m, eidx, n_e), axis=0, keepdims=True)
        vals.append(m)
        idxs.append(idx)
        lt = jnp.where(eidx == idx, -jnp.inf, lt)
    exps = [jnp.exp(v - vals[0]) for v in vals]
    den = exps[0]
    for e in exps[1:]:
        den = den + e

    @pl.when(i == 0)
    def _():
        cnt_scr[...] = jnp.zeros(cnt_scr.shape, cnt_scr.dtype)

    onehot = jnp.zeros(lt.shape, F32)
    for k in range(TOP_K):
        onehot = onehot + (eidx == idxs[k]).astype(F32)
    row = lax.broadcasted_iota(jnp.int32, (tt, tt), 0)
    col = lax.broadcasted_iota(jnp.int32, (tt, tt), 1)
    before = _dot(onehot.astype(BF16), (row < col).astype(BF16)) + cnt_scr[:, 0:1]
    cnt = cnt_scr[...] + jnp.sum(onehot, axis=1, keepdims=True)
    cnt_scr[...] = cnt
    cnt_ref[...] = cnt.astype(jnp.int32)

    row_te = lax.broadcasted_iota(jnp.int32, te_ref.shape, 0)
    row_tg = lax.broadcasted_iota(jnp.int32, (LANES, tt), 0)
    te = jnp.zeros(te_ref.shape, jnp.int32)
    tg = jnp.zeros((LANES, tt), F32)
    for k in range(TOP_K):
        rank = jnp.sum(jnp.where(eidx == idxs[k], before, 0.0), axis=0, keepdims=True)
        te = jnp.where(row_te == k, idxs[k], te)
        te = jnp.where(row_te == TOP_K + k, rank.astype(jnp.int32), te)
        tg = jnp.where(row_tg == k, exps[k] / den, tg)
    te_ref[...] = te
    tg_ref[...] = tg.T


def _expert_rows(n_experts):
    return -(-n_experts // SUBLANES) * SUBLANES


def _epilogue_specs(l, d, n_experts):
    in_specs = [pl.BlockSpec((None, None, 1, d), lambda i, s, p, n: (l, 0, 0, 0)),
                pl.BlockSpec((None, None, 1, d), lambda i, s, p, n: (l, 0, 0, 0)),
                pl.BlockSpec((None, d, LANES), lambda i, s, p, n: (l, 0, 0)),
                pl.BlockSpec((None, 1, LANES), lambda i, s, p, n: (l, 0, 0))]
    out_specs = [pl.BlockSpec((TOKEN_TILE, d), lambda i, s, p, n: (i, 0)),
                 pl.BlockSpec((TOKEN_TILE, d), lambda i, s, p, n: (i, 0)),
                 pl.BlockSpec((2 * TOP_K, TOKEN_TILE), lambda i, s, p, n: (0, i)),
                 pl.BlockSpec((TOKEN_TILE, LANES), lambda i, s, p, n: (i, 0)),
                 pl.BlockSpec((_expert_rows(n_experts), LANES), lambda i, s, p, n: (0, 0))]
    return in_specs, out_specs


def _epilogue_out_shapes(t, d, n_experts):
    return [jax.ShapeDtypeStruct((t, d), F32), jax.ShapeDtypeStruct((t, d), F32),
            jax.ShapeDtypeStruct((2 * TOP_K, t), jnp.int32), jax.ShapeDtypeStruct((t, LANES), F32),
            jax.ShapeDtypeStruct((_expert_rows(n_experts), LANES), jnp.int32)]


def _epilogue_scratch(n_experts):
    return [pltpu.VMEM((_expert_rows(n_experts), LANES), F32)]


def _even_body(alpha, n_first, seg_ref, prev_ref, next_ref, *refs):
    n_x = 6 if n_first else 3
    x_refs = refs[:n_x]
    (mod_ref, win_ref, sg_ref, sb_ref, wsp_ref, bsp_ref, cw_ref, cb_ref, cg_ref, cbb_ref, wout_ref,
     lng_ref, lnb_ref, wr_ref, br_ref, x1_ref, h2_ref, te_ref, tg_ref, cnt_ref, gl_scr, cnt_scr) = refs[n_x:]
    i = pl.program_id(0)

    def x_in(j):
        if n_first:
            return jnp.where(i < n_first, x_refs[j][...], x_refs[3 + j][...])
        return x_refs[j][...]

    x = x_in(0)
    tt = x.shape[0]
    ca = sg_ref.shape[-1]
    cb2 = 2 * ca
    sc = 1.0 + mod_ref[1:2, :]
    sh = mod_ref[0:1, :]
    z = _dot((x * sc + sh).astype(BF16), win_ref[...])

    u = _gelu(z[:, :ca])
    v = _ln(_gelu(z[:, ca:cb2]), sg_ref[...], sb_ref[...]).astype(BF16)
    half = lax.broadcasted_iota(jnp.int32, (CHUNK, LANES), 1) < (LANES // 2)
    chunks = []
    for ck in range(tt // CHUNK):
        cols = []
        for j in range(ca // LANES):
            vblk = v[ck * CHUNK:(ck + 1) * CHUNK, j * LANES:(j + 1) * LANES]
            cols.append(jnp.where(half, _dot(wsp_ref[2 * j], vblk), _dot(wsp_ref[2 * j + 1], vblk)))
        chunks.append(jnp.concatenate(cols, axis=1) + bsp_ref[...])
    y_a = u * jnp.concatenate(chunks, axis=0)

    def glu_rows(xh):
        zh = _dot((xh * sc + sh).astype(BF16), win_ref[:, cb2:])
        return zh[:, :ca] * _sigmoid(zh[:, ca:])

    gl_scr[0:HALO, :] = jnp.where(prev_ref[i] > 0, glu_rows(x_in(1)), 0.0)
    gl_scr[HALO:HALO + tt, :] = z[:, cb2:cb2 + ca] * _sigmoid(z[:, cb2 + ca:])
    gl_scr[HALO + tt:, :] = jnp.where(next_ref[i] > 0, glu_rows(x_in(2)), 0.0)
    off = HALO - CONV_K // 2
    g_ext = gl_scr[...]
    n_ext = g_ext.shape[0]
    dc = None
    for res in range(SUBLANES):
        taps = [k for k in range(CONV_K) if (off + k) % SUBLANES == res]
        if not taps:
            continue
        shifted = g_ext if res == 0 else pltpu.roll(g_ext, n_ext - res, axis=0)
        for k in taps:
            q = (off + k) // SUBLANES * SUBLANES
            term = shifted[q:q + tt, :] * cw_ref[k:k + 1, :]
            dc = term if dc is None else dc + term
    yb = _ln(dc + cb_ref[...], cg_ref[...], cbb_ref[...])
    y_b = yb * _sigmoid(yb)

    y = _dot(jnp.concatenate([y_a, y_b], axis=1).astype(BF16), wout_ref[...])
    _post_mixer(alpha, x, y, mod_ref, lng_ref, lnb_ref, wr_ref, br_ref,
                x1_ref, h2_ref, te_ref, tg_ref, cnt_ref, cnt_scr)


def _even_layer(l, alpha, meta, x, mod, p):
    xs = x if isinstance(x, tuple) else (x,)
    t, d = sum(a.shape[0] for a in xs), xs[0].shape[1]
    li = l // 2
    nh = TOKEN_TILE // HALO
    ca = p["sgu_g"].shape[-1]
    ep_in, ep_out = _epilogue_specs(l, d, p["n_experts"])
    const3 = lambda i, s, pv, nx: (li, 0, 0)

    def x_specs(first_tile, n_rows):
        n_t, n_h = n_rows // TOKEN_TILE, n_rows // HALO
        loc = lambda i: jnp.clip(i - first_tile, 0, n_t - 1)
        return [pl.BlockSpec((TOKEN_TILE, d), lambda i, s, pv, nx: (loc(i), 0)),
                pl.BlockSpec((HALO, d), lambda i, s, pv, nx: (jnp.maximum(loc(i) * nh - 1, 0), 0)),
                pl.BlockSpec((HALO, d), lambda i, s, pv, nx: (jnp.minimum((loc(i) + 1) * nh, n_h - 1), 0))]

    n_first = xs[0].shape[0] // TOKEN_TILE if len(xs) == 2 else 0
    x_in_specs, x_args, first = [], [], 0
    for a in xs:
        x_in_specs += x_specs(first, a.shape[0])
        x_args += [a, a, a]
        first += a.shape[0] // TOKEN_TILE
    in_specs = x_in_specs + [
        pl.BlockSpec((None, None, 6, d), lambda i, s, pv, nx: (l, s[i], 0, 0)),
        pl.BlockSpec((None, d, 4 * ca), const3),
        pl.BlockSpec((None, 1, ca), const3),
        pl.BlockSpec((None, 1, ca), const3),
        pl.BlockSpec((None, G_A, CHUNK, CHUNK), lambda i, s, pv, nx: (li, 0, 0, 0)),
        pl.BlockSpec((None, CHUNK, ca), const3),
        pl.BlockSpec((None, CONV_K, ca), const3),
        pl.BlockSpec((None, 1, ca), const3),
        pl.BlockSpec((None, 1, ca), const3),
        pl.BlockSpec((None, 1, ca), const3),
        pl.BlockSpec((None, 2 * ca, d), const3),
    ] + ep_in
    return pl.pallas_call(
        functools.partial(_even_body, alpha, n_first),
        grid_spec=pltpu.PrefetchScalarGridSpec(
            num_scalar_prefetch=3, grid=(t // TOKEN_TILE,),
            in_specs=in_specs, out_specs=ep_out,
            scratch_shapes=[pltpu.VMEM((TOKEN_TILE + 2 * HALO, ca), F32)]
            + _epilogue_scratch(p["n_experts"])),
        out_shape=_epilogue_out_shapes(t, d, p["n_experts"]),
        compiler_params=_params(1),
        name=f"even_mixer_{l}",
    )(meta["seg"], meta["prev"], meta["next"], *x_args, mod,
      p["w_in"], p["sgu_g"], p["sgu_b"], p["w_sp"], p["b_sp"], p["conv_w"], p["conv_b"],
      p["cln_g"], p["cln_b"], p["w_out_ab"], p["ln_g"], p["ln_b"], p["w_router"], p["b_router"])


def _qkv_from(x, mod_ref, w_ref, n_ctx_tiles, q_ref, k_ref, v_ref, k32_ref, v32_ref):
    d = x.shape[1]
    h = (x * (1.0 + mod_ref[1:2, :]) + mod_ref[0:1, :]).astype(BF16)
    qkv = _dot(h, w_ref[...])
    q_ref[...] = qkv[:, :d].astype(BF16)
    k = qkv[:, d:2 * d]
    v = qkv[:, 2 * d:]
    k_ref[...] = k.astype(BF16)
    v_ref[...] = v.astype(BF16)

    @pl.when(pl.program_id(0) < n_ctx_tiles)
    def _():
        for ref, val in ((k32_ref, k), (v32_ref, v)):
            if len(ref.shape) == 3:
                ref[0] = val
                if ref.shape[0] > 1:
                    ref[1:] = jnp.zeros((ref.shape[0] - 1,) + ref.shape[1:], ref.dtype)
            else:
                ref[...] = val


def _head_pair_attention(q2, k_parts, v_parts, bias_parts, scale):
    m_rows = q2.shape[0]
    lane = lax.broadcasted_iota(jnp.int32, q2.shape, 1)
    qs = q2 * scale
    zero = jnp.zeros_like(q2)
    q_both = jnp.concatenate([jnp.where(lane < LANES // 2, qs, zero),
                              jnp.where(lane >= LANES // 2, qs, zero)], axis=0)
    ss = []
    for j, kp in enumerate(k_parts):
        s = _dot_nt(q_both, kp)
        if bias_parts[0][j] is not None:
            s = s + jnp.concatenate([bias_parts[0][j], bias_parts[1][j]], axis=0)
        ss.append(s)
    m = ss[0].max(axis=-1, keepdims=True)
    for s in ss[1:]:
        m = jnp.maximum(m, s.max(axis=-1, keepdims=True))
    den = None
    o = None
    for s, vp in zip(ss, v_parts):
        e = jnp.exp(s - m)
        es = e.sum(axis=-1, keepdims=True)
        den = es if den is None else den + es
        pv = _dot(e.astype(BF16), vp)
        o = pv if o is None else o + pv
    o = o / den
    return jnp.where(lane < LANES // 2, o[:m_rows], o[m_rows:])


def _attn_ctx_body(scale, q_ref, k_ref, v_ref, o_ref):
    d = q_ref.shape[1]
    for pr in range(d // LANES):
        sl = slice(pr * LANES, (pr + 1) * LANES)
        o = _head_pair_attention(q_ref[:, sl], [k_ref[:, sl]], [v_ref[:, sl]],
                                 [[None], [None]], scale)
        o_ref[:, sl] = o.astype(o_ref.dtype)


def _attn_ctx(q, k, v, n_seq, seq, scale):
    d = q.shape[1]
    blk = pl.BlockSpec((seq, d), lambda b: (b, 0))
    return pl.pallas_call(
        functools.partial(_attn_ctx_body, scale),
        grid=(n_seq,),
        in_specs=[blk, blk, blk],
        out_specs=blk,
        out_shape=jax.ShapeDtypeStruct((n_seq * seq, d), BF16),
        compiler_params=_params(1),
        name="attn_ctx",
    )(q, k, v)


def _attn_lat_body(scale, cls_ref, kb_ref, q_ref, k0_ref, k1_ref, k2_ref, v0_ref, v1_ref, v2_ref,
                   ck_ref, cv_ref, bias_ref, o_ref):
    d = q_ref.shape[1]
    for pr in range(d // LANES):
        sl = slice(pr * LANES, (pr + 1) * LANES)
        k_loc = jnp.concatenate([k0_ref[:, sl], k1_ref[:, sl], k2_ref[:, sl]], axis=0)
        v_loc = jnp.concatenate([v0_ref[:, sl], v1_ref[:, sl], v2_ref[:, sl]], axis=0)
        ck = ck_ref[:, sl].astype(BF16)
        cv = cv_ref[:, sl].astype(BF16)
        o = _head_pair_attention(q_ref[:, sl], [k_loc, ck], [v_loc, cv],
                                 [[bias_ref[2 * pr], None], [bias_ref[2 * pr + 1], None]], scale)
        o_ref[:, sl] = o.astype(o_ref.dtype)


def _latent_window_tables(rows_n):
    wr = min(WIN_ROWS, rows_n)
    n_rt = rows_n // Q_ROWS
    kstart = np.clip(np.arange(n_rt) * Q_ROWS - wr // 2, 0, rows_n - K_ROWS)
    kstart = (kstart // Q_ROWS) * Q_ROWS
    patterns, cls = [], []
    for rt in range(n_rt):
        pat = np.full((Q_ROWS, K_ROWS), -1, np.int64)
        for qi in range(Q_ROWS):
            r = rt * Q_ROWS + qi
            rs = int(np.clip(r - wr // 2, 0, rows_n - wr))
            for kj in range(K_ROWS):
                kr = int(kstart[rt]) + kj
                if rs <= kr < rs + wr:
                    pat[qi, kj] = kr - r + WIN_ROWS - 1
        assert (pat >= 0).sum(axis=1).min() == wr, "key block does not cover the window"
        key = pat.tobytes()
        if key not in [p.tobytes() for p in patterns]:
            patterns.append(pat)
        cls.append([p.tobytes() for p in patterns].index(key))
    return (kstart // Q_ROWS).astype(np.int32), np.asarray(cls, np.int32), np.stack(patterns)


def _latent_bias(rpb, patterns):
    h = rpb.shape[0]
    qc = np.arange(GRID_W)[:, None]
    kc = np.arange(GRID_W)[None, :]
    qcs = np.clip(qc - WIN_COLS // 2, 0, GRID_W - WIN_COLS)
    col_ok = (kc >= qcs) & (kc < qcs + WIN_COLS)
    dc = np.clip(kc - qc + WIN_COLS - 1, 0, 2 * WIN_COLS - 2)
    onehot = (dc[None] == np.arange(2 * WIN_COLS - 1)[:, None, None]) & col_ok[None]
    cm = jnp.einsum("hrd,dqk->hrqk", rpb, jnp.asarray(onehot, F32), precision=lax.Precision.HIGHEST)
    cm = jnp.where(jnp.asarray(col_ok), cm, NEG_INF)
    cx = jnp.concatenate([cm, jnp.full((h, 1, GRID_W, GRID_W), NEG_INF, F32)], axis=1)
    cx2 = jnp.concatenate([cx, cx], axis=-1)
    idx = np.where(patterns >= 0, patterns, 2 * WIN_ROWS - 1)
    n_cls = idx.shape[0]
    assert 2 * GRID_W == LANES and K_ROWS % 2 == 0

    def body(idx_ref, cx_ref, o_ref):
        c = pl.program_id(0)
        low = lax.broadcasted_iota(jnp.int32, (GRID_W, LANES), 1) < GRID_W
        for qi in range(Q_ROWS):
            for m in range(K_ROWS // 2):
                base = (c * Q_ROWS + qi) * K_ROWS + 2 * m
                blk = jnp.where(low, cx_ref[idx_ref[base]], cx_ref[idx_ref[base + 1]])
                o_ref[qi * GRID_W:(qi + 1) * GRID_W, m * LANES:(m + 1) * LANES] = blk

    return pl.pallas_call(
        body,
        grid_spec=pltpu.PrefetchScalarGridSpec(
            num_scalar_prefetch=1, grid=(n_cls, h),
            in_specs=[pl.BlockSpec((None, cx2.shape[1], GRID_W, LANES), lambda c, j, ix: (j, 0, 0, 0))],
            out_specs=pl.BlockSpec((None, None, Q_ROWS * GRID_W, K_ROWS * GRID_W),
                                   lambda c, j, ix: (c, j, 0, 0))),
        out_shape=jax.ShapeDtypeStruct((n_cls, h, Q_ROWS * GRID_W, K_ROWS * GRID_W), F32),
        compiler_params=_params(2),
        name="latent_bias_table",
    )(jnp.asarray(idx.reshape(-1), jnp.int32), cx2)


def _attn_lat(q, k, v, ck, cv, li, bias, tables, tok0, n_batch, n_tok, scale):
    d = q.shape[1]
    kblk, cls, _ = tables
    n_rt = kblk.shape[0]
    qt = Q_ROWS * GRID_W
    base = tok0 // qt
    per_b = n_tok // qt
    h = bias.shape[1]
    lc = ck.shape[2]

    def kv_spec(j):
        return pl.BlockSpec((qt, d), lambda b, r, c, kb: (base + b * per_b + kb[r] + j, 0))

    return pl.pallas_call(
        functools.partial(_attn_lat_body, scale),
        grid_spec=pltpu.PrefetchScalarGridSpec(
            num_scalar_prefetch=2, grid=(n_batch, n_rt),
            in_specs=[pl.BlockSpec((qt, d), lambda b, r, c, kb: (base + b * per_b + r, 0)),
                      kv_spec(0), kv_spec(1), kv_spec(2), kv_spec(0), kv_spec(1), kv_spec(2),
                      pl.BlockSpec((None, None, lc, d), lambda b, r, c, kb: (b, li, 0, 0)),
                      pl.BlockSpec((None, None, lc, d), lambda b, r, c, kb: (b, li, 0, 0)),
                      pl.BlockSpec((None, h, qt, K_ROWS * GRID_W), lambda b, r, c, kb: (c[r], 0, 0, 0))],
            out_specs=pl.BlockSpec((qt, d), lambda b, r, c, kb: (b * per_b + r, 0))),
        out_shape=jax.ShapeDtypeStruct((n_batch * n_tok, d), BF16),
        compiler_params=_params(2),
        name="attn_latent",
    )(jnp.asarray(cls), jnp.asarray(kblk), q, k, k, k, v, v, v, ck, cv, bias)


def _proj_body(alpha, n_ctx_tiles, seg_ref, prev_ref, next_ref, x_ref, oc_ref, ol_ref, mod_ref, w_ref,
               lng_ref, lnb_ref, wr_ref, br_ref, x1_ref, h2_ref, te_ref, tg_ref, cnt_ref, cnt_scr):
    o = jnp.where(pl.program_id(0) < n_ctx_tiles, oc_ref[...], ol_ref[...])
    y = _dot(o, w_ref[...])
    _post_mixer(alpha, x_ref[...], y, mod_ref, lng_ref, lnb_ref, wr_ref, br_ref,
                x1_ref, h2_ref, te_ref, tg_ref, cnt_ref, cnt_scr)


def _odd_out_proj(l, alpha, meta, x, o_ctx, o_lat, mod, p):
    t, d = x.shape
    li = l // 2
    nc = o_ctx.shape[0] // TOKEN_TILE
    ep_in, ep_out = _epilogue_specs(l, d, p["n_experts"])
    tile = pl.BlockSpec((TOKEN_TILE, d), lambda i, s, pv, nx: (i, 0))
    return pl.pallas_call(
        functools.partial(_proj_body, alpha, nc),
        grid_spec=pltpu.PrefetchScalarGridSpec(
            num_scalar_prefetch=3, grid=(t // TOKEN_TILE,),
            in_specs=[tile,
                      pl.BlockSpec((TOKEN_TILE, d), lambda i, s, pv, nx: (jnp.minimum(i, nc - 1), 0)),
                      pl.BlockSpec((TOKEN_TILE, d), lambda i, s, pv, nx: (jnp.maximum(i - nc, 0), 0)),
                      pl.BlockSpec((None, None, 6, d), lambda i, s, pv, nx: (l, s[i], 0, 0)),
                      pl.BlockSpec((None, d, d), lambda i, s, pv, nx: (li, 0, 0))] + ep_in,
            out_specs=ep_out, scratch_shapes=_epilogue_scratch(p["n_experts"])),
        out_shape=_epilogue_out_shapes(t, d, p["n_experts"]),
        compiler_params=_params(1),
        name=f"attn_out_proj_{l}",
    )(meta["seg"], meta["prev"], meta["next"], x, o_ctx, o_lat, mod, p["w_out_c"],
      p["ln_g"], p["ln_b"], p["w_router"], p["b_router"])


def _route_tables(te, counts, n_experts):
    t = te.shape[1]
    counts = counts[:n_experts, 0]
    padded = (counts + MOE_TILE - 1) // MOE_TILE * MOE_TILE
    pad_end = jnp.cumsum(padded).astype(jnp.int32)
    pad_start = pad_end - padded
    experts, ranks = te[:TOP_K], te[TOP_K:]
    eid = jnp.arange(n_experts, dtype=jnp.int32)
    sel = experts[None] == eid[:, None, None]
    pos = ranks + jnp.sum(jnp.where(sel, pad_start[:, None, None], 0), axis=0)
    pos = jnp.transpose(pos.reshape(TOP_K, t // TOKEN_TILE, TOKEN_TILE), (1, 2, 0))
    pos = pos.astype(jnp.int32).reshape(t // TOKEN_TILE, 1, TOKEN_TILE * TOP_K)
    return pad_end, pos


def _dispatch_body(pend_ref, pos_ref, h2_ref, xs_hbm, stage, sem, zsem):
    i = pl.program_id(0)
    n_steps = pl.num_programs(0)
    slot = i % 2
    tt = h2_ref.shape[0]
    n_e = pend_ref.shape[0]

    def scatter_wait(s):
        for _ in range(TOP_K):
            pltpu.make_async_copy(stage.at[s], xs_hbm.at[pl.ds(0, tt)], sem.at[s]).wait()

    @pl.when(i == 0)
    def _():
        stage[0] = jnp.zeros(stage.shape[1:], stage.dtype)
        n_tiles = xs_hbm.shape[0] // tt
        n_used = pend_ref[n_e - 1] // tt
        for phase in range(2):
            for e in range(n_e):
                lo = pend_ref[e - 1] if e else 0
                for cond, row0 in ((pend_ref[e] > lo, pend_ref[e] - tt),
                                   (n_used + e < n_tiles, (n_used + e) * tt)):
                    @pl.when(cond)
                    def _():
                        cp = pltpu.make_async_copy(
                            stage.at[0], xs_hbm.at[pl.ds(pl.multiple_of(row0, tt), tt)], zsem)
                        if phase == 0:
                            cp.start()
                        else:
                            cp.wait()

    @pl.when(i >= 2)
    def _():
        scatter_wait(slot)

    stage[slot] = h2_ref[...].reshape(stage.shape[1:])

    def body(r, c):
        for k in range(TOP_K):
            pltpu.make_async_copy(stage.at[slot, r], xs_hbm.at[pos_ref[0, r * TOP_K + k]],
                                  sem.at[slot]).start(priority=k % 2)
        return c
    lax.fori_loop(0, tt, body, 0, unroll=8)

    @pl.when(i == n_steps - 1)
    def _():
        @pl.when(n_steps >= 2)
        def _():
            scatter_wait(1 - slot)
        scatter_wait(slot)


def _moe_dispatch(l, h2, pad_end, pos, n_tiles):
    t, d = h2.shape
    assert TOKEN_TILE == MOE_TILE
    return pl.pallas_call(
        _dispatch_body,
        grid_spec=pltpu.PrefetchScalarGridSpec(
            num_scalar_prefetch=1, grid=(t // TOKEN_TILE,),
            in_specs=[pl.BlockSpec((None, 1, TOKEN_TILE * TOP_K), lambda i, pe: (i, 0, 0),
                                   memory_space=pltpu.SMEM),
                      pl.BlockSpec((TOKEN_TILE, d), lambda i, pe: (i, 0))],
            out_specs=pl.BlockSpec(memory_space=pl.ANY),
            scratch_shapes=[pltpu.VMEM((2, TOKEN_TILE, SUBLANES, LANES), F32),
                            pltpu.SemaphoreType.DMA((2,)),
                            pltpu.SemaphoreType.DMA]),
        out_shape=jax.ShapeDtypeStruct((n_tiles * MOE_TILE, SUBLANES, LANES), F32),
        compiler_params=_params(1),
        name=f"moe_dispatch_{l}",
    )(pad_end, pos, h2)


def _moe_body(l, pend_ref, bgu_ref, bd_ref, xs_hbm, wgu_hbm, wd_hbm, yb_hbm,
              xbuf, ybuf, wgu_st, wd_st, wgu_bf, wd_bf, xsem, ysem, wsem):
    e = pl.program_id(0)
    n_e = pl.num_programs(0)
    tm = xbuf.shape[1]
    de = wd_bf.shape[0]
    par = e % 2
    g_lo = jnp.where(e == 0, 0, pend_ref[jnp.maximum(e - 1, 0)]) // tm
    g_hi = pend_ref[e] // tm
    n_used = pend_ref[n_e - 1] // tm
    n_tiles = yb_hbm.shape[0] // tm

    def weight_copies(ex, p):
        return (pltpu.make_async_copy(wgu_hbm.at[l, ex], wgu_st.at[p], wsem.at[p, 0]),
                pltpu.make_async_copy(wd_hbm.at[l, ex], wd_st.at[p], wsem.at[p, 1]))

    def x_copy(g, s):
        return pltpu.make_async_copy(xs_hbm.at[pl.ds(pl.multiple_of(g * tm, tm), tm)], xbuf.at[s],
                                     xsem.at[s])

    def y_copy(g, s):
        return pltpu.make_async_copy(ybuf.at[s], yb_hbm.at[pl.ds(pl.multiple_of(g * tm, tm), tm)],
                                     ysem.at[s])

    @pl.when(e == 0)
    def _():
        for cp in weight_copies(0, 0):
            cp.start(priority=1)

        @pl.when(n_used > 0)
        def _():
            x_copy(0, 0).start()

    @pl.when(e + 1 < n_e)
    def _():
        for cp in weight_copies(e + 1, 1 - par):
            cp.start(priority=1)

    for cp in weight_copies(e, par):
        cp.wait()

    @pl.when(g_hi > g_lo)
    def _():
        wgu_bf[...] = wgu_st[par].astype(BF16)
        wd_bf[...] = wd_st[par].astype(BF16)

    def tile_body(g, carry):
        s = g % 2
        x_copy(g, s).wait()

        @pl.when(g + 1 < n_used)
        def _():
            x_copy(g + 1, 1 - s).start()

        x = xbuf[s].reshape(tm, wgu_bf.shape[0])
        hgu = _dot(x.astype(BF16), wgu_bf[...]) + bgu_ref[...]
        x_glu = jnp.minimum(hgu[:, :de], SWIGLU_LIMIT)
        x_lin = jnp.clip(hgu[:, de:], -SWIGLU_LIMIT, SWIGLU_LIMIT)
        act = x_glu * _sigmoid(SWIGLU_ALPHA * x_glu) * (x_lin + 1.0)
        y = _dot(act.astype(BF16), wd_bf[...]) + bd_ref[...]

        @pl.when(g >= 2)
        def _():
            y_copy(g - 2, s).wait()

        ybuf[s] = y.reshape(ybuf.shape[1:])
        y_copy(g, s).start()
        return carry

    lax.fori_loop(g_lo, g_hi, tile_body, 0)

    @pl.when(e == n_e - 1)
    def _():
        @pl.when(n_used >= 2)
        def _():
            y_copy(n_used - 2, n_used % 2).wait()

        @pl.when(n_used >= 1)
        def _():
            y_copy(n_used - 1, (n_used + 1) % 2).wait()

        ybuf[0] = jnp.zeros(ybuf.shape[1:], ybuf.dtype)
        for phase in range(2):
            for m in range(n_e):
                @pl.when(n_used + m < n_tiles)
                def _():
                    cp = y_copy(n_used + m, 0)
                    if phase == 0:
                        cp.start()
                    else:
                        cp.wait()


def _moe_experts(l, xs, pad_end, w_gu, b_gu, w_d, b_d):
    n_e, de, d = w_d.shape[1], w_d.shape[2], w_d.shape[3]
    row_tile = (MOE_TILE,) + xs.shape[1:]
    return pl.pallas_call(
        functools.partial(_moe_body, l),
        grid_spec=pltpu.PrefetchScalarGridSpec(
            num_scalar_prefetch=1, grid=(n_e,),
            in_specs=[pl.BlockSpec((None, None, 1, 2 * de), lambda e, pe: (l, e, 0, 0)),
                      pl.BlockSpec((None, None, 1, d), lambda e, pe: (l, e, 0, 0)),
                      pl.BlockSpec(memory_space=pl.ANY),
                      pl.BlockSpec(memory_space=pl.ANY),
                      pl.BlockSpec(memory_space=pl.ANY)],
            out_specs=pl.BlockSpec(memory_space=pl.ANY),
            scratch_shapes=[pltpu.VMEM((2,) + row_tile, F32), pltpu.VMEM((2,) + row_tile, F32),
                            pltpu.VMEM((2, d, 2 * de), F32), pltpu.VMEM((2, de, d), F32),
                            pltpu.VMEM((d, 2 * de), BF16), pltpu.VMEM((de, d), BF16),
                            pltpu.SemaphoreType.DMA((2,)), pltpu.SemaphoreType.DMA((2,)),
                            pltpu.SemaphoreType.DMA((2, 2))]),
        out_shape=jax.ShapeDtypeStruct(xs.shape, F32),
        compiler_params=_params(1),
        name=f"moe_experts_{l}",
    )(pad_end, b_gu.reshape(b_gu.shape[0], n_e, 1, 2 * de), b_d.reshape(b_d.shape[0], n_e, 1, d),
      xs, w_gu, w_d)


def _combine_rows(alpha, pos_ref, posn_ref, x1_ref, g_ref, mod_ref, lng_ref, lnb_ref, yb_hbm, buf, sem):
    i = pl.program_id(0)
    n_steps = pl.num_programs(0)
    slot = i % 2
    tt = x1_ref.shape[0]

    def gather_start(idx_ref, s):
        def body(r, c):
            for k in range(TOP_K):
                pltpu.make_async_copy(yb_hbm.at[idx_ref[0, r * TOP_K + k]], buf.at[s, k, r],
                                      sem.at[s]).start(priority=k % 2)
            return c
        lax.fori_loop(0, tt, body, 0, unroll=8)

    @pl.when(i == 0)
    def _():
        gather_start(pos_ref, 0)

    @pl.when(i + 1 < n_steps)
    def _():
        gather_start(posn_ref, 1 - slot)

    for k in range(TOP_K):
        pltpu.make_async_copy(yb_hbm.at[pl.ds(0, tt)], buf.at[slot, k], sem.at[slot]).wait()
    g = g_ref[...]
    y = None
    for k in range(TOP_K):
        part = g[:, k:k + 1] * buf[slot, k].reshape(x1_ref.shape)
        y = part if y is None else y + part
    return _ln(alpha * x1_ref[...] + mod_ref[5:6, :] * y, lng_ref[...], lnb_ref[...])


def _combine_body(alpha, n_ctx_tiles, seg_ref, pos_ref, posn_ref, x1_ref, g_ref, mod_ref, lng_ref,
                  lnb_ref, yb_hbm, o_ref, *rest):
    o2_ref = rest[0] if len(rest) == 3 else None
    buf, sem = rest[-2:]
    i = pl.program_id(0)
    out = _combine_rows(alpha, pos_ref, posn_ref, x1_ref, g_ref, mod_ref, lng_ref, lnb_ref, yb_hbm,
                        buf, sem)
    if o2_ref is None:
        o_ref[...] = out
    else:
        @pl.when(i < n_ctx_tiles)
        def _():
            o_ref[...] = out

        @pl.when(i >= n_ctx_tiles)
        def _():
            o2_ref[...] = out


def _combine_qkv_body(alpha, n_ctx_tiles, n_aliased, seg_ref, pos_ref, posn_ref, x1_ref, g_ref, mod_ref,
                      lng_ref, lnb_ref, modn_ref, w_ref, yb_hbm, *rest):
    o_ref, q_ref, k_ref, v_ref, k32_ref, v32_ref, buf, sem = rest[n_aliased:]
    out = _combine_rows(alpha, pos_ref, posn_ref, x1_ref, g_ref, mod_ref, lng_ref, lnb_ref, yb_hbm,
                        buf, sem)
    o_ref[...] = out
    _qkv_from(out, modn_ref, w_ref, n_ctx_tiles, q_ref, k_ref, v_ref, k32_ref, v32_ref)


def _moe_combine(l, alpha, meta, x1, yb, pos, gates, mod, ln_g, ln_b, split_at=None):
    t, d = x1.shape
    nt = t // TOKEN_TILE
    tile = pl.BlockSpec((TOKEN_TILE, d), lambda i, s: (i, 0))
    pos_blk = lambda f: pl.BlockSpec((None, 1, TOKEN_TILE * TOP_K), f, memory_space=pltpu.SMEM)
    if split_at is None:
        nc, out_specs, out_shape = 0, tile, jax.ShapeDtypeStruct((t, d), F32)
    else:
        nc = split_at // TOKEN_TILE
        out_specs = [pl.BlockSpec((TOKEN_TILE, d), lambda i, s: (jnp.minimum(i, nc - 1), 0)),
                     pl.BlockSpec((TOKEN_TILE, d), lambda i, s: (jnp.maximum(i - nc, 0), 0))]
        out_shape = [jax.ShapeDtypeStruct((split_at, d), F32),
                     jax.ShapeDtypeStruct((t - split_at, d), F32)]
    return pl.pallas_call(
        functools.partial(_combine_body, alpha, nc),
        grid_spec=pltpu.PrefetchScalarGridSpec(
            num_scalar_prefetch=1, grid=(nt,),
            in_specs=[pos_blk(lambda i, s: (i, 0, 0)),
                      pos_blk(lambda i, s: (jnp.minimum(i + 1, nt - 1), 0, 0)),
                      tile,
                      pl.BlockSpec((TOKEN_TILE, LANES), lambda i, s: (i, 0)),
                      pl.BlockSpec((None, None, 6, d), lambda i, s: (l, s[i], 0, 0)),
                      pl.BlockSpec((None, None, 1, d), lambda i, s: (l, 1, 0, 0)),
                      pl.BlockSpec((None, None, 1, d), lambda i, s: (l, 1, 0, 0)),
                      pl.BlockSpec(memory_space=pl.ANY)],
            out_specs=out_specs,
            scratch_shapes=[pltpu.VMEM((2, TOP_K, TOKEN_TILE) + yb.shape[1:], F32),
                            pltpu.SemaphoreType.DMA((2,))]),
        out_shape=out_shape,
        compiler_params=_params(1),
        name=f"moe_combine_{l}",
    )(meta["seg"], pos, pos, x1, gates, mod, ln_g, ln_b, yb)


def _moe_combine_qkv(l, alpha, meta, x1, yb, pos, gates, mod, ln_g, ln_b, w_qkv, cache_dims, prev_kv):
    t, d = x1.shape
    n_seq, n_odd, seq = cache_dims
    nt = t // TOKEN_TILE
    per_seq = seq // TOKEN_TILE
    nc = n_seq * per_seq
    li = (l + 1) // 2
    tile = pl.BlockSpec((TOKEN_TILE, d), lambda i, s: (i, 0))
    seq_of = lambda i: jnp.minimum(i, nc - 1) // per_seq
    blk_of = lambda i: jnp.minimum(i, nc - 1) % per_seq
    if prev_kv is None:
        ctx_tile = pl.BlockSpec((None, n_odd, TOKEN_TILE, d), lambda i, s: (seq_of(i), 0, blk_of(i), 0))
        extra_in, extra_args, aliases = [], (), {}
    else:
        ctx_tile = pl.BlockSpec((None, None, TOKEN_TILE, d), lambda i, s: (seq_of(i), li, blk_of(i), 0))
        extra_in, extra_args = [pl.BlockSpec(memory_space=pl.ANY)] * 2, tuple(prev_kv)
        aliases = {11: 4, 12: 5}
    pos_blk = lambda f: pl.BlockSpec((None, 1, TOKEN_TILE * TOP_K), f, memory_space=pltpu.SMEM)
    return pl.pallas_call(
        functools.partial(_combine_qkv_body, alpha, nc, len(extra_in)),
        input_output_aliases=aliases,
        grid_spec=pltpu.PrefetchScalarGridSpec(
            num_scalar_prefetch=1, grid=(nt,),
            in_specs=[pos_blk(lambda i, s: (i, 0, 0)),
                      pos_blk(lambda i, s: (jnp.minimum(i + 1, nt - 1), 0, 0)),
                      tile,
                      pl.BlockSpec((TOKEN_TILE, LANES), lambda i, s: (i, 0)),
                      pl.BlockSpec((None, None, 6, d), lambda i, s: (l, s[i], 0, 0)),
                      pl.BlockSpec((None, None, 1, d), lambda i, s: (l, 1, 0, 0)),
                      pl.BlockSpec((None, None, 1, d), lambda i, s: (l, 1, 0, 0)),
                      pl.BlockSpec((None, None, 6, d), lambda i, s: (l + 1, s[i], 0, 0)),
                      pl.BlockSpec((None, d, 3 * d), lambda i, s: (li, 0, 0)),
                      pl.BlockSpec(memory_space=pl.ANY)] + extra_in,
            out_specs=[tile] * 4 + [ctx_tile] * 2,
            scratch_shapes=[pltpu.VMEM((2, TOP_K, TOKEN_TILE) + yb.shape[1:], F32),
                            pltpu.SemaphoreType.DMA((2,))]),
        out_shape=[jax.ShapeDtypeStruct((t, d), F32)] + [jax.ShapeDtypeStruct((t, d), BF16)] * 3
        + [jax.ShapeDtypeStruct((n_seq, n_odd, seq, d), F32)] * 2,
        compiler_params=_params(1),
        name=f"moe_combine_qkv_{l}",
    )(meta["seg"], pos, pos, x1, gates, mod, ln_g, ln_b, mod, w_qkv, yb, *extra_args)


def _token_meta(n_ctx_seq, seq, n_lat, lat_seq):
    seg, prev, nxt = [], [], []
    for n_seq, length, seg_of in ((n_ctx_seq, seq, lambda b: 0), (n_lat, lat_seq, lambda b: 1 + b)):
        per = length // TOKEN_TILE
        for b in range(n_seq):
            for j in range(per):
                seg.append(seg_of(b))
                prev.append(int(j > 0))
                nxt.append(int(j < per - 1))
    as_i32 = lambda a: jnp.asarray(np.asarray(a, np.int32))
    return {"seg": as_i32(seg), "prev": as_i32(prev), "next": as_i32(nxt)}


def kernel(x_prompt, x_sample, c, cache_k, cache_v, c_ctx, w_mod, b_mod, ln_g, ln_b, w_in_ab, sgu_ln_g, sgu_ln_b, w_spatial, b_spatial, conv_w, conv_b, conv_ln_g, conv_ln_b, w_out_ab, w_qkv, rpb, w_out_c, w_router, b_router, w_gate_up, b_gate_up, w_down, b_down):
    n_ctx_seq, seq, d = x_prompt.shape
    n_lat, lat_seq, _ = x_sample.shape
    depth = w_mod.shape[0]
    n_heads, head_dim = cache_k.shape[3], cache_k.shape[4]
    n_experts = w_router.shape[-1]
    ca = sgu_ln_g.shape[-1]
    n_even, n_odd = w_in_ab.shape[0], w_qkv.shape[0]
    t_ctx, t_lat = n_ctx_seq * seq, n_lat * lat_seq
    t = t_ctx + t_lat
    rows_n = lat_seq // GRID_W
    assert seq % TOKEN_TILE == 0 and lat_seq % TOKEN_TILE == 0 and 1 + n_lat <= SUBLANES
    assert TOKEN_TILE % CHUNK == 0 and HALO >= CONV_K // 2 and ca == w_out_ab.shape[1] // 2
    assert rows_n % Q_ROWS == 0 and rows_n >= K_ROWS and t_ctx % (Q_ROWS * GRID_W) == 0
    assert n_heads * head_dim == d and 2 * head_dim == LANES and n_experts <= LANES
    assert (t * TOP_K) % MOE_TILE == 0
    assert d == SUBLANES * LANES, "MoE rows are moved as one (SUBLANES, LANES) f32 tile each"
    alpha = float((2 * depth) ** 0.25)
    scale = float(head_dim ** -0.5)
    assert np.frexp(scale)[0] == 0.5, "the attention scale is folded into the bf16 queries"
    meta = _token_meta(n_ctx_seq, seq, n_lat, lat_seq)

    x = (x_prompt.reshape(t_ctx, d), x_sample.reshape(t_lat, d))
    cvec = jnp.zeros((SUBLANES, d), F32).at[0].set(c_ctx).at[1:1 + n_lat].set(c)
    mod = _modulation(cvec, w_mod, b_mod).reshape(depth, SUBLANES, 6, d)

    pad_e = LANES - n_experts
    common = {
        "n_experts": n_experts,
        "ln_g": ln_g.reshape(depth, 2, 1, d), "ln_b": ln_b.reshape(depth, 2, 1, d),
        "w_router": jnp.pad(w_router, ((0, 0), (0, 0), (0, pad_e))).astype(BF16),
        "b_router": jnp.pad(b_router, ((0, 0), (0, pad_e)), constant_values=PAD_LOGIT).reshape(depth, 1, LANES),
    }
    even = dict(common)
    even.update({
        "w_in": w_in_ab.astype(BF16), "sgu_g": sgu_ln_g.reshape(n_even, 1, ca),
        "sgu_b": sgu_ln_b.reshape(n_even, 1, ca), "w_sp": w_spatial.astype(BF16),
        "b_sp": jnp.repeat(jnp.transpose(b_spatial, (0, 2, 1)), ca // G_A, axis=2),
        "conv_w": conv_w, "conv_b": conv_b.reshape(n_even, 1, ca),
        "cln_g": conv_ln_g.reshape(n_even, 1, ca), "cln_b": conv_ln_b.reshape(n_even, 1, ca),
        "w_out_ab": w_out_ab.astype(BF16)})
    odd = dict(common)
    odd["w_out_c"] = w_out_c.astype(BF16)
    w_qkv_bf = w_qkv.astype(BF16)
    lat_tables = _latent_window_tables(rows_n)
    n_tiles = t * TOP_K // MOE_TILE + n_experts

    ck_all = cache_k.reshape(n_lat, n_odd, -1, d)
    cv_all = cache_v.reshape(n_lat, n_odd, -1, d)
    new_kv = None
    qkv = None
    for l in range(depth):
        i = l // 2
        if l % 2 == 0:
            x1, h2, top_e, gates, counts = _even_layer(l, alpha, meta, x, mod, even)
        else:
            q, k, v = qkv
            o_ctx = _attn_ctx(q, k, v, n_ctx_seq, seq, scale)
            bias = _latent_bias(rpb[i], lat_tables[2])
            o_lat = _attn_lat(q, k, v, ck_all, cv_all, i, bias, lat_tables, t_ctx, n_lat, lat_seq, scale)
            x1, h2, top_e, gates, counts = _odd_out_proj(l, alpha, meta, x, o_ctx, o_lat, mod, odd)
        pad_end, pos = _route_tables(top_e, counts, n_experts)
        xs = _moe_dispatch(l, h2, pad_end, pos, n_tiles)
        yb = _moe_experts(l, xs, pad_end, w_gate_up, b_gate_up, w_down, b_down)
        if l % 2 == 0 and l + 1 < depth:
            x, q, k, v, k32, v32 = _moe_combine_qkv(
                l, alpha, meta, x1, yb, pos, gates, mod, common["ln_g"], common["ln_b"], w_qkv_bf,
                (n_ctx_seq, n_odd, seq), new_kv)
            qkv, new_kv = (q, k, v), (k32, v32)
        else:
            x = _moe_combine(l, alpha, meta, x1, yb, pos, gates, mod, common["ln_g"], common["ln_b"],
                             split_at=t_ctx if l == depth - 1 else None)

    y_prompt = x[0].reshape(n_ctx_seq, seq, d)
    y_sample = x[1].reshape(n_lat, lat_seq, d)
    cache_shape = (n_ctx_seq, n_odd, seq, n_heads, head_dim)
    return (y_prompt, y_sample, new_kv[0].reshape(cache_shape), new_kv[1].reshape(cache_shape))
```
